```python
import math
import jax, jax.numpy as jnp
from jax import lax
import numpy as np

D_MODEL = 1024
BATCH = 4
SEQ = 4096
DEPTH = 2
DEC_BATCH = 16
DEC_SEQ = 16
PAST_LEN = 1024

CHUNK = 64
EPS = 1e-6
NEG_INF = -1e30
N_HEADS = 8
N_KV_HEADS = 2
HEAD_DIM = 64
Q_PER_KV = N_HEADS // N_KV_HEADS
ATT_WIDTH = N_HEADS * HEAD_DIM
KV_WIDTH = N_KV_HEADS * HEAD_DIM
WINDOW = 128
WIN_CHUNKS = WINDOW // CHUNK
NUM_BUCKETS = 32
MAX_DISTANCE = 128
SSD_INNER = D_MODEL
SSD_HEAD_DIM = 64
SSD_HEADS = SSD_INNER // SSD_HEAD_DIM
SSD_GROUPS = 2
SSD_HPG = SSD_HEADS // SSD_GROUPS
SSD_STATE = 128
CONV_W = 4
CONV_DIM = SSD_INNER + 2 * SSD_GROUPS * SSD_STATE
SSD_BLOCK = CHUNK
DT_MIN = 1e-3
DT_MAX = 1e-1
D_FF = -(-8 * D_MODEL // (3 * 256)) * 256
IN_SPLITS = (ATT_WIDTH, KV_WIDTH, KV_WIDTH, SSD_INNER, CONV_DIM, SSD_HEADS, D_MODEL, D_MODEL)
IN_WIDTH = sum(IN_SPLITS)

kernel_name = 'hybrid_swa_sink_ssd_stream_encoder_step'


def rmsnorm(x, g):
    xf = x.astype(jnp.float32)
    y = xf * lax.rsqrt(jnp.mean(xf * xf, axis=-1, keepdims=True) + EPS)
    return (y * g.astype(jnp.float32)).astype(x.dtype)


def t5_bucket(rel):
    nb = NUM_BUCKETS // 2
    max_exact = nb // 2
    ret = jnp.where(rel > 0, nb, 0)
    n = jnp.abs(rel)
    nf = jnp.maximum(n, 1).astype(jnp.float32)
    large = max_exact + (jnp.log(nf / max_exact) / math.log(MAX_DISTANCE / max_exact)
                         * (nb - max_exact)).astype(jnp.int32)
    large = jnp.minimum(large, nb - 1)
    return ret + jnp.where(n < max_exact, n, large)


def rel_bias(rel, table):
    b = table[t5_bucket(rel)].astype(jnp.float32)
    return jnp.moveaxis(b, -1, 0).reshape(N_KV_HEADS, Q_PER_KV, rel.shape[0], rel.shape[1])


def sink_attention(q, k, v, bias, sinks, kmask):
    logits = jnp.einsum('bxqgrd,bxkgd->bxgrqk', q.astype(jnp.float32), k.astype(jnp.float32))
    logits = logits * (HEAD_DIM ** -0.5) + bias
    if kmask is not None:
        logits = jnp.where(kmask[:, :, None, None, None, :], logits, NEG_INF)
    s = sinks.astype(jnp.float32).reshape(1, 1, N_KV_HEADS, Q_PER_KV, 1, 1)
    m = jnp.maximum(jnp.max(logits, axis=-1, keepdims=True), s)
    p = jnp.exp(logits - m)
    denom = jnp.sum(p, axis=-1, keepdims=True) + jnp.exp(s - m)
    return jnp.einsum('bxgrqk,bxkgd->bxqgrd', p / denom, v.astype(jnp.float32))


def attn_prompt(q, k, v, table, sinks, kv_len):
    b, s = q.shape[0], q.shape[1]
    nc = s // CHUNK
    pad = WIN_CHUNKS * CHUNK
    lk = pad + CHUNK
    qb = q.reshape(b, nc, CHUNK, N_KV_HEADS, Q_PER_KV, HEAD_DIM)

    def band(t):
        tp = jnp.pad(t, ((0, 0), (pad, 0), (0, 0), (0, 0)))
        tp = tp.reshape(b, nc + WIN_CHUNKS, CHUNK, N_KV_HEADS, HEAD_DIM)
        return jnp.concatenate([tp[:, i:i + nc] for i in range(WIN_CHUNKS + 1)], axis=2)

    qi = jnp.arange(CHUNK, dtype=jnp.int32)
    kj = jnp.arange(lk, dtype=jnp.int32)
    bias = rel_bias(kj[None, :] - pad - qi[:, None], table)
    kpos = (jnp.arange(nc, dtype=jnp.int32)[:, None] - WIN_CHUNKS) * CHUNK + kj[None, :]
    o = sink_attention(qb, band(k), band(v), bias, sinks, (kpos >= 0)[None])
    return o.reshape(b, s, ATT_WIDTH), k[:, s - kv_len:], v[:, s - kv_len:]


def attn_sample(q, k, v, k_cache, v_cache, table, sinks):
    b, t = q.shape[0], q.shape[1]
    kv_len = k_cache.shape[1]
    kk = jnp.concatenate([k_cache.astype(k.dtype), k], axis=1)
    vv = jnp.concatenate([v_cache.astype(v.dtype), v], axis=1)
    qpos = PAST_LEN + jnp.arange(t, dtype=jnp.int32)
    kpos = PAST_LEN - kv_len + jnp.arange(kv_len + t, dtype=jnp.int32)
    bias = rel_bias(kpos[None, :] - qpos[:, None], table)
    o = sink_attention(q.reshape(b, 1, t, N_KV_HEADS, Q_PER_KV, HEAD_DIM),
                       kk[:, None], vv[:, None], bias, sinks, None)
    return o.reshape(b, t, ATT_WIDTH), kk[:, t:], vv[:, t:]


def causal_conv(xbc, hist, w, bias):
    xp = jnp.concatenate([hist.astype(xbc.dtype), xbc], axis=1)
    out = lax.conv_general_dilated(xp, w[:, None, :].astype(xp.dtype), window_strides=(1,),
                                   padding='VALID', dimension_numbers=('NWC', 'WIO', 'NWC'),
                                   feature_group_count=CONV_DIM)
    return jax.nn.silu(out + bias.astype(xp.dtype)), xp[:, -(CONV_W - 1):]


def ssd_scan(x, dt, a, bm, cm, h0, block):
    b, s = x.shape[0], x.shape[1]
    nc = s // block
    f32 = jnp.float32
    xr = x.astype(f32).reshape(b, nc, block, SSD_GROUPS, SSD_HPG, SSD_HEAD_DIM)
    dtr = dt.reshape(b, nc, block, SSD_GROUPS, SSD_HPG)
    br = bm.astype(f32).reshape(b, nc, block, SSD_GROUPS, SSD_STATE)
    cr = cm.astype(f32).reshape(b, nc, block, SSD_GROUPS, SSD_STATE)
    acum = jnp.cumsum(dtr * a.reshape(SSD_GROUPS, SSD_HPG), axis=2)
    ac = jnp.moveaxis(acum, 2, -1)
    causal = jnp.tril(jnp.ones((block, block), dtype=bool))
    lmat = jnp.exp(jnp.where(causal, ac[..., :, None] - ac[..., None, :], -jnp.inf))
    cb = jnp.einsum('bclgn,bcsgn->bcgls', cr, br)
    y_diag = jnp.einsum('bcgls,bcghls,bcsgh,bcsghp->bclghp', cb, lmat, dtr, xr)
    decay_end = jnp.exp(ac[..., -1:] - ac)
    st = jnp.einsum('bclgn,bcghl,bclgh,bclghp->bcghpn', br, decay_end, dtr, xr)
    blk_decay = jnp.exp(ac[..., -1])

    def step(h, inp):
        s_c, d_c = inp
        return d_c[..., None, None] * h + s_c, h

    h_init = h0.astype(f32).reshape(b, SSD_GROUPS, SSD_HPG, SSD_HEAD_DIM, SSD_STATE)
    h_fin, h_prev = lax.scan(step, h_init, (jnp.moveaxis(st, 1, 0), jnp.moveaxis(blk_decay, 1, 0)))
    h_prev = jnp.moveaxis(h_prev, 0, 1)
    y_off = jnp.einsum('bclgn,bcghpn,bcghl->bclghp', cr, h_prev, jnp.exp(ac))
    y = (y_diag + y_off).reshape(b, s, SSD_HEADS, SSD_HEAD_DIM)
    return y, h_fin.reshape(b, SSD_HEADS, SSD_HEAD_DIM, SSD_STATE)


def trunk_layer(x, attn_fn, conv_hist, ssm0, ssd_block, g_mix, w_in, conv_w, conv_b, dt_bias, a_log,
                d_skip, g_ssd, w_att_out, w_ssd_out, w_out, g_ffn, w_gate, w_up, w_down):
    b, t = x.shape[0], x.shape[1]
    f32 = jnp.float32
    hn = rmsnorm(x, g_mix)
    u = hn @ w_in
    q, k, v, z, xbc, dt_raw, gate_a, gate_s = jnp.split(u, np.cumsum(IN_SPLITS)[:-1].tolist(), axis=-1)
    o_att, k_state, v_state = attn_fn(q, k.reshape(b, t, N_KV_HEADS, HEAD_DIM),
                                      v.reshape(b, t, N_KV_HEADS, HEAD_DIM))
    xbc, conv_state = causal_conv(xbc, conv_hist, conv_w, conv_b)
    xs, bm, cm = jnp.split(xbc, [SSD_INNER, SSD_INNER + SSD_GROUPS * SSD_STATE], axis=-1)
    dt = jax.nn.softplus(dt_raw.astype(f32) + dt_bias.astype(f32))
    a = -jnp.exp(a_log.astype(f32))
    xh = xs.reshape(b, t, SSD_HEADS, SSD_HEAD_DIM)
    y, ssm_state = ssd_scan(xh, dt, a, bm.reshape(b, t, SSD_GROUPS, SSD_STATE),
                            cm.reshape(b, t, SSD_GROUPS, SSD_STATE), ssm0, ssd_block)
    y = y + xh.astype(f32) * d_skip.astype(f32)[:, None]
    y = y.reshape(b, t, SSD_INNER) * jax.nn.silu(z.astype(f32))
    y_ssd = rmsnorm(y, g_ssd).astype(x.dtype)
    merged = (jax.nn.sigmoid(gate_a) * (o_att.astype(x.dtype) @ w_att_out)
              + jax.nn.sigmoid(gate_s) * (y_ssd @ w_ssd_out))
    h = x + merged @ w_out
    hf = rmsnorm(h, g_ffn)
    out = h + (jax.nn.silu(hf @ w_gate) * (hf @ w_up)) @ w_down
    return out, (k_state, v_state, conv_state, ssm_state.astype(x.dtype))


def setup_inputs(seed: int = 0) -> dict:
    key = jax.random.key(seed)
    k = jax.random.split(key, 24)
    f32 = jnp.float32
    kv_len = min(WINDOW, PAST_LEN)

    def nrm(kk, shape, scale):
        return jax.random.normal(kk, shape, f32) * scale

    dt0 = jnp.exp(jax.random.uniform(k[10], (DEPTH, SSD_HEADS), f32, math.log(DT_MIN), math.log(DT_MAX)))
    return {
        'x_prompt': nrm(k[0], (BATCH, SEQ, D_MODEL), 1.0),
        'x_sample': nrm(k[1], (DEC_BATCH, DEC_SEQ, D_MODEL), 1.0),
        'cache_k': nrm(k[2], (DEPTH, DEC_BATCH, kv_len, N_KV_HEADS, HEAD_DIM), 1.0),
        'cache_v': nrm(k[3], (DEPTH, DEC_BATCH, kv_len, N_KV_HEADS, HEAD_DIM), 1.0),
        'state_conv': nrm(k[4], (DEPTH, DEC_BATCH, CONV_W - 1, CONV_DIM), 1.0),
        'state_ssm': nrm(k[5], (DEPTH, DEC_BATCH, SSD_HEADS, SSD_HEAD_DIM, SSD_STATE), 0.1),
        'rel_table': nrm(k[6], (NUM_BUCKETS, N_HEADS), 0.5),
        'g_mix': 1.0 + nrm(k[7], (DEPTH, D_MODEL), 0.02),
        'w_in': nrm(k[8], (DEPTH, D_MODEL, IN_WIDTH), D_MODEL ** -0.5),
        'conv_w': nrm(k[9], (DEPTH, CONV_W, CONV_DIM), CONV_W ** -0.5),
        'conv_b': nrm(k[11], (DEPTH, CONV_DIM), 0.02),
        'dt_bias': dt0 + jnp.log(-jnp.expm1(-dt0)),
        'a_log': jnp.log(jax.random.uniform(k[12], (DEPTH, SSD_HEADS), f32, 1.0, 16.0)),
        'd_skip': 1.0 + nrm(k[13], (DEPTH, SSD_HEADS), 0.1),
        'g_ssd': 1.0 + nrm(k[14], (DEPTH, SSD_INNER), 0.02),
        'sinks': nrm(k[15], (DEPTH, N_HEADS), 0.5),
        'w_att_out': nrm(k[16], (DEPTH, ATT_WIDTH, D_MODEL), ATT_WIDTH ** -0.5),
        'w_ssd_out': nrm(k[17], (DEPTH, SSD_INNER, D_MODEL), SSD_INNER ** -0.5),
        'w_out': nrm(k[18], (DEPTH, D_MODEL, D_MODEL), D_MODEL ** -0.5),
        'g_ffn': 1.0 + nrm(k[19], (DEPTH, D_MODEL), 0.02),
        'w_gate': nrm(k[20], (DEPTH, D_MODEL, D_FF), D_MODEL ** -0.5),
        'w_up': nrm(k[21], (DEPTH, D_MODEL, D_FF), D_MODEL ** -0.5),
        'w_down': nrm(k[22], (DEPTH, D_FF, D_MODEL), D_FF ** -0.5),
        'g_final': 1.0 + nrm(k[23], (D_MODEL,), 0.02),
    }


def reference(x_prompt, x_sample, cache_k, cache_v, state_conv, state_ssm, rel_table, g_mix, w_in,
              conv_w, conv_b, dt_bias, a_log, d_skip, g_ssd, sinks, w_att_out, w_ssd_out, w_out,
              g_ffn, w_gate, w_up, w_down, g_final):
    kv_len = cache_k.shape[2]
    xp, xs = x_prompt, x_sample
    bp = xp.shape[0]
    st_p, st_s = [], []
    for l in range(DEPTH):
        lw = (g_mix[l], w_in[l], conv_w[l], conv_b[l], dt_bias[l], a_log[l], d_skip[l], g_ssd[l],
              w_att_out[l], w_ssd_out[l], w_out[l], g_ffn[l], w_gate[l], w_up[l], w_down[l])
        sl = sinks[l]
        xp, sp = trunk_layer(
            xp, lambda q, k, v, sl=sl: attn_prompt(q, k, v, rel_table, sl, kv_len),
            jnp.zeros((bp, CONV_W - 1, CONV_DIM), xp.dtype),
            jnp.zeros((bp, SSD_HEADS, SSD_HEAD_DIM, SSD_STATE), jnp.float32),
            SSD_BLOCK, *lw)
        xs, ss = trunk_layer(
            xs, lambda q, k, v, sl=sl, kc=cache_k[l], vc=cache_v[l]: attn_sample(q, k, v, kc, vc, rel_table, sl),
            state_conv[l], state_ssm[l], xs.shape[1], *lw)
        st_p.append(sp)
        st_s.append(ss)
    y_prompt = rmsnorm(xp, g_final)
    y_sample = rmsnorm(xs, g_final)
    k_prompt = jnp.stack([s[0] for s in st_p])
    v_prompt = jnp.stack([s[1] for s in st_p])
    conv_prompt = jnp.stack([s[2] for s in st_p])
    ssm_prompt = jnp.stack([s[3] for s in st_p])
    k_sample = jnp.stack([s[0] for s in st_s])
    v_sample = jnp.stack([s[1] for s in st_s])
    conv_sample = jnp.stack([s[2] for s in st_s])
    ssm_sample = jnp.stack([s[3] for s in st_s])
    return (y_prompt, y_sample, k_prompt, v_prompt, conv_prompt, ssm_prompt,
            k_sample, v_sample, conv_sample, ssm_sample)
```

```python
import functools
import math

import jax
import jax.numpy as jnp
from jax import lax
from jax.experimental import pallas as pl
from jax.experimental.pallas import tpu as pltpu

D_MODEL = 1024
CHUNK = 64
EPS = 1e-6
NEG_INF = -1e30
N_HEADS = 8
N_KV_HEADS = 2
HEAD_DIM = 64
ATT_WIDTH = N_HEADS * HEAD_DIM
KV_WIDTH = N_KV_HEADS * HEAD_DIM
WINDOW = 128
N_KEYS = WINDOW + CHUNK
NUM_BUCKETS = 32
MAX_DISTANCE = 128
SSD_INNER = 1024
SSD_HEADS = 16
SSD_HEAD_DIM = 64
SSD_GROUPS = 2
SSD_STATE = 128
GROUP_WIDTH = SSD_INNER // SSD_GROUPS
CONV_W = 4
CONV_DIM = SSD_INNER + 2 * SSD_GROUPS * SSD_STATE
D_FF = 2816
LANES = 128
HIST_ROWS = 8

OFF_Q = 0
OFF_KV = OFF_Q + ATT_WIDTH
OFF_Z = OFF_KV + 2 * KV_WIDTH
OFF_XBC = OFF_Z + SSD_INNER
OFF_DT = OFF_XBC + CONV_DIM
OFF_GA = OFF_DT + LANES
OFF_GS = OFF_GA + D_MODEL
IN_PACKED = OFF_GS + D_MODEL

VMEM_LIMIT_BYTES = 60 * 1024 * 1024

F32 = jnp.float32
BF16 = jnp.bfloat16


def _dot(a, b):
    return jnp.dot(a, b, preferred_element_type=F32)


def _dot_nt(a, b):
    return lax.dot_general(a, b, (((1,), (1,)), ((), ())), preferred_element_type=F32)


def _split3(x):
    hi = x.astype(BF16)
    r1 = x - hi.astype(F32)
    mid = r1.astype(BF16)
    lo = (r1 - mid.astype(F32)).astype(BF16)
    return hi, mid, lo


def _exact_dot_right(x, sel):
    hi, mid, lo = _split3(x)
    return _dot(hi, sel) + _dot(mid, sel) + _dot(lo, sel)


def _exact_dot_left(sel, x):
    hi, mid, lo = _split3(x)
    return _dot(sel, hi) + _dot(sel, mid) + _dot(sel, lo)


def _rmsnorm(x, g):
    return x * lax.rsqrt(jnp.mean(x * x, axis=-1, keepdims=True) + EPS) * g


def _sigmoid(x):
    return 1.0 / (1.0 + jnp.exp(-x))


def _silu(x):
    return x * _sigmoid(x)


def _softplus(x):
    return jnp.maximum(x, 0.0) + jnp.log1p(jnp.exp(-jnp.abs(x)))


def _bias_kernel(table_ref, bucket_ref, o_ref):
    bucket = bucket_ref[...]
    rows = lax.broadcasted_iota(jnp.int32, bucket.shape, 0)
    cols = lax.broadcasted_iota(jnp.int32, bucket.shape, 1)
    row_hi = rows >= CHUNK
    col_hi = cols >= N_KEYS
    for g in range(N_KV_HEADS):
        acc = jnp.zeros(bucket.shape, F32)
        for b in range(NUM_BUCKETS):
            t0 = table_ref[b, 4 * g + 0]
            t1 = table_ref[b, 4 * g + 1]
            t2 = table_ref[b, 4 * g + 2]
            t3 = table_ref[b, 4 * g + 3]
            tv = jnp.where(row_hi, jnp.where(col_hi, t3, t2), jnp.where(col_hi, t1, t0))
            acc = jnp.where(bucket == b, tv, acc)
        o_ref[g] = acc


def _t5_bucket(rel):
    nb = NUM_BUCKETS // 2
    max_exact = nb // 2
    ret = jnp.where(rel > 0, nb, 0)
    n = jnp.abs(rel)
    nf = jnp.maximum(n, 1).astype(jnp.float32)
    large = max_exact + (jnp.log(nf / max_exact) / math.log(MAX_DISTANCE / max_exact)
                         * (nb - max_exact)).astype(jnp.int32)
    large = jnp.minimum(large, nb - 1)
    return ret + jnp.where(n < max_exact, n, large)


def _blocked_bias(rel_table):
    qi = jnp.arange(CHUNK, dtype=jnp.int32)
    kj = jnp.arange(N_KEYS, dtype=jnp.int32)
    bucket = _t5_bucket(kj[None, :] - WINDOW - qi[:, None]).astype(jnp.int32)
    bucket = jnp.tile(bucket, (2, 2))
    return pl.pallas_call(
        _bias_kernel,
        out_shape=jax.ShapeDtypeStruct((N_KV_HEADS, 2 * CHUNK, 2 * N_KEYS), F32),
        in_specs=[pl.BlockSpec(memory_space=pltpu.SMEM),
                  pl.BlockSpec(memory_space=pltpu.VMEM)],
        out_specs=pl.BlockSpec(memory_space=pltpu.VMEM),
        name="rel_bias",
    )(rel_table, bucket)


def _attention_chunk(q, kwin, vwin, bias_ref, sinks_ref, valid):
    lane = lax.broadcasted_iota(jnp.int32, (N_KEYS, LANES), 1)
    lo_half = lane < HEAD_DIM
    cols = lax.broadcasted_iota(jnp.int32, (2 * CHUNK, 2 * N_KEYS), 1)
    head0 = cols < N_KEYS
    rows1 = lax.broadcasted_iota(jnp.int32, (2 * CHUNK, 1), 0)
    row_lo = rows1 < CHUNK
    lane_o = lax.broadcasted_iota(jnp.int32, (2 * CHUNK, LANES), 1) < HEAD_DIM
    outs = []
    for g in range(N_KV_HEADS):
        def blocked(win):
            if g == 0:
                a0 = jnp.where(lo_half, win, 0.0)
                a1 = pltpu.roll(a0, HEAD_DIM, axis=1)
            else:
                a1 = jnp.where(lo_half, 0.0, win)
                a0 = pltpu.roll(a1, HEAD_DIM, axis=1)
            return jnp.concatenate([a0, a1], axis=0).astype(BF16)

        kblk = blocked(kwin)
        vblk = blocked(vwin)
        qs = jnp.concatenate([q[:, 256 * g:256 * g + LANES],
                              q[:, 256 * g + LANES:256 * (g + 1)]], axis=0).astype(BF16)
        logits = _dot_nt(qs, kblk) + bias_ref[g]
        logits = jnp.where(valid, logits, NEG_INF)
        s0 = jnp.where(row_lo, sinks_ref[4 * g + 0], sinks_ref[4 * g + 2])
        s1 = jnp.where(row_lo, sinks_ref[4 * g + 1], sinks_ref[4 * g + 3])
        m0 = jnp.max(jnp.where(head0, logits, -jnp.inf), axis=1, keepdims=True)
        m1 = jnp.max(jnp.where(head0, -jnp.inf, logits), axis=1, keepdims=True)
        m0 = jnp.maximum(m0, s0)
        m1 = jnp.maximum(m1, s1)
        p = jnp.exp(logits - jnp.where(head0, m0, m1))
        d0 = jnp.sum(jnp.where(head0, p, 0.0), axis=1, keepdims=True) + jnp.exp(s0 - m0)
        d1 = jnp.sum(jnp.where(head0, 0.0, p), axis=1, keepdims=True) + jnp.exp(s1 - m1)
        o = _dot(p.astype(BF16), vblk) / jnp.where(lane_o, d0, d1)
        outs += [o[:CHUNK], o[CHUNK:]]
    return jnp.concatenate(outs, axis=1)


def _ssd_chunk(xbc_act, dt_raw, ht_ref, g_idx, dtb_ref, a_ref, dskip_ref, nvalid):
    xs = xbc_act[:, :SSD_INNER]
    bm = xbc_act[:, SSD_INNER:SSD_INNER + SSD_GROUPS * SSD_STATE]
    cm = xbc_act[:, SSD_INNER + SSD_GROUPS * SSD_STATE:]

    dt = _softplus(dt_raw + dtb_ref[...])
    if nvalid < CHUNK:
        rows = lax.broadcasted_iota(jnp.int32, dt.shape, 0)
        dt = jnp.where(rows < nvalid, dt, 0.0)
    da = dt * a_ref[...]

    r = lax.broadcasted_iota(jnp.int32, (CHUNK, CHUNK), 0)
    c = lax.broadcasted_iota(jnp.int32, (CHUNK, CHUNK), 1)
    tri = jnp.where(c <= r, 1.0, 0.0).astype(BF16)
    acum = _exact_dot_left(tri, da)

    hrow = lax.broadcasted_iota(jnp.int32, (LANES, SSD_INNER), 0)
    hcol = lax.broadcasted_iota(jnp.int32, (LANES, SSD_INNER), 1)
    expand = jnp.where(hcol >> 6 == hrow, 1.0, 0.0).astype(BF16)
    dt_exp = _exact_dot_right(dt, expand)
    a_col = _exact_dot_right(acum, expand)
    a_last = a_col[CHUNK - 1:CHUNK, :]
    xdt = xs * dt_exp
    xw = xdt * jnp.exp(a_last - a_col)
    e_col = jnp.exp(a_col)
    e_last = jnp.exp(a_last)

    l2 = lax.broadcasted_iota(jnp.int32, (CHUNK, LANES), 0)
    j2 = lax.broadcasted_iota(jnp.int32, (CHUNK, LANES), 1)
    s2 = j2 & (CHUNK - 1)
    diag_sel = jnp.where(s2 == l2, 1.0, 0.0)
    causal2 = s2 <= l2
    lane_lo = j2 < SSD_HEAD_DIM

    ys = []
    for gq in range(SSD_GROUPS):
        bg = bm[:, SSD_STATE * gq:SSD_STATE * (gq + 1)]
        cg = cm[:, SSD_STATE * gq:SSD_STATE * (gq + 1)].astype(BF16)
        bg_bf = bg.astype(BF16)
        cb2 = _dot_nt(cg, jnp.concatenate([bg_bf, bg_bf], axis=0))
        gsl = slice(GROUP_WIDTH * gq, GROUP_WIDTH * (gq + 1))
        h_prev = ht_ref[g_idx, :, gsl]
        y_off = _dot(cg, h_prev.astype(BF16)) * e_col[:, gsl]
        yd = []
        for i in range(GROUP_WIDTH // LANES):
            psl = slice(GROUP_WIDTH * gq + LANES * i, GROUP_WIDTH * gq + LANES * (i + 1))
            ac = a_col[:, psl]
            a_row = jnp.sum(ac * diag_sel, axis=0, keepdims=True)
            lmat = jnp.exp(jnp.where(causal2, ac - a_row, -jnp.inf))
            m2 = (cb2 * lmat).astype(BF16)
            xp = xdt[:, psl]
            xblk = jnp.concatenate([jnp.where(lane_lo, xp, 0.0), jnp.where(lane_lo, 0.0, xp)],
                                   axis=0).astype(BF16)
            yd.append(_dot(m2, xblk))
        ys.append(jnp.concatenate(yd, axis=1) + y_off)
        st = _dot(bg.T.astype(BF16), xw[:, gsl].astype(BF16))
        ht_ref[g_idx, :, gsl] = h_prev * e_last[:, gsl] + st
    return jnp.concatenate(ys, axis=1) + xs * dskip_ref[...]


def _mixer_kernel(*refs, n_seq, n_chunk, n_tiles, nvalid, has_init):
    it = iter(refs)
    x_ref = next(it)
    bias_ref = next(it)
    sinks_ref = next(it)
    gmix_ref = next(it)
    win_ref = next(it)
    convw_ref = next(it)
    convb_ref = next(it)
    dtb_ref = next(it)
    a_ref = next(it)
    dskip_ref = next(it)
    gssd_ref = next(it)
    watt_ref = next(it)
    wssd_ref = next(it)
    wout_ref = next(it)
    if has_init:
        kvi_ref = next(it)
        convi_ref = next(it)
        ssmi_ref = next(it)
    h_ref = next(it)
    kvo_ref = next(it)
    convo_ref = next(it)
    ssmo_ref = next(it)
    hn_s = next(it)
    q_s = next(it)
    kv_s = next(it)
    xbc_s = next(it)
    dt_s = next(it)
    oatt_s = next(it)
    y_s = next(it)
    ht_s = next(it)

    t = pl.program_id(1)
    seq_rows = n_chunk * CHUNK

    @pl.when(t == 0)
    def _init():
        if has_init:
            kv_s[:, 0:WINDOW, :] = kvi_ref[...]
            xbc_s[:, 0:HIST_ROWS, :] = convi_ref[...]
            for g in range(n_seq):
                ht_s[g] = ssmi_ref[g].T
        else:
            kv_s[:, 0:WINDOW, :] = jnp.zeros((n_seq, WINDOW, 2 * KV_WIDTH), F32)
            xbc_s[:, 0:HIST_ROWS, :] = jnp.zeros((n_seq, HIST_ROWS, CONV_DIM), F32)
            ht_s[...] = jnp.zeros(ht_s.shape, F32)

    hn = _rmsnorm(x_ref[...], gmix_ref[...]).astype(BF16)
    hn_s[...] = hn
    q_s[...] = _dot(hn, win_ref[:, OFF_Q:OFF_KV]) * (HEAD_DIM ** -0.5)
    kv = _dot(hn, win_ref[:, OFF_KV:OFF_Z])
    xbc = _dot(hn, win_ref[:, OFF_XBC:OFF_DT])
    for g in range(n_seq):
        kv_s[g, WINDOW:WINDOW + seq_rows, :] = kv[g * seq_rows:(g + 1) * seq_rows]
        xbc_s[g, HIST_ROWS:HIST_ROWS + seq_rows, :] = xbc[g * seq_rows:(g + 1) * seq_rows]
    dt_s[...] = _dot(hn, win_ref[:, OFF_DT:OFF_GA])

    cols = lax.broadcasted_iota(jnp.int32, (2 * CHUNK, 2 * N_KEYS), 1)
    colmod = jnp.where(cols >= N_KEYS, cols - N_KEYS, cols)

    def chunk_body(idx, carry):
        if n_seq == 1:
            g, j = 0, idx
        else:
            g, j = idx, 0
        r0 = pl.multiple_of(idx * CHUNK, CHUNK)
        k0 = r0 if n_seq == 1 else 0
        if has_init:
            valid = colmod < WINDOW + nvalid
        else:
            first_valid = jnp.maximum(0, (2 - (t * n_chunk + j)) * CHUNK)
            valid = colmod >= first_valid
        kvwin = kv_s[g, pl.ds(k0, N_KEYS), :]
        oatt_s[pl.ds(r0, CHUNK), :] = _attention_chunk(
            q_s[pl.ds(r0, CHUNK), :], kvwin[:, :KV_WIDTH], kvwin[:, KV_WIDTH:], bias_ref,
            sinks_ref, valid)
        win = xbc_s[g, pl.ds(k0, HIST_ROWS + CHUNK), :]
        conv = convb_ref[...] + convw_ref[3:4, :] * win[HIST_ROWS:HIST_ROWS + CHUNK]
        for i in range(1, CONV_W):
            conv = conv + convw_ref[3 - i:4 - i, :] * win[HIST_ROWS - i:HIST_ROWS - i + CHUNK]
        y_s[pl.ds(r0, CHUNK), :] = _ssd_chunk(_silu(conv), dt_s[pl.ds(r0, CHUNK), :], ht_s, g,
                                              dtb_ref, a_ref, dskip_ref, nvalid)
        return carry

    lax.fori_loop(0, n_seq * n_chunk, chunk_body, 0)

    hn = hn_s[...]
    z = _dot(hn, win_ref[:, OFF_Z:OFF_XBC])
    y_ssd = _rmsnorm(y_s[...] * _silu(z), gssd_ref[...]).astype(BF16)
    merged = (_sigmoid(_dot(hn, win_ref[:, OFF_GA:OFF_GS])) * _dot(oatt_s[...].astype(BF16), watt_ref[...])
              + _sigmoid(_dot(hn, win_ref[:, OFF_GS:IN_PACKED])) * _dot(y_ssd, wssd_ref[...]))
    h_ref[...] = x_ref[...] + _dot(merged.astype(BF16), wout_ref[...])

    @pl.when(t == n_tiles - 1)
    def _emit_states():
        kvo_ref[...] = kv_s[:, nvalid:nvalid + WINDOW, :]
        convo_ref[...] = xbc_s[:, nvalid:nvalid + HIST_ROWS, :]
        for g in range(n_seq):
            ssmo_ref[g] = ht_s[g].T

    if n_tiles > 1:
        kv_s[:, 0:WINDOW, :] = kv_s[:, seq_rows:seq_rows + WINDOW, :]
        xbc_s[:, 0:HIST_ROWS, :] = xbc_s[:, seq_rows:seq_rows + HIST_ROWS, :]


def _const_spec(shape):
    nd = len(shape)
    return pl.BlockSpec(shape, lambda b, t, _nd=nd: (0,) * _nd, pipeline_mode=pl.Buffered(1))


def _mixer(x2d, bias, lw, init, *, batch, seq_pad, n_seq, n_chunk, nvalid):
    seq_rows = n_chunk * CHUNK
    n_tiles = seq_pad // seq_rows
    assert n_seq == 1 or n_tiles == 1
    assert batch % n_seq == 0 and seq_pad % seq_rows == 0
    rows = n_seq * seq_rows
    has_init = init is not None
    last_valid = nvalid - (n_tiles - 1) * seq_rows

    tile_map = lambda b, t: (b * n_tiles + t, 0)
    seq_map = lambda b, t: (b, 0, 0)
    in_specs = [
        pl.BlockSpec((rows, D_MODEL), tile_map),
        _const_spec(bias.shape),
        pl.BlockSpec(memory_space=pltpu.SMEM),
        _const_spec((1, D_MODEL)),
        _const_spec((D_MODEL, IN_PACKED)),
        _const_spec((CONV_W, CONV_DIM)),
        _const_spec((1, CONV_DIM)),
        _const_spec((1, LANES)),
        _const_spec((1, LANES)),
        _const_spec((1, SSD_INNER)),
        _const_spec((1, SSD_INNER)),
        _const_spec((ATT_WIDTH, D_MODEL)),
        _const_spec((SSD_INNER, D_MODEL)),
        _const_spec((D_MODEL, D_MODEL)),
    ]
    args = [x2d, bias, lw["sinks"], lw["g_mix"], lw["w_in"], lw["conv_w"], lw["conv_b"], lw["dt_bias"],
            lw["a"], lw["d_skip"], lw["g_ssd"], lw["w_att_out"], lw["w_ssd_out"], lw["w_out"]]
    if has_init:
        in_specs += [
            pl.BlockSpec((n_seq, WINDOW, 2 * KV_WIDTH), seq_map),
            pl.BlockSpec((n_seq, HIST_ROWS, CONV_DIM), seq_map),
            pl.BlockSpec((n_seq, SSD_INNER, SSD_STATE), seq_map),
        ]
        args += list(init)
    out_shape = [
        jax.ShapeDtypeStruct((batch * seq_pad, D_MODEL), F32),
        jax.ShapeDtypeStruct((batch, WINDOW, 2 * KV_WIDTH), F32),
        jax.ShapeDtypeStruct((batch, HIST_ROWS, CONV_DIM), F32),
        jax.ShapeDtypeStruct((batch, SSD_INNER, SSD_STATE), F32),
    ]
    out_specs = [
        pl.BlockSpec((rows, D_MODEL), tile_map),
        pl.BlockSpec((n_seq, WINDOW, 2 * KV_WIDTH), seq_map),
        pl.BlockSpec((n_seq, HIST_ROWS, CONV_DIM), seq_map),
        pl.BlockSpec((n_seq, SSD_INNER, SSD_STATE), seq_map),
    ]
    scratch = [
        pltpu.VMEM((rows, D_MODEL), BF16),
        pltpu.VMEM((rows, ATT_WIDTH), F32),
        pltpu.VMEM((n_seq, WINDOW + seq_rows, 2 * KV_WIDTH), F32),
        pltpu.VMEM((n_seq, HIST_ROWS + seq_rows, CONV_DIM), F32),
        pltpu.VMEM((rows, LANES), F32),
        pltpu.VMEM((rows, ATT_WIDTH), F32),
        pltpu.VMEM((rows, SSD_INNER), F32),
        pltpu.VMEM((n_seq, SSD_STATE, SSD_INNER), F32),
    ]
    kern = functools.partial(_mixer_kernel, n_seq=n_seq, n_chunk=n_chunk, n_tiles=n_tiles,
                             nvalid=last_valid, has_init=has_init)
    return pl.pallas_call(
        kern,
        grid=(batch // n_seq, n_tiles),
        in_specs=in_specs,
        out_specs=out_specs,
        out_shape=out_shape,
        scratch_shapes=scratch,
        compiler_params=pltpu.CompilerParams(
            dimension_semantics=("arbitrary", "arbitrary"), vmem_limit_bytes=VMEM_LIMIT_BYTES),
        name="mixer_init" if has_init else "mixer",
    )(*args)


def _ffn_kernel(h_ref, g_ref, wg_ref, wu_ref, wd_ref, gfin_ref, o_ref, *, final):
    h = h_ref[...]
    hf = _rmsnorm(h, g_ref[...]).astype(BF16)
    act = (_silu(_dot(hf, wg_ref[...])) * _dot(hf, wu_ref[...])).astype(BF16)
    out = h + _dot(act, wd_ref[...])
    if final:
        out = _rmsnorm(out, gfin_ref[...])
    o_ref[...] = out


def _ffn(h2d, lw, g_final, *, block_rows, final):
    n_rows = h2d.shape[0]
    assert n_rows % block_rows == 0
    const = lambda shape: pl.BlockSpec(shape, lambda i: (0, 0), pipeline_mode=pl.Buffered(1))
    return pl.pallas_call(
        functools.partial(_ffn_kernel, final=final),
        grid=(n_rows // block_rows,),
        in_specs=[
            pl.BlockSpec((block_rows, D_MODEL), lambda i: (i, 0)),
            const((1, D_MODEL)),
            const((D_MODEL, D_FF)),
            const((D_MODEL, D_FF)),
            const((D_FF, D_MODEL)),
            const((1, D_MODEL)),
        ],
        out_specs=pl.BlockSpec((block_rows, D_MODEL), lambda i: (i, 0)),
        out_shape=jax.ShapeDtypeStruct((n_rows, D_MODEL), F32),
        compiler_params=pltpu.CompilerParams(
            dimension_semantics=("arbitrary",), vmem_limit_bytes=VMEM_LIMIT_BYTES),
        name="ffn_final" if final else "ffn",
    )(h2d, lw["g_ffn"], lw["w_gate"], lw["w_up"], lw["w_down"], g_final)


def _pack_w_in(w):
    o = 0
    parts = {}
    for name, width in (("q", ATT_WIDTH), ("k", KV_WIDTH), ("v", KV_WIDTH), ("z", SSD_INNER),
                        ("xbc", CONV_DIM), ("dt", SSD_HEADS), ("ga", D_MODEL), ("gs", D_MODEL)):
        parts[name] = w[:, o:o + width]
        o += width
    dt = jnp.pad(parts["dt"], ((0, 0), (0, LANES - SSD_HEADS)))
    return jnp.concatenate([parts["q"], parts["k"], parts["v"], parts["z"], parts["xbc"], dt,
                            parts["ga"], parts["gs"]], axis=1).astype(BF16)


def _layer_weights(l, g_mix, w_in, conv_w, conv_b, dt_bias, a_log, d_skip, g_ssd, sinks, w_att_out,
                   w_ssd_out, w_out, g_ffn, w_gate, w_up, w_down):
    pad_h = ((0, LANES - SSD_HEADS),)
    return {
        "sinks": sinks[l],
        "g_mix": g_mix[l][None, :],
        "w_in": _pack_w_in(w_in[l]),
        "conv_w": conv_w[l],
        "conv_b": conv_b[l][None, :],
        "dt_bias": jnp.pad(dt_bias[l], pad_h)[None, :],
        "a": jnp.pad(-jnp.exp(a_log[l]), pad_h)[None, :],
        "d_skip": jnp.repeat(d_skip[l], SSD_HEAD_DIM)[None, :],
        "g_ssd": g_ssd[l][None, :],
        "w_att_out": w_att_out[l].astype(BF16),
        "w_ssd_out": w_ssd_out[l].astype(BF16),
        "w_out": w_out[l].astype(BF16),
        "g_ffn": g_ffn[l][None, :],
        "w_gate": w_gate[l].astype(BF16),
        "w_up": w_up[l].astype(BF16),
        "w_down": w_down[l].astype(BF16),
    }


PROMPT_CHUNKS_PER_TILE = 8
SAMPLE_SEQS_PER_TILE = 4
FFN_BLOCK_ROWS = 512


def kernel(x_prompt, x_sample, cache_k, cache_v, state_conv, state_ssm, rel_table, g_mix, w_in, conv_w, conv_b, dt_bias, a_log, d_skip, g_ssd, sinks, w_att_out, w_ssd_out, w_out, g_ffn, w_gate, w_up, w_down, g_final):
    depth = w_in.shape[0]
    bp, sp, _ = x_prompt.shape
    bs, ts, _ = x_sample.shape
    kv_len = cache_k.shape[2]
    assert kv_len == WINDOW and ts <= CHUNK and ts % 8 == 0 and ts >= CONV_W - 1

    bias = _blocked_bias(rel_table)
    g_fin = g_final[None, :]

    xp = x_prompt.reshape(bp * sp, D_MODEL)
    xs = x_sample.reshape(bs * ts, D_MODEL)
    st_p, st_s = [], []
    for l in range(depth):
        lw = _layer_weights(l, g_mix, w_in, conv_w, conv_b, dt_bias, a_log, d_skip, g_ssd, sinks,
                            w_att_out, w_ssd_out, w_out, g_ffn, w_gate, w_up, w_down)
        final = l == depth - 1
        hp, kvp, convp, ssmp = _mixer(xp, bias, lw, None, batch=bp, seq_pad=sp, n_seq=1,
                                      n_chunk=PROMPT_CHUNKS_PER_TILE, nvalid=sp)
        xp = _ffn(hp, lw, g_fin, block_rows=FFN_BLOCK_ROWS, final=final)
        st_p.append((kvp, convp, ssmp))
        xs_pad = jnp.pad(xs.reshape(bs, ts, D_MODEL), ((0, 0), (0, CHUNK - ts), (0, 0)))
        init = (
            jnp.concatenate([cache_k[l].reshape(bs, WINDOW, KV_WIDTH),
                             cache_v[l].reshape(bs, WINDOW, KV_WIDTH)], axis=-1),
            jnp.pad(state_conv[l], ((0, 0), (HIST_ROWS - (CONV_W - 1), 0), (0, 0))),
            state_ssm[l].reshape(bs, SSD_INNER, SSD_STATE),
        )
        hs, kvs, convs, ssms = _mixer(xs_pad.reshape(bs * CHUNK, D_MODEL), bias, lw, init, batch=bs,
                                      seq_pad=CHUNK, n_seq=SAMPLE_SEQS_PER_TILE, n_chunk=1, nvalid=ts)
        hs = hs.reshape(bs, CHUNK, D_MODEL)[:, :ts].reshape(bs * ts, D_MODEL)
        xs = _ffn(hs, lw, g_fin, block_rows=bs * ts, final=final)
        st_s.append((kvs, convs, ssms))

    def states(sts, b):
        kv = jnp.stack([s[0] for s in sts])
        k = kv[..., :KV_WIDTH].reshape(depth, b, WINDOW, N_KV_HEADS, HEAD_DIM)
        v = kv[..., KV_WIDTH:].reshape(depth, b, WINDOW, N_KV_HEADS, HEAD_DIM)
        conv = jnp.stack([s[1] for s in sts])[:, :, HIST_ROWS - (CONV_W - 1):, :]
        ssm = jnp.stack([s[2] for s in sts]).reshape(depth, b, SSD_HEADS, SSD_HEAD_DIM, SSD_STATE)
        return k, v, conv, ssm

    kp, vp, cp, sp_state = states(st_p, bp)
    ks, vs, cs, ss_state = states(st_s, bs)
    return (xp.reshape(x_prompt.shape), xs.reshape(x_sample.shape), kp, vp, cp, sp_state,
            ks, vs, cs, ss_state)
```

```python
import functools
import math

import jax
import jax.numpy as jnp
from jax import lax
from jax.experimental import pallas as pl
from jax.experimental.pallas import tpu as pltpu

D_MODEL = 1024
CHUNK = 64
EPS = 1e-6
NEG_INF = -1e30
N_HEADS = 8
N_KV_HEADS = 2
HEAD_DIM = 64
ATT_WIDTH = N_HEADS * HEAD_DIM
KV_WIDTH = N_KV_HEADS * HEAD_DIM
WINDOW = 128
N_KEYS = WINDOW + CHUNK
NUM_BUCKETS = 32
MAX_DISTANCE = 128
SSD_INNER = 1024
SSD_HEADS = 16
SSD_HEAD_DIM = 64
SSD_GROUPS = 2
SSD_STATE = 128
GROUP_WIDTH = SSD_INNER // SSD_GROUPS
CONV_W = 4
CONV_DIM = SSD_INNER + 2 * SSD_GROUPS * SSD_STATE
D_FF = 2816
LANES = 128
HIST_ROWS = 8
MXU_WIDTH = 256
CHUNKS_PER_ITER = 2
FILL_POINTS = 6

OFF_Q = 0
OFF_KV = OFF_Q + ATT_WIDTH
OFF_XBC = OFF_KV + 2 * KV_WIDTH
OFF_DT = OFF_XBC + CONV_DIM
IN_PACKED = OFF_DT + LANES
GATE_Z = 0
GATE_A = GATE_Z + SSD_INNER
GATE_S = GATE_A + D_MODEL
GATE_WIDTH = GATE_S + D_MODEL

VMEM_LIMIT_BYTES = 60 * 1024 * 1024

F32 = jnp.float32
BF16 = jnp.bfloat16


def _dot(a, b):
    return jnp.dot(a, b, preferred_element_type=F32)


def _dot_nt(a, b):
    return lax.dot_general(a, b, (((1,), (1,)), ((), ())), preferred_element_type=F32)


def _split3(x):
    hi = x.astype(BF16)
    r1 = x - hi.astype(F32)
    mid = r1.astype(BF16)
    lo = (r1 - mid.astype(F32)).astype(BF16)
    return hi, mid, lo


def _exact_dot_right(x, sel, passes=3):
    parts = _split3(x)[:passes]
    out = _dot(parts[0], sel)
    for part in parts[1:]:
        out = out + _dot(part, sel)
    return out


def _exact_dot_left(sel, x):
    hi, mid, lo = _split3(x)
    return _dot(sel, hi) + _dot(sel, mid) + _dot(sel, lo)


def _rmsnorm(x, g):
    return x * lax.rsqrt(jnp.mean(x * x, axis=-1, keepdims=True) + EPS) * g


def _sigmoid(x):
    return 1.0 / (1.0 + jnp.exp(-x))


def _silu(x):
    return x * _sigmoid(x)


def _softplus(x):
    return jnp.maximum(x, 0.0) + jnp.log1p(jnp.exp(-jnp.abs(x)))


def _bias_kernel(table_ref, bucket_ref, o_ref):
    bucket = bucket_ref[...]
    rows = lax.broadcasted_iota(jnp.int32, bucket.shape, 0)
    cols = lax.broadcasted_iota(jnp.int32, bucket.shape, 1)
    row_hi = rows >= CHUNK
    col_hi = cols >= N_KEYS
    for g in range(N_KV_HEADS):
        acc = jnp.zeros(bucket.shape, F32)
        for b in range(NUM_BUCKETS):
            t0 = table_ref[b, 4 * g + 0]
            t1 = table_ref[b, 4 * g + 1]
            t2 = table_ref[b, 4 * g + 2]
            t3 = table_ref[b, 4 * g + 3]
            tv = jnp.where(row_hi, jnp.where(col_hi, t3, t2), jnp.where(col_hi, t1, t0))
            acc = jnp.where(bucket == b, tv, acc)
        o_ref[g] = acc


def _t5_bucket(rel):
    nb = NUM_BUCKETS // 2
    max_exact = nb // 2
    ret = jnp.where(rel > 0, nb, 0)
    n = jnp.abs(rel)
    nf = jnp.maximum(n, 1).astype(jnp.float32)
    large = max_exact + (jnp.log(nf / max_exact) / math.log(MAX_DISTANCE / max_exact)
                         * (nb - max_exact)).astype(jnp.int32)
    large = jnp.minimum(large, nb - 1)
    return ret + jnp.where(n < max_exact, n, large)


def _blocked_bias(rel_table):
    qi = jnp.arange(CHUNK, dtype=jnp.int32)
    kj = jnp.arange(N_KEYS, dtype=jnp.int32)
    bucket = _t5_bucket(kj[None, :] - WINDOW - qi[:, None]).astype(jnp.int32)
    bucket = jnp.tile(bucket, (2, 2))
    return pl.pallas_call(
        _bias_kernel,
        out_shape=jax.ShapeDtypeStruct((N_KV_HEADS, 2 * CHUNK, 2 * N_KEYS), F32),
        in_specs=[pl.BlockSpec(memory_space=pltpu.SMEM),
                  pl.BlockSpec(memory_space=pltpu.VMEM)],
        out_specs=pl.BlockSpec(memory_space=pltpu.VMEM),
        name="rel_bias",
    )(rel_table, bucket)


def _blocked_kv(win, g):
    lo_half = lax.broadcasted_iota(jnp.int32, (N_KEYS, LANES), 1) < HEAD_DIM
    if g == 0:
        a0 = jnp.where(lo_half, win, 0.0)
        a1 = pltpu.roll(a0, HEAD_DIM, axis=1)
    else:
        a1 = jnp.where(lo_half, 0.0, win)
        a0 = pltpu.roll(a1, HEAD_DIM, axis=1)
    return jnp.concatenate([a0, a1], axis=0).astype(BF16)


def _attention_logits(q, kwin, bias_ref, g):
    qs = jnp.concatenate([q[:, 256 * g:256 * g + LANES],
                          q[:, 256 * g + LANES:256 * (g + 1)]], axis=0).astype(BF16)
    return _dot_nt(qs, _blocked_kv(kwin, g)) + bias_ref[g]


def _attention_finish(logits, vwin, sinks_ref, valid, g):
    head0 = lax.broadcasted_iota(jnp.int32, (2 * CHUNK, 2 * N_KEYS), 1) < N_KEYS
    row_lo = lax.broadcasted_iota(jnp.int32, (2 * CHUNK, 1), 0) < CHUNK
    lane_o = lax.broadcasted_iota(jnp.int32, (2 * CHUNK, LANES), 1) < HEAD_DIM
    logits = jnp.where(valid, logits, NEG_INF)
    s0 = jnp.where(row_lo, sinks_ref[4 * g + 0], sinks_ref[4 * g + 2])
    s1 = jnp.where(row_lo, sinks_ref[4 * g + 1], sinks_ref[4 * g + 3])
    m0 = jnp.max(jnp.where(head0, logits, -jnp.inf), axis=1, keepdims=True)
    m1 = jnp.max(jnp.where(head0, -jnp.inf, logits), axis=1, keepdims=True)
    m0 = jnp.maximum(m0, s0)
    m1 = jnp.maximum(m1, s1)
    p = jnp.exp(logits - jnp.where(head0, m0, m1))
    d0 = jnp.sum(jnp.where(head0, p, 0.0), axis=1, keepdims=True) + jnp.exp(s0 - m0)
    d1 = jnp.sum(jnp.where(head0, 0.0, p), axis=1, keepdims=True) + jnp.exp(s1 - m1)
    o = _dot(p.astype(BF16), _blocked_kv(vwin, g)) / jnp.where(lane_o, d0, d1)
    return jnp.concatenate([o[:CHUNK], o[CHUNK:]], axis=1)


def _ssd_constants():
    r = lax.broadcasted_iota(jnp.int32, (CHUNK, CHUNK), 0)
    c = lax.broadcasted_iota(jnp.int32, (CHUNK, CHUNK), 1)
    tri = jnp.where(c <= r, 1.0, 0.0).astype(BF16)
    hrow = lax.broadcasted_iota(jnp.int32, (LANES, SSD_INNER), 0)
    hcol = lax.broadcasted_iota(jnp.int32, (LANES, SSD_INNER), 1)
    expand = jnp.where(hcol >> 6 == hrow, 1.0, 0.0).astype(BF16)
    return tri, expand


def _ssd_chunk(xbc_act, dt_raw, ht_ref, g_idx, dtb_ref, a_ref, dskip_ref, nvalid, tri, expand, fill):
    xs = xbc_act[:, :SSD_INNER]
    bm = xbc_act[:, SSD_INNER:SSD_INNER + SSD_GROUPS * SSD_STATE]
    cm = xbc_act[:, SSD_INNER + SSD_GROUPS * SSD_STATE:]

    dt = _softplus(dt_raw + dtb_ref[...])
    if nvalid < CHUNK:
        rows = lax.broadcasted_iota(jnp.int32, dt.shape, 0)
        dt = jnp.where(rows < nvalid, dt, 0.0)
    da = dt * a_ref[...]

    acum = _exact_dot_left(tri, da)
    dt_exp = _exact_dot_right(dt, expand, passes=1)
    a_col = _exact_dot_right(acum, expand, passes=2)
    a_last = a_col[CHUNK - 1:CHUNK, :]
    fill()
    xdt = xs * dt_exp
    xw = xdt * jnp.exp(a_last - a_col)
    e_col = jnp.exp(a_col)
    e_last = jnp.exp(a_last)

    l2 = lax.broadcasted_iota(jnp.int32, (CHUNK, LANES), 0)
    j2 = lax.broadcasted_iota(jnp.int32, (CHUNK, LANES), 1)
    s2 = j2 & (CHUNK - 1)
    diag_sel = jnp.where(s2 == l2, 1.0, 0.0)
    causal2 = s2 <= l2
    lane_lo = j2 < SSD_HEAD_DIM

    ys = []
    for gq in range(SSD_GROUPS):
        bg = bm[:, SSD_STATE * gq:SSD_STATE * (gq + 1)]
        cg = cm[:, SSD_STATE * gq:SSD_STATE * (gq + 1)].astype(BF16)
        bg_bf = bg.astype(BF16)
        cb2 = _dot_nt(cg, jnp.concatenate([bg_bf, bg_bf], axis=0))
        gsl = slice(GROUP_WIDTH * gq, GROUP_WIDTH * (gq + 1))
        h_prev = ht_ref[g_idx, :, gsl]
        fill()
        y_off = _dot(cg, h_prev.astype(BF16)) * e_col[:, gsl]
        yd = []
        for i in range(GROUP_WIDTH // LANES):
            psl = slice(GROUP_WIDTH * gq + LANES * i, GROUP_WIDTH * gq + LANES * (i + 1))
            ac = a_col[:, psl]
            a_row = jnp.sum(ac * diag_sel, axis=0, keepdims=True)
            lmat = jnp.exp(jnp.where(causal2, ac - a_row, -jnp.inf))
            m2 = (cb2 * lmat).astype(BF16)
            xp = xdt[:, psl]
            xblk = jnp.concatenate([jnp.where(lane_lo, xp, 0.0), jnp.where(lane_lo, 0.0, xp)],
                                   axis=0).astype(BF16)
            yd.append(_dot(m2, xblk))
        ys.append(jnp.concatenate(yd, axis=1) + y_off)
        st = _dot(bg.T.astype(BF16), xw[:, gsl].astype(BF16))
        ht_ref[g_idx, :, gsl] = h_prev * e_last[:, gsl] + st
    return jnp.concatenate(ys, axis=1) + xs * dskip_ref[...]


def _mixer_kernel(*refs, n_seq, n_chunk, n_tiles, nvalid, has_init):
    it = iter(refs)
    x_ref = next(it)
    bias_ref = next(it)
    sinks_ref = next(it)
    gmix_ref = next(it)
    win_ref = next(it)
    wgate_ref = next(it)
    convw_ref = next(it)
    convb_ref = next(it)
    dtb_ref = next(it)
    a_ref = next(it)
    dskip_ref = next(it)
    gssd_ref = next(it)
    watt_ref = next(it)
    wssd_ref = next(it)
    wout_ref = next(it)
    if has_init:
        kvi_ref = next(it)
        convi_ref = next(it)
        ssmi_ref = next(it)
    h_ref = next(it)
    kvo_ref = next(it)
    convo_ref = next(it)
    ssmo_ref = next(it)
    hn_s = next(it)
    q_s = next(it)
    kv_s = next(it)
    xbc_s = next(it)
    dt_s = next(it)
    oatt_s = next(it)
    y_s = next(it)
    ht_s = next(it)
    gate_s = next(it)

    t = pl.program_id(1)
    seq_rows = n_chunk * CHUNK

    @pl.when(t == 0)
    def _init():
        if has_init:
            kv_s[:, 0:WINDOW, :] = kvi_ref[...]
            xbc_s[:, 0:HIST_ROWS, :] = convi_ref[...]
            for g in range(n_seq):
                ht_s[g] = ssmi_ref[g].T
        else:
            kv_s[:, 0:WINDOW, :] = jnp.zeros((n_seq, WINDOW, 2 * KV_WIDTH), F32)
            xbc_s[:, 0:HIST_ROWS, :] = jnp.zeros((n_seq, HIST_ROWS, CONV_DIM), F32)
            ht_s[...] = jnp.zeros(ht_s.shape, F32)

    hn = _rmsnorm(x_ref[...], gmix_ref[...]).astype(BF16)
    hn_s[...] = hn
    xbc = _dot(hn, win_ref[:, OFF_XBC:OFF_DT])
    for g in range(n_seq):
        xbc_s[g, HIST_ROWS:HIST_ROWS + seq_rows, :] = xbc[g * seq_rows:(g + 1) * seq_rows]
    q_s[...] = _dot(hn, win_ref[:, OFF_Q:OFF_KV]) * (HEAD_DIM ** -0.5)
    kv = _dot(hn, win_ref[:, OFF_KV:OFF_XBC])
    for g in range(n_seq):
        kv_s[g, WINDOW:WINDOW + seq_rows, :] = kv[g * seq_rows:(g + 1) * seq_rows]
    dt_s[...] = _dot(hn, win_ref[:, OFF_DT:IN_PACKED])

    cols = lax.broadcasted_iota(jnp.int32, (2 * CHUNK, 2 * N_KEYS), 1)
    colmod = jnp.where(cols >= N_KEYS, cols - N_KEYS, cols)
    tri, expand = _ssd_constants()

    n_iter = n_seq * n_chunk // CHUNKS_PER_ITER
    slab = GATE_WIDTH // n_iter

    def do_chunk(idx, fill):
        if n_seq == 1:
            g, j = 0, idx
        else:
            g, j = idx, 0
        r0 = pl.multiple_of(idx * CHUNK, CHUNK)
        k0 = r0 if n_seq == 1 else 0
        if has_init:
            valid = colmod < WINDOW + nvalid
        else:
            first_valid = jnp.maximum(0, (2 - (t * n_chunk + j)) * CHUNK)
            valid = colmod >= first_valid
        kvwin = kv_s[g, pl.ds(k0, N_KEYS), :]
        q = q_s[pl.ds(r0, CHUNK), :]
        logits = [_attention_logits(q, kvwin[:, :KV_WIDTH], bias_ref, hg) for hg in range(N_KV_HEADS)]
        fill()
        win = xbc_s[g, pl.ds(k0, HIST_ROWS + CHUNK), :]
        conv = convb_ref[...] + convw_ref[CONV_W - 1:CONV_W, :] * win[HIST_ROWS:]
        for i in range(1, CONV_W):
            conv = conv + convw_ref[CONV_W - 1 - i:CONV_W - i, :] * win[HIST_ROWS - i:HIST_ROWS - i + CHUNK]
        xbc_act = _silu(conv)
        outs = []
        for hg in range(N_KV_HEADS):
            outs.append(_attention_finish(logits[hg], kvwin[:, KV_WIDTH:], sinks_ref, valid, hg))
            fill()
        oatt_s[pl.ds(r0, CHUNK), :] = jnp.concatenate(outs, axis=1)
        y_s[pl.ds(r0, CHUNK), :] = _ssd_chunk(xbc_act, dt_s[pl.ds(r0, CHUNK), :], ht_s, g, dtb_ref,
                                              a_ref, dskip_ref, nvalid, tri, expand, fill)

    def iter_body(it, carry):
        n_pieces = slab // MXU_WIDTH
        n_points = FILL_POINTS * CHUNKS_PER_ITER
        calls = [0]

        def fill():
            k = calls[0]
            calls[0] += 1
            for p in range(-(-k * n_pieces // n_points), -(-(k + 1) * n_pieces // n_points)):
                c0 = p * MXU_WIDTH
                gate_s[it, :, c0:c0 + MXU_WIDTH] = _dot(hn_s[...], wgate_ref[it, :, c0:c0 + MXU_WIDTH])

        for u in range(CHUNKS_PER_ITER):
            do_chunk(it * CHUNKS_PER_ITER + u, fill)
        assert calls[0] == n_points
        return carry

    lax.fori_loop(0, n_iter, iter_body, 0)

    def gate_cols(lo, hi):
        pieces = []
        while lo < hi:
            s, off = divmod(lo, slab)
            take = min(hi - lo, slab - off)
            pieces.append(gate_s[s, :, off:off + take])
            lo += take
        return jnp.concatenate(pieces, axis=1)

    y_ssd = _rmsnorm(y_s[...] * _silu(gate_cols(GATE_Z, GATE_A)), gssd_ref[...]).astype(BF16)
    merged = (_sigmoid(gate_cols(GATE_A, GATE_S)) * _dot(oatt_s[...].astype(BF16), watt_ref[...])
              + _sigmoid(gate_cols(GATE_S, GATE_WIDTH)) * _dot(y_ssd, wssd_ref[...]))
    h_ref[...] = x_ref[...] + _dot(merged.astype(BF16), wout_ref[...])

    @pl.when(t == n_tiles - 1)
    def _emit_states():
        kvo_ref[...] = kv_s[:, nvalid:nvalid + WINDOW, :]
        convo_ref[...] = xbc_s[:, nvalid:nvalid + HIST_ROWS, :]
        for g in range(n_seq):
            ssmo_ref[g] = ht_s[g].T

    if n_tiles > 1:
        kv_s[:, 0:WINDOW, :] = kv_s[:, seq_rows:seq_rows + WINDOW, :]
        xbc_s[:, 0:HIST_ROWS, :] = xbc_s[:, seq_rows:seq_rows + HIST_ROWS, :]


def _const_spec(shape):
    nd = len(shape)
    return pl.BlockSpec(shape, lambda b, t, _nd=nd: (0,) * _nd, pipeline_mode=pl.Buffered(1))


def _mixer(x2d, bias, lw, init, *, batch, seq_pad, n_seq, n_chunk, nvalid):
    seq_rows = n_chunk * CHUNK
    n_tiles = seq_pad // seq_rows
    assert n_seq == 1 or n_tiles == 1
    assert batch % n_seq == 0 and seq_pad % seq_rows == 0
    rows = n_seq * seq_rows
    has_init = init is not None
    n_slab = n_seq * n_chunk // CHUNKS_PER_ITER
    assert (n_seq * n_chunk) % CHUNKS_PER_ITER == 0 and GATE_WIDTH % (n_slab * MXU_WIDTH) == 0
    last_valid = nvalid - (n_tiles - 1) * seq_rows

    tile_map = lambda b, t: (b * n_tiles + t, 0)
    seq_map = lambda b, t: (b, 0, 0)
    in_specs = [
        pl.BlockSpec((rows, D_MODEL), tile_map),
        _const_spec(bias.shape),
        pl.BlockSpec(memory_space=pltpu.SMEM),
        _const_spec((1, D_MODEL)),
        _const_spec((D_MODEL, IN_PACKED)),
        _const_spec((n_slab, D_MODEL, GATE_WIDTH // n_slab)),
        _const_spec((CONV_W, CONV_DIM)),
        _const_spec((1, CONV_DIM)),
        _const_spec((1, LANES)),
        _const_spec((1, LANES)),
        _const_spec((1, SSD_INNER)),
        _const_spec((1, SSD_INNER)),
        _const_spec((ATT_WIDTH, D_MODEL)),
        _const_spec((SSD_INNER, D_MODEL)),
        _const_spec((D_MODEL, D_MODEL)),
    ]
    w_gate_slabs = lw["w_gates"].reshape(D_MODEL, n_slab, GATE_WIDTH // n_slab).transpose(1, 0, 2)
    args = [x2d, bias, lw["sinks"], lw["g_mix"], lw["w_in"], w_gate_slabs, lw["conv_w"], lw["conv_b"], lw["dt_bias"],
            lw["a"], lw["d_skip"], lw["g_ssd"], lw["w_att_out"], lw["w_ssd_out"], lw["w_out"]]
    if has_init:
        in_specs += [
            pl.BlockSpec((n_seq, WINDOW, 2 * KV_WIDTH), seq_map),
            pl.BlockSpec((n_seq, HIST_ROWS, CONV_DIM), seq_map),
            pl.BlockSpec((n_seq, SSD_INNER, SSD_STATE), seq_map),
        ]
        args += list(init)
    out_shape = [
        jax.ShapeDtypeStruct((batch * seq_pad, D_MODEL), F32),
        jax.ShapeDtypeStruct((batch, WINDOW, 2 * KV_WIDTH), F32),
        jax.ShapeDtypeStruct((batch, HIST_ROWS, CONV_DIM), F32),
        jax.ShapeDtypeStruct((batch, SSD_INNER, SSD_STATE), F32),
    ]
    out_specs = [
        pl.BlockSpec((rows, D_MODEL), tile_map),
        pl.BlockSpec((n_seq, WINDOW, 2 * KV_WIDTH), seq_map),
        pl.BlockSpec((n_seq, HIST_ROWS, CONV_DIM), seq_map),
        pl.BlockSpec((n_seq, SSD_INNER, SSD_STATE), seq_map),
    ]
    scratch = [
        pltpu.VMEM((rows, D_MODEL), BF16),
        pltpu.VMEM((rows, ATT_WIDTH), F32),
        pltpu.VMEM((n_seq, WINDOW + seq_rows, 2 * KV_WIDTH), F32),
        pltpu.VMEM((n_seq, HIST_ROWS + seq_rows, CONV_DIM), F32),
        pltpu.VMEM((rows, LANES), F32),
        pltpu.VMEM((rows, ATT_WIDTH), F32),
        pltpu.VMEM((rows, SSD_INNER), F32),
        pltpu.VMEM((n_seq, SSD_STATE, SSD_INNER), F32),
        pltpu.VMEM((n_slab, rows, GATE_WIDTH // n_slab), F32),
    ]
    kern = functools.partial(_mixer_kernel, n_seq=n_seq, n_chunk=n_chunk, n_tiles=n_tiles,
                             nvalid=last_valid, has_init=has_init)
    return pl.pallas_call(
        kern,
        grid=(batch // n_seq, n_tiles),
        in_specs=in_specs,
        out_specs=out_specs,
        out_shape=out_shape,
        scratch_shapes=scratch,
        compiler_params=pltpu.CompilerParams(
            dimension_semantics=("arbitrary", "arbitrary"), vmem_limit_bytes=VMEM_LIMIT_BYTES),
        name="mixer_init" if has_init else "mixer",
    )(*args)


def _ffn_kernel(h_ref, g_ref, wg_ref, wu_ref, wd_ref, gfin_ref, o_ref, *, final):
    h = h_ref[...]
    hf = _rmsnorm(h, g_ref[...]).astype(BF16)
    act = (_silu(_dot(hf, wg_ref[...])) * _dot(hf, wu_ref[...])).astype(BF16)
    out = h + _dot(act, wd_ref[...])
    if final:
        out = _rmsnorm(out, gfin_ref[...])
    o_ref[...] = out


def _ffn(h2d, lw, g_final, *, block_rows, final):
    n_rows = h2d.shape[0]
    assert n_rows % block_rows == 0
    const = lambda shape: pl.BlockSpec(shape, lambda i: (0, 0), pipeline_mode=pl.Buffered(1))
    return pl.pallas_call(
        functools.partial(_ffn_kernel, final=final),
        grid=(n_rows // block_rows,),
        in_specs=[
            pl.BlockSpec((block_rows, D_MODEL), lambda i: (i, 0)),
            const((1, D_MODEL)),
            const((D_MODEL, D_FF)),
            const((D_MODEL, D_FF)),
            const((D_FF, D_MODEL)),
            const((1, D_MODEL)),
        ],
        out_specs=pl.BlockSpec((block_rows, D_MODEL), lambda i: (i, 0)),
        out_shape=jax.ShapeDtypeStruct((n_rows, D_MODEL), F32),
        compiler_params=pltpu.CompilerParams(
            dimension_semantics=("arbitrary",), vmem_limit_bytes=VMEM_LIMIT_BYTES),
        name="ffn_final" if final else "ffn",
    )(h2d, lw["g_ffn"], lw["w_gate"], lw["w_up"], lw["w_down"], g_final)


def _pack_w_in(w):
    o = 0
    parts = {}
    for name, width in (("q", ATT_WIDTH), ("k", KV_WIDTH), ("v", KV_WIDTH), ("z", SSD_INNER),
                        ("xbc", CONV_DIM), ("dt", SSD_HEADS), ("ga", D_MODEL), ("gs", D_MODEL)):
        parts[name] = w[:, o:o + width]
        o += width
    dt = jnp.pad(parts["dt"], ((0, 0), (0, LANES - SSD_HEADS)))
    packed = jnp.concatenate([parts["q"], parts["k"], parts["v"], parts["xbc"], dt], axis=1)
    gates = jnp.concatenate([parts["z"], parts["ga"], parts["gs"]], axis=1)
    return packed.astype(BF16), gates.astype(BF16)


def _layer_weights(l, g_mix, w_in, conv_w, conv_b, dt_bias, a_log, d_skip, g_ssd, sinks, w_att_out,
                   w_ssd_out, w_out, g_ffn, w_gate, w_up, w_down):
    pad_h = ((0, LANES - SSD_HEADS),)
    w_packed, w_gates = _pack_w_in(w_in[l])
    return {
        "sinks": sinks[l],
        "g_mix": g_mix[l][None, :],
        "w_in": w_packed,
        "w_gates": w_gates,
        "conv_w": conv_w[l],
        "conv_b": conv_b[l][None, :],
        "dt_bias": jnp.pad(dt_bias[l], pad_h)[None, :],
        "a": jnp.pad(-jnp.exp(a_log[l]), pad_h)[None, :],
        "d_skip": jnp.repeat(d_skip[l], SSD_HEAD_DIM)[None, :],
        "g_ssd": g_ssd[l][None, :],
        "w_att_out": w_att_out[l].astype(BF16),
        "w_ssd_out": w_ssd_out[l].astype(BF16),
        "w_out": w_out[l].astype(BF16),
        "g_ffn": g_ffn[l][None, :],
        "w_gate": w_gate[l].astype(BF16),
        "w_up": w_up[l].astype(BF16),
        "w_down": w_down[l].astype(BF16),
    }


PROMPT_CHUNKS_PER_TILE = 8
SAMPLE_SEQS_PER_TILE = 4
FFN_BLOCK_ROWS = 512


def kernel(x_prompt, x_sample, cache_k, cache_v, state_conv, state_ssm, rel_table, g_mix, w_in, conv_w, conv_b, dt_bias, a_log, d_skip, g_ssd, sinks, w_att_out, w_ssd_out, w_out, g_ffn, w_gate, w_up, w_down, g_final):
    depth = w_in.shape[0]
    bp, sp, _ = x_prompt.shape
    bs, ts, _ = x_sample.shape
    kv_len = cache_k.shape[2]
    assert kv_len == WINDOW and ts <= CHUNK and ts % 8 == 0 and ts >= CONV_W - 1

    bias = _blocked_bias(rel_table)
    g_fin = g_final[None, :]

    xp = x_prompt.reshape(bp * sp, D_MODEL)
    xs = x_sample.reshape(bs * ts, D_MODEL)
    st_p, st_s = [], []
    for l in range(depth):
        lw = _layer_weights(l, g_mix, w_in, conv_w, conv_b, dt_bias, a_log, d_skip, g_ssd, sinks,
                            w_att_out, w_ssd_out, w_out, g_ffn, w_gate, w_up, w_down)
        final = l == depth - 1
        hp, kvp, convp, ssmp = _mixer(xp, bias, lw, None, batch=bp, seq_pad=sp, n_seq=1,
                                      n_chunk=PROMPT_CHUNKS_PER_TILE, nvalid=sp)
        xp = _ffn(hp, lw, g_fin, block_rows=FFN_BLOCK_ROWS, final=final)
        st_p.append((kvp, convp, ssmp))
        xs_pad = jnp.pad(xs.reshape(bs, ts, D_MODEL), ((0, 0), (0, CHUNK - ts), (0, 0)))
        init = (
            jnp.concatenate([cache_k[l].reshape(bs, WINDOW, KV_WIDTH),
                             cache_v[l].reshape(bs, WINDOW, KV_WIDTH)], axis=-1),
            jnp.pad(state_conv[l], ((0, 0), (HIST_ROWS - (CONV_W - 1), 0), (0, 0))),
            state_ssm[l].reshape(bs, SSD_INNER, SSD_STATE),
        )
        hs, kvs, convs, ssms = _mixer(xs_pad.reshape(bs * CHUNK, D_MODEL), bias, lw, init, batch=bs,
                                      seq_pad=CHUNK, n_seq=SAMPLE_SEQS_PER_TILE, n_chunk=1, nvalid=ts)
        hs = hs.reshape(bs, CHUNK, D_MODEL)[:, :ts].reshape(bs * ts, D_MODEL)
        xs = _ffn(hs, lw, g_fin, block_rows=bs * ts, final=final)
        st_s.append((kvs, convs, ssms))

    def states(sts, b):
        kv = jnp.stack([s[0] for s in sts])
        k = kv[..., :KV_WIDTH].reshape(depth, b, WINDOW, N_KV_HEADS, HEAD_DIM)
        v = kv[..., KV_WIDTH:].reshape(depth, b, WINDOW, N_KV_HEADS, HEAD_DIM)
        conv = jnp.stack([s[1] for s in sts])[:, :, HIST_ROWS - (CONV_W - 1):, :]
        ssm = jnp.stack([s[2] for s in sts]).reshape(depth, b, SSD_HEADS, SSD_HEAD_DIM, SSD_STATE)
        return k, v, conv, ssm

    kp, vp, cp, sp_state = states(st_p, bp)
    ks, vs, cs, ss_state = states(st_s, bs)
    return (xp.reshape(x_prompt.shape), xs.reshape(x_sample.shape), kp, vp, cp, sp_state,
            ks, vs, cs, ss_state)
```

```python
import functools
import math

import jax
import jax.numpy as jnp
from jax import lax
from jax.experimental import pallas as pl
from jax.experimental.pallas import tpu as pltpu

D_MODEL = 1024
CHUNK = 64
EPS = 1e-6
NEG_INF = -1e30
N_HEADS = 8
N_KV_HEADS = 2
HEAD_DIM = 64
ATT_WIDTH = N_HEADS * HEAD_DIM
KV_WIDTH = N_KV_HEADS * HEAD_DIM
WINDOW = 128
N_KEYS = WINDOW + CHUNK
NUM_BUCKETS = 32
MAX_DISTANCE = 128
SSD_INNER = 1024
SSD_HEADS = 16
SSD_HEAD_DIM = 64
SSD_GROUPS = 2
SSD_STATE = 128
GROUP_WIDTH = SSD_INNER // SSD_GROUPS
CONV_W = 4
CONV_DIM = SSD_INNER + 2 * SSD_GROUPS * SSD_STATE
D_FF = 2816
LANES = 128
HIST_ROWS = 8
MXU_WIDTH = 256
FILL_POINTS = 7

OFF_Q = 0
OFF_KV = OFF_Q + ATT_WIDTH
OFF_XBC = OFF_KV + 2 * KV_WIDTH
OFF_DT = OFF_XBC + CONV_DIM
IN_PACKED = OFF_DT + LANES
GATE_Z = 0
GATE_A = GATE_Z + SSD_INNER
GATE_S = GATE_A + D_MODEL
GATE_WIDTH = GATE_S + D_MODEL

VMEM_LIMIT_BYTES = 60 * 1024 * 1024

F32 = jnp.float32
BF16 = jnp.bfloat16


def _dot(a, b):
    return jnp.dot(a, b, preferred_element_type=F32)


def _dot_nt(a, b):
    return lax.dot_general(a, b, (((1,), (1,)), ((), ())), preferred_element_type=F32)


def _split3(x):
    hi = x.astype(BF16)
    r1 = x - hi.astype(F32)
    mid = r1.astype(BF16)
    lo = (r1 - mid.astype(F32)).astype(BF16)
    return hi, mid, lo


def _exact_dot_right(x, sel, passes=3):
    parts = _split3(x)[:passes]
    out = _dot(parts[0], sel)
    for part in parts[1:]:
        out = out + _dot(part, sel)
    return out


def _exact_dot_left(sel, x):
    hi, mid, lo = _split3(x)
    return _dot(sel, hi) + _dot(sel, mid) + _dot(sel, lo)


def _rmsnorm(x, g):
    return x * lax.rsqrt(jnp.mean(x * x, axis=-1, keepdims=True) + EPS) * g


def _sigmoid(x):
    return 1.0 / (1.0 + jnp.exp(-x))


def _silu(x):
    return x * _sigmoid(x)


def _softplus(x):
    return jnp.maximum(x, 0.0) + jnp.log1p(jnp.exp(-jnp.abs(x)))


def _bias_kernel(table_ref, bucket_ref, o_ref):
    bucket = bucket_ref[...]
    rows = lax.broadcasted_iota(jnp.int32, bucket.shape, 0)
    cols = lax.broadcasted_iota(jnp.int32, bucket.shape, 1)
    row_hi = rows >= CHUNK
    col_hi = cols >= N_KEYS
    for g in range(N_KV_HEADS):
        acc = jnp.zeros(bucket.shape, F32)
        for b in range(NUM_BUCKETS):
            t0 = table_ref[b, 4 * g + 0]
            t1 = table_ref[b, 4 * g + 1]
            t2 = table_ref[b, 4 * g + 2]
            t3 = table_ref[b, 4 * g + 3]
            tv = jnp.where(row_hi, jnp.where(col_hi, t3, t2), jnp.where(col_hi, t1, t0))
            acc = jnp.where(bucket == b, tv, acc)
        o_ref[g] = acc


def _t5_bucket(rel):
    nb = NUM_BUCKETS // 2
    max_exact = nb // 2
    ret = jnp.where(rel > 0, nb, 0)
    n = jnp.abs(rel)
    nf = jnp.maximum(n, 1).astype(jnp.float32)
    large = max_exact + (jnp.log(nf / max_exact) / math.log(MAX_DISTANCE / max_exact)
                         * (nb - max_exact)).astype(jnp.int32)
    large = jnp.minimum(large, nb - 1)
    return ret + jnp.where(n < max_exact, n, large)


def _blocked_bias(rel_table):
    qi = jnp.arange(CHUNK, dtype=jnp.int32)
    kj = jnp.arange(N_KEYS, dtype=jnp.int32)
    bucket = _t5_bucket(kj[None, :] - WINDOW - qi[:, None]).astype(jnp.int32)
    bucket = jnp.tile(bucket, (2, 2))
    return pl.pallas_call(
        _bias_kernel,
        out_shape=jax.ShapeDtypeStruct((N_KV_HEADS, 2 * CHUNK, 2 * N_KEYS), F32),
        in_specs=[pl.BlockSpec(memory_space=pltpu.SMEM),
                  pl.BlockSpec(memory_space=pltpu.VMEM)],
        out_specs=pl.BlockSpec(memory_space=pltpu.VMEM),
        name="rel_bias",
    )(rel_table, bucket)


def _blocked_kv(win, g):
    lo_half = lax.broadcasted_iota(jnp.int32, (N_KEYS, LANES), 1) < HEAD_DIM
    if g == 0:
        a0 = jnp.where(lo_half, win, 0.0)
        a1 = pltpu.roll(a0, HEAD_DIM, axis=1)
    else:
        a1 = jnp.where(lo_half, 0.0, win)
        a0 = pltpu.roll(a1, HEAD_DIM, axis=1)
    return jnp.concatenate([a0, a1], axis=0).astype(BF16)


def _attention_logits(q, kwin, bias_ref, g):
    qs = jnp.concatenate([q[:, 256 * g:256 * g + LANES],
                          q[:, 256 * g + LANES:256 * (g + 1)]], axis=0).astype(BF16)
    return _dot_nt(qs, _blocked_kv(kwin, g)) + bias_ref[g]


def _attention_finish(logits, vwin, sinks_ref, valid, g):
    head0 = lax.broadcasted_iota(jnp.int32, (2 * CHUNK, 2 * N_KEYS), 1) < N_KEYS
    row_lo = lax.broadcasted_iota(jnp.int32, (2 * CHUNK, 1), 0) < CHUNK
    lane_o = lax.broadcasted_iota(jnp.int32, (2 * CHUNK, LANES), 1) < HEAD_DIM
    logits = jnp.where(valid, logits, NEG_INF)
    s0 = jnp.where(row_lo, sinks_ref[4 * g + 0], sinks_ref[4 * g + 2])
    s1 = jnp.where(row_lo, sinks_ref[4 * g + 1], sinks_ref[4 * g + 3])
    m0 = jnp.max(jnp.where(head0, logits, -jnp.inf), axis=1, keepdims=True)
    m1 = jnp.max(jnp.where(head0, -jnp.inf, logits), axis=1, keepdims=True)
    m0 = jnp.maximum(m0, s0)
    m1 = jnp.maximum(m1, s1)
    p = jnp.exp(logits - jnp.where(head0, m0, m1))
    d0 = jnp.sum(jnp.where(head0, p, 0.0), axis=1, keepdims=True) + jnp.exp(s0 - m0)
    d1 = jnp.sum(jnp.where(head0, 0.0, p), axis=1, keepdims=True) + jnp.exp(s1 - m1)
    o = _dot(p.astype(BF16), _blocked_kv(vwin, g)) / jnp.where(lane_o, d0, d1)
    return jnp.concatenate([o[:CHUNK], o[CHUNK:]], axis=1)


def _ssd_constants():
    r = lax.broadcasted_iota(jnp.int32, (CHUNK, CHUNK), 0)
    c = lax.broadcasted_iota(jnp.int32, (CHUNK, CHUNK), 1)
    tri = jnp.where(c <= r, 1.0, 0.0).astype(BF16)
    hrow = lax.broadcasted_iota(jnp.int32, (LANES, SSD_INNER), 0)
    hcol = lax.broadcasted_iota(jnp.int32, (LANES, SSD_INNER), 1)
    expand = jnp.where(hcol >> 6 == hrow, 1.0, 0.0).astype(BF16)
    return tri, expand


def _ssd_prepare(c, dt_raw, dtb_ref, a_ref, nvalid, tri, expand):
    dt = _softplus(dt_raw + dtb_ref[...])
    if nvalid < CHUNK:
        rows = lax.broadcasted_iota(jnp.int32, dt.shape, 0)
        dt = jnp.where(rows < nvalid, dt, 0.0)
    acum = _exact_dot_left(tri, dt * a_ref[...])
    c["dt_exp"] = _exact_dot_right(dt, expand, passes=1)
    c["a_col"] = _exact_dot_right(acum, expand, passes=2)


def _ssd_decay(c):
    a_col = c["a_col"]
    a_last = a_col[CHUNK - 1:CHUNK, :]
    c["xs"] = c["xbc_act"][:, :SSD_INNER]
    c["xdt"] = c["xs"] * c.pop("dt_exp")
    c["xw"] = c["xdt"] * jnp.exp(a_last - a_col)
    c["e_col"] = jnp.exp(a_col)
    c["e_last"] = jnp.exp(a_last)


def _ssd_group(c, gq, ht_ref, g_idx):
    l2 = lax.broadcasted_iota(jnp.int32, (CHUNK, LANES), 0)
    j2 = lax.broadcasted_iota(jnp.int32, (CHUNK, LANES), 1)
    s2 = j2 & (CHUNK - 1)
    diag_sel = jnp.where(s2 == l2, 1.0, 0.0)
    causal2 = s2 <= l2
    lane_lo = j2 < SSD_HEAD_DIM

    b0 = SSD_INNER + SSD_STATE * gq
    c0 = SSD_INNER + SSD_GROUPS * SSD_STATE + SSD_STATE * gq
    bg = c["xbc_act"][:, b0:b0 + SSD_STATE]
    cg = c["xbc_act"][:, c0:c0 + SSD_STATE].astype(BF16)
    bg_bf = bg.astype(BF16)
    cb2 = _dot_nt(cg, jnp.concatenate([bg_bf, bg_bf], axis=0))
    gsl = slice(GROUP_WIDTH * gq, GROUP_WIDTH * (gq + 1))
    h_prev = ht_ref[g_idx, :, gsl]
    y_off = _dot(cg, h_prev.astype(BF16)) * c["e_col"][:, gsl]
    yd = []
    for i in range(GROUP_WIDTH // LANES):
        psl = slice(GROUP_WIDTH * gq + LANES * i, GROUP_WIDTH * gq + LANES * (i + 1))
        ac = c["a_col"][:, psl]
        a_row = jnp.sum(ac * diag_sel, axis=0, keepdims=True)
        lmat = jnp.exp(jnp.where(causal2, ac - a_row, -jnp.inf))
        m2 = (cb2 * lmat).astype(BF16)
        xp = c["xdt"][:, psl]
        xblk = jnp.concatenate([jnp.where(lane_lo, xp, 0.0), jnp.where(lane_lo, 0.0, xp)],
                               axis=0).astype(BF16)
        yd.append(_dot(m2, xblk))
    c.setdefault("ys", []).append(jnp.concatenate(yd, axis=1) + y_off)
    st = _dot(bg.T.astype(BF16), c["xw"][:, gsl].astype(BF16))
    ht_ref[g_idx, :, gsl] = h_prev * c["e_last"][:, gsl] + st


def _mixer_kernel(*refs, n_seq, n_chunk, n_tiles, nvalid, has_init):
    it = iter(refs)
    x_ref = next(it)
    bias_ref = next(it)
    sinks_ref = next(it)
    gmix_ref = next(it)
    win_ref = next(it)
    wgate_ref = next(it)
    convw_ref = next(it)
    convb_ref = next(it)
    dtb_ref = next(it)
    a_ref = next(it)
    dskip_ref = next(it)
    gssd_ref = next(it)
    watt_ref = next(it)
    wssd_ref = next(it)
    wout_ref = next(it)
    if has_init:
        kvi_ref = next(it)
        convi_ref = next(it)
        ssmi_ref = next(it)
    h_ref = next(it)
    kvo_ref = next(it)
    convo_ref = next(it)
    ssmo_ref = next(it)
    hn_s = next(it)
    q_s = next(it)
    kv_s = next(it)
    xbc_s = next(it)
    dt_s = next(it)
    oatt_s = next(it)
    y_s = next(it)
    ht_s = next(it)
    gate_s = next(it)

    t = pl.program_id(1)
    seq_rows = n_chunk * CHUNK

    @pl.when(t == 0)
    def _init():
        if has_init:
            kv_s[:, 0:WINDOW, :] = kvi_ref[...]
            xbc_s[:, 0:HIST_ROWS, :] = convi_ref[...]
            for g in range(n_seq):
                ht_s[g] = ssmi_ref[g].T
        else:
            kv_s[:, 0:WINDOW, :] = jnp.zeros((n_seq, WINDOW, 2 * KV_WIDTH), F32)
            xbc_s[:, 0:HIST_ROWS, :] = jnp.zeros((n_seq, HIST_ROWS, CONV_DIM), F32)
            ht_s[...] = jnp.zeros(ht_s.shape, F32)

    rows = n_seq * seq_rows
    hn = _rmsnorm(x_ref[...].reshape(rows, D_MODEL), gmix_ref[...]).astype(BF16)
    hn_s[...] = hn
    xbc = _dot(hn, win_ref[:, OFF_XBC:OFF_DT])
    for g in range(n_seq):
        xbc_s[g, HIST_ROWS:HIST_ROWS + seq_rows, :] = xbc[g * seq_rows:(g + 1) * seq_rows]
    q_s[...] = _dot(hn, win_ref[:, OFF_Q:OFF_KV]) * (HEAD_DIM ** -0.5)
    kv = _dot(hn, win_ref[:, OFF_KV:OFF_XBC])
    for g in range(n_seq):
        kv_s[g, WINDOW:WINDOW + seq_rows, :] = kv[g * seq_rows:(g + 1) * seq_rows]
    dt_s[...] = _dot(hn, win_ref[:, OFF_DT:IN_PACKED])

    cols = lax.broadcasted_iota(jnp.int32, (2 * CHUNK, 2 * N_KEYS), 1)
    colmod = jnp.where(cols >= N_KEYS, cols - N_KEYS, cols)
    tri, expand = _ssd_constants()

    slab = GATE_WIDTH // n_chunk

    def iter_body(j, carry):
        k0 = pl.multiple_of(j * CHUNK, CHUNK) if n_chunk > 1 else 0
        n_pieces = slab // MXU_WIDTH
        n_points = FILL_POINTS * n_seq
        calls = [0]

        def fill():
            k = calls[0]
            calls[0] += 1
            for p in range(-(-k * n_pieces // n_points), -(-(k + 1) * n_pieces // n_points)):
                c0 = p * MXU_WIDTH
                gate_s[j, :, c0:c0 + MXU_WIDTH] = _dot(hn_s[...], wgate_ref[j, :, c0:c0 + MXU_WIDTH])

        if has_init:
            valid = colmod < WINDOW + nvalid
        else:
            first_valid = jnp.maximum(0, (2 - (t * n_chunk + j)) * CHUNK)
            valid = colmod >= first_valid
        chunks = [{"g": g, "rows": pl.ds(pl.multiple_of(g * seq_rows + k0, CHUNK), CHUNK)}
                  for g in range(n_seq)]
        for c in chunks:
            kvwin = kv_s[c["g"], pl.ds(k0, N_KEYS), :]
            c["vwin"] = kvwin[:, KV_WIDTH:]
            q = q_s[c["rows"], :]
            c["logits"] = [_attention_logits(q, kvwin[:, :KV_WIDTH], bias_ref, hg) for hg in range(N_KV_HEADS)]
            fill()
        for c in chunks:
            win = xbc_s[c["g"], pl.ds(k0, HIST_ROWS + CHUNK), :]
            conv = convb_ref[...] + convw_ref[CONV_W - 1:CONV_W, :] * win[HIST_ROWS:]
            for i in range(1, CONV_W):
                conv = conv + convw_ref[CONV_W - 1 - i:CONV_W - i, :] * win[HIST_ROWS - i:HIST_ROWS - i + CHUNK]
            c["xbc_act"] = _silu(conv)
            fill()
        for c in chunks:
            outs = [_attention_finish(c["logits"][hg], c["vwin"], sinks_ref, valid, hg)
                    for hg in range(N_KV_HEADS)]
            oatt_s[c["rows"], :] = jnp.concatenate(outs, axis=1)
            del c["logits"], c["vwin"]
            fill()
        for c in chunks:
            _ssd_prepare(c, dt_s[c["rows"], :], dtb_ref, a_ref, nvalid, tri, expand)
            fill()
        for c in chunks:
            _ssd_decay(c)
            fill()
        for gq in range(SSD_GROUPS):
            for c in chunks:
                _ssd_group(c, gq, ht_s, c["g"])
                fill()
        for c in chunks:
            y_s[c["rows"], :] = jnp.concatenate(c["ys"], axis=1) + c["xs"] * dskip_ref[...]
        assert calls[0] == n_points
        return carry

    lax.fori_loop(0, n_chunk, iter_body, 0)

    def gate_cols(lo, hi):
        pieces = []
        while lo < hi:
            s, off = divmod(lo, slab)
            take = min(hi - lo, slab - off)
            pieces.append(gate_s[s, :, off:off + take])
            lo += take
        return jnp.concatenate(pieces, axis=1)

    y_ssd = _rmsnorm(y_s[...] * _silu(gate_cols(GATE_Z, GATE_A)), gssd_ref[...]).astype(BF16)
    merged = (_sigmoid(gate_cols(GATE_A, GATE_S)) * _dot(oatt_s[...].astype(BF16), watt_ref[...])
              + _sigmoid(gate_cols(GATE_S, GATE_WIDTH)) * _dot(y_ssd, wssd_ref[...]))
    h = x_ref[...].reshape(rows, D_MODEL) + _dot(merged.astype(BF16), wout_ref[...])
    h_ref[...] = h.reshape(n_seq, seq_rows, D_MODEL)

    @pl.when(t == n_tiles - 1)
    def _emit_states():
        kvo_ref[...] = kv_s[:, nvalid:nvalid + WINDOW, :]
        convo_ref[...] = xbc_s[:, nvalid:nvalid + HIST_ROWS, :]
        for g in range(n_seq):
            ssmo_ref[g] = ht_s[g].T

    if n_tiles > 1:
        kv_s[:, 0:WINDOW, :] = kv_s[:, seq_rows:seq_rows + WINDOW, :]
        xbc_s[:, 0:HIST_ROWS, :] = xbc_s[:, seq_rows:seq_rows + HIST_ROWS, :]


def _const_spec(shape):
    nd = len(shape)
    return pl.BlockSpec(shape, lambda b, t, _nd=nd: (0,) * _nd, pipeline_mode=pl.Buffered(1))


def _mixer(x3d, bias, lw, init, *, n_seq, n_chunk, nvalid):
    batch, seq_pad, _ = x3d.shape
    seq_rows = n_chunk * CHUNK
    n_tiles = seq_pad // seq_rows
    assert batch % n_seq == 0 and seq_pad % seq_rows == 0
    rows = n_seq * seq_rows
    has_init = init is not None
    n_slab = n_chunk
    assert GATE_WIDTH % (n_slab * MXU_WIDTH) == 0
    last_valid = nvalid - (n_tiles - 1) * seq_rows

    tile_map = lambda b, t: (b, t, 0)
    seq_map = lambda b, t: (b, 0, 0)
    in_specs = [
        pl.BlockSpec((n_seq, seq_rows, D_MODEL), tile_map),
        _const_spec(bias.shape),
        pl.BlockSpec(memory_space=pltpu.SMEM),
        _const_spec((1, D_MODEL)),
        _const_spec((D_MODEL, IN_PACKED)),
        _const_spec((n_slab, D_MODEL, GATE_WIDTH // n_slab)),
        _const_spec((CONV_W, CONV_DIM)),
        _const_spec((1, CONV_DIM)),
        _const_spec((1, LANES)),
        _const_spec((1, LANES)),
        _const_spec((1, SSD_INNER)),
        _const_spec((1, SSD_INNER)),
        _const_spec((ATT_WIDTH, D_MODEL)),
        _const_spec((SSD_INNER, D_MODEL)),
        _const_spec((D_MODEL, D_MODEL)),
    ]
    w_gate_slabs = lw["w_gates"].reshape(D_MODEL, n_slab, GATE_WIDTH // n_slab).transpose(1, 0, 2)
    args = [x3d, bias, lw["sinks"], lw["g_mix"], lw["w_in"], w_gate_slabs, lw["conv_w"], lw["conv_b"], lw["dt_bias"],
            lw["a"], lw["d_skip"], lw["g_ssd"], lw["w_att_out"], lw["w_ssd_out"], lw["w_out"]]
    if has_init:
        in_specs += [
            pl.BlockSpec((n_seq, WINDOW, 2 * KV_WIDTH), seq_map),
            pl.BlockSpec((n_seq, HIST_ROWS, CONV_DIM), seq_map),
            pl.BlockSpec((n_seq, SSD_INNER, SSD_STATE), seq_map),
        ]
        args += list(init)
    out_shape = [
        jax.ShapeDtypeStruct((batch, seq_pad, D_MODEL), F32),
        jax.ShapeDtypeStruct((batch, WINDOW, 2 * KV_WIDTH), F32),
        jax.ShapeDtypeStruct((batch, HIST_ROWS, CONV_DIM), F32),
        jax.ShapeDtypeStruct((batch, SSD_INNER, SSD_STATE), F32),
    ]
    out_specs = [
        pl.BlockSpec((n_seq, seq_rows, D_MODEL), tile_map),
        pl.BlockSpec((n_seq, WINDOW, 2 * KV_WIDTH), seq_map),
        pl.BlockSpec((n_seq, HIST_ROWS, CONV_DIM), seq_map),
        pl.BlockSpec((n_seq, SSD_INNER, SSD_STATE), seq_map),
    ]
    scratch = [
        pltpu.VMEM((rows, D_MODEL), BF16),
        pltpu.VMEM((rows, ATT_WIDTH), F32),
        pltpu.VMEM((n_seq, WINDOW + seq_rows, 2 * KV_WIDTH), F32),
        pltpu.VMEM((n_seq, HIST_ROWS + seq_rows, CONV_DIM), F32),
        pltpu.VMEM((rows, LANES), F32),
        pltpu.VMEM((rows, ATT_WIDTH), F32),
        pltpu.VMEM((rows, SSD_INNER), F32),
        pltpu.VMEM((n_seq, SSD_STATE, SSD_INNER), F32),
        pltpu.VMEM((n_slab, rows, GATE_WIDTH // n_slab), F32),
    ]
    kern = functools.partial(_mixer_kernel, n_seq=n_seq, n_chunk=n_chunk, n_tiles=n_tiles,
                             nvalid=last_valid, has_init=has_init)
    return pl.pallas_call(
        kern,
        grid=(batch // n_seq, n_tiles),
        in_specs=in_specs,
        out_specs=out_specs,
        out_shape=out_shape,
        scratch_shapes=scratch,
        compiler_params=pltpu.CompilerParams(
            dimension_semantics=("arbitrary", "arbitrary"), vmem_limit_bytes=VMEM_LIMIT_BYTES),
        name="mixer_init" if has_init else "mixer",
    )(*args)


def _ffn_kernel(h_ref, g_ref, wg_ref, wu_ref, wd_ref, gfin_ref, o_ref, *, final):
    h = h_ref[...]
    hf = _rmsnorm(h, g_ref[...]).astype(BF16)
    act = (_silu(_dot(hf, wg_ref[...])) * _dot(hf, wu_ref[...])).astype(BF16)
    out = h + _dot(act, wd_ref[...])
    if final:
        out = _rmsnorm(out, gfin_ref[...])
    o_ref[...] = out


def _ffn(h2d, lw, g_final, *, block_rows, final):
    n_rows = h2d.shape[0]
    assert n_rows % block_rows == 0
    const = lambda shape: pl.BlockSpec(shape, lambda i: (0, 0), pipeline_mode=pl.Buffered(1))
    return pl.pallas_call(
        functools.partial(_ffn_kernel, final=final),
        grid=(n_rows // block_rows,),
        in_specs=[
            pl.BlockSpec((block_rows, D_MODEL), lambda i: (i, 0)),
            const((1, D_MODEL)),
            const((D_MODEL, D_FF)),
            const((D_MODEL, D_FF)),
            const((D_FF, D_MODEL)),
            const((1, D_MODEL)),
        ],
        out_specs=pl.BlockSpec((block_rows, D_MODEL), lambda i: (i, 0)),
        out_shape=jax.ShapeDtypeStruct((n_rows, D_MODEL), F32),
        compiler_params=pltpu.CompilerParams(
            dimension_semantics=("arbitrary",), vmem_limit_bytes=VMEM_LIMIT_BYTES),
        name="ffn_final" if final else "ffn",
    )(h2d, lw["g_ffn"], lw["w_gate"], lw["w_up"], lw["w_down"], g_final)


def _pack_w_in(w):
    o = 0
    parts = {}
    for name, width in (("q", ATT_WIDTH), ("k", KV_WIDTH), ("v", KV_WIDTH), ("z", SSD_INNER),
                        ("xbc", CONV_DIM), ("dt", SSD_HEADS), ("ga", D_MODEL), ("gs", D_MODEL)):
        parts[name] = w[:, o:o + width]
        o += width
    dt = jnp.pad(parts["dt"], ((0, 0), (0, LANES - SSD_HEADS)))
    packed = jnp.concatenate([parts["q"], parts["k"], parts["v"], parts["xbc"], dt], axis=1)
    gates = jnp.concatenate([parts["z"], parts["ga"], parts["gs"]], axis=1)
    return packed.astype(BF16), gates.astype(BF16)


def _layer_weights(l, g_mix, w_in, conv_w, conv_b, dt_bias, a_log, d_skip, g_ssd, sinks, w_att_out,
                   w_ssd_out, w_out, g_ffn, w_gate, w_up, w_down):
    pad_h = ((0, LANES - SSD_HEADS),)
    w_packed, w_gates = _pack_w_in(w_in[l])
    return {
        "sinks": sinks[l],
        "g_mix": g_mix[l][None, :],
        "w_in": w_packed,
        "w_gates": w_gates,
        "conv_w": conv_w[l],
        "conv_b": conv_b[l][None, :],
        "dt_bias": jnp.pad(dt_bias[l], pad_h)[None, :],
        "a": jnp.pad(-jnp.exp(a_log[l]), pad_h)[None, :],
        "d_skip": jnp.repeat(d_skip[l], SSD_HEAD_DIM)[None, :],
        "g_ssd": g_ssd[l][None, :],
        "w_att_out": w_att_out[l].astype(BF16),
        "w_ssd_out": w_ssd_out[l].astype(BF16),
        "w_out": w_out[l].astype(BF16),
        "g_ffn": g_ffn[l][None, :],
        "w_gate": w_gate[l].astype(BF16),
        "w_up": w_up[l].astype(BF16),
        "w_down": w_down[l].astype(BF16),
    }


PROMPT_SEQS_PER_TILE = 4
PROMPT_CHUNKS_PER_TILE = 2
SAMPLE_SEQS_PER_TILE = 4
FFN_BLOCK_ROWS = 512


def kernel(x_prompt, x_sample, cache_k, cache_v, state_conv, state_ssm, rel_table, g_mix, w_in, conv_w, conv_b, dt_bias, a_log, d_skip, g_ssd, sinks, w_att_out, w_ssd_out, w_out, g_ffn, w_gate, w_up, w_down, g_final):
    depth = w_in.shape[0]
    bp, sp, _ = x_prompt.shape
    bs, ts, _ = x_sample.shape
    kv_len = cache_k.shape[2]
    assert kv_len == WINDOW and ts <= CHUNK and ts % 8 == 0 and ts >= CONV_W - 1

    bias = _blocked_bias(rel_table)
    g_fin = g_final[None, :]

    xp = x_prompt
    xs = x_sample.reshape(bs * ts, D_MODEL)
    st_p, st_s = [], []
    for l in range(depth):
        lw = _layer_weights(l, g_mix, w_in, conv_w, conv_b, dt_bias, a_log, d_skip, g_ssd, sinks,
                            w_att_out, w_ssd_out, w_out, g_ffn, w_gate, w_up, w_down)
        final = l == depth - 1
        hp, kvp, convp, ssmp = _mixer(xp, bias, lw, None, n_seq=PROMPT_SEQS_PER_TILE,
                                      n_chunk=PROMPT_CHUNKS_PER_TILE, nvalid=sp)
        xp = _ffn(hp.reshape(bp * sp, D_MODEL), lw, g_fin, block_rows=FFN_BLOCK_ROWS,
                  final=final).reshape(bp, sp, D_MODEL)
        st_p.append((kvp, convp, ssmp))
        xs_pad = jnp.pad(xs.reshape(bs, ts, D_MODEL), ((0, 0), (0, CHUNK - ts), (0, 0)))
        init = (
            jnp.concatenate([cache_k[l].reshape(bs, WINDOW, KV_WIDTH),
                             cache_v[l].reshape(bs, WINDOW, KV_WIDTH)], axis=-1),
            jnp.pad(state_conv[l], ((0, 0), (HIST_ROWS - (CONV_W - 1), 0), (0, 0))),
            state_ssm[l].reshape(bs, SSD_INNER, SSD_STATE),
        )
        hs, kvs, convs, ssms = _mixer(xs_pad, bias, lw, init, n_seq=SAMPLE_SEQS_PER_TILE, n_chunk=1,
                                      nvalid=ts)
        xs = _ffn(hs[:, :ts].reshape(bs * ts, D_MODEL), lw, g_fin, block_rows=bs * ts, final=final)
        st_s.append((kvs, convs, ssms))

    def states(sts, b):
        kv = jnp.stack([s[0] for s in sts])
        k = kv[..., :KV_WIDTH].reshape(depth, b, WINDOW, N_KV_HEADS, HEAD_DIM)
        v = kv[..., KV_WIDTH:].reshape(depth, b, WINDOW, N_KV_HEADS, HEAD_DIM)
        conv = jnp.stack([s[1] for s in sts])[:, :, HIST_ROWS - (CONV_W - 1):, :]
        ssm = jnp.stack([s[2] for s in sts]).reshape(depth, b, SSD_HEADS, SSD_HEAD_DIM, SSD_STATE)
        return k, v, conv, ssm

    kp, vp, cp, sp_state = states(st_p, bp)
    ks, vs, cs, ss_state = states(st_s, bs)
    return (xp, xs.reshape(x_sample.shape), kp, vp, cp, sp_state, ks, vs, cs, ss_state)
```

```python
import functools
import math

import jax
import jax.numpy as jnp
from jax import lax
from jax.experimental import pallas as pl
from jax.experimental.pallas import tpu as pltpu

D_MODEL = 1024
CHUNK = 64
EPS = 1e-6
NEG_INF = -1e30
N_HEADS = 8
N_KV_HEADS = 2
HEAD_DIM = 64
ATT_WIDTH = N_HEADS * HEAD_DIM
KV_WIDTH = N_KV_HEADS * HEAD_DIM
WINDOW = 128
N_KEYS = WINDOW + CHUNK
NUM_BUCKETS = 32
MAX_DISTANCE = 128
SSD_INNER = 1024
SSD_HEADS = 16
SSD_HEAD_DIM = 64
SSD_GROUPS = 2
SSD_STATE = 128
GROUP_WIDTH = SSD_INNER // SSD_GROUPS
CONV_W = 4
CONV_DIM = SSD_INNER + 2 * SSD_GROUPS * SSD_STATE
D_FF = 2816
LANES = 128
HIST_ROWS = 8
MXU_WIDTH = 256
FILL_POINTS = 7

OFF_Q = 0
OFF_KV = OFF_Q + ATT_WIDTH
OFF_XBC = OFF_KV + 2 * KV_WIDTH
OFF_DT = OFF_XBC + CONV_DIM
IN_PACKED = OFF_DT + LANES
GATE_Z = 0
GATE_A = GATE_Z + SSD_INNER
GATE_S = GATE_A + D_MODEL
GATE_WIDTH = GATE_S + D_MODEL
GATE_SLABS = 2

VMEM_LIMIT_BYTES = 60 * 1024 * 1024

F32 = jnp.float32
BF16 = jnp.bfloat16


def _dot(a, b):
    return jnp.dot(a, b, preferred_element_type=F32)


def _dot_nt(a, b):
    return lax.dot_general(a, b, (((1,), (1,)), ((), ())), preferred_element_type=F32)


def _split3(x):
    hi = x.astype(BF16)
    r1 = x - hi.astype(F32)
    mid = r1.astype(BF16)
    lo = (r1 - mid.astype(F32)).astype(BF16)
    return hi, mid, lo


def _exact_dot_right(x, sel, passes=3):
    parts = _split3(x)[:passes]
    out = _dot(parts[0], sel)
    for part in parts[1:]:
        out = out + _dot(part, sel)
    return out


def _exact_dot_left(sel, x):
    hi, mid, lo = _split3(x)
    return _dot(sel, hi) + _dot(sel, mid) + _dot(sel, lo)


def _rmsnorm(x, g):
    return x * lax.rsqrt(jnp.mean(x * x, axis=-1, keepdims=True) + EPS) * g


def _sigmoid(x):
    return 1.0 / (1.0 + jnp.exp(-x))


def _silu(x):
    return x * _sigmoid(x)


def _softplus(x):
    return jnp.maximum(x, 0.0) + jnp.log1p(jnp.exp(-jnp.abs(x)))


def _bias_kernel(table_ref, bucket_ref, o_ref):
    bucket = bucket_ref[...]
    rows = lax.broadcasted_iota(jnp.int32, bucket.shape, 0)
    cols = lax.broadcasted_iota(jnp.int32, bucket.shape, 1)
    row_hi = rows >= CHUNK
    col_hi = cols >= N_KEYS
    for g in range(N_KV_HEADS):
        acc = jnp.zeros(bucket.shape, F32)
        for b in range(NUM_BUCKETS):
            t0 = table_ref[b, 4 * g + 0]
            t1 = table_ref[b, 4 * g + 1]
            t2 = table_ref[b, 4 * g + 2]
            t3 = table_ref[b, 4 * g + 3]
            tv = jnp.where(row_hi, jnp.where(col_hi, t3, t2), jnp.where(col_hi, t1, t0))
            acc = jnp.where(bucket == b, tv, acc)
        o_ref[g] = acc


def _t5_bucket(rel):
    nb = NUM_BUCKETS // 2
    max_exact = nb // 2
    ret = jnp.where(rel > 0, nb, 0)
    n = jnp.abs(rel)
    nf = jnp.maximum(n, 1).astype(jnp.float32)
    large = max_exact + (jnp.log(nf / max_exact) / math.log(MAX_DISTANCE / max_exact)
                         * (nb - max_exact)).astype(jnp.int32)
    large = jnp.minimum(large, nb - 1)
    return ret + jnp.where(n < max_exact, n, large)


def _blocked_bias(rel_table):
    qi = jnp.arange(CHUNK, dtype=jnp.int32)
    kj = jnp.arange(N_KEYS, dtype=jnp.int32)
    bucket = _t5_bucket(kj[None, :] - WINDOW - qi[:, None]).astype(jnp.int32)
    bucket = jnp.tile(bucket, (2, 2))
    return pl.pallas_call(
        _bias_kernel,
        out_shape=jax.ShapeDtypeStruct((N_KV_HEADS, 2 * CHUNK, 2 * N_KEYS), F32),
        in_specs=[pl.BlockSpec(memory_space=pltpu.SMEM),
                  pl.BlockSpec(memory_space=pltpu.VMEM)],
        out_specs=pl.BlockSpec(memory_space=pltpu.VMEM),
        name="rel_bias",
    )(rel_table, bucket)


def _blocked_kv(win, g):
    lo_half = lax.broadcasted_iota(jnp.int32, (N_KEYS, LANES), 1) < HEAD_DIM
    if g == 0:
        a0 = jnp.where(lo_half, win, 0.0)
        a1 = pltpu.roll(a0, HEAD_DIM, axis=1)
    else:
        a1 = jnp.where(lo_half, 0.0, win)
        a0 = pltpu.roll(a1, HEAD_DIM, axis=1)
    return jnp.concatenate([a0, a1], axis=0).astype(BF16)


def _attention_logits(q, kwin, bias_ref, g):
    qs = jnp.concatenate([q[:, 256 * g:256 * g + LANES],
                          q[:, 256 * g + LANES:256 * (g + 1)]], axis=0).astype(BF16)
    return _dot_nt(qs, _blocked_kv(kwin, g)) + bias_ref[g]


def _attention_finish(logits, vwin, sinks_ref, valid, g):
    head0 = lax.broadcasted_iota(jnp.int32, (2 * CHUNK, 2 * N_KEYS), 1) < N_KEYS
    row_lo = lax.broadcasted_iota(jnp.int32, (2 * CHUNK, 1), 0) < CHUNK
    lane_o = lax.broadcasted_iota(jnp.int32, (2 * CHUNK, LANES), 1) < HEAD_DIM
    logits = jnp.where(valid, logits, NEG_INF)
    s0 = jnp.where(row_lo, sinks_ref[4 * g + 0], sinks_ref[4 * g + 2])
    s1 = jnp.where(row_lo, sinks_ref[4 * g + 1], sinks_ref[4 * g + 3])
    m0 = jnp.max(jnp.where(head0, logits, -jnp.inf), axis=1, keepdims=True)
    m1 = jnp.max(jnp.where(head0, -jnp.inf, logits), axis=1, keepdims=True)
    m0 = jnp.maximum(m0, s0)
    m1 = jnp.maximum(m1, s1)
    p = jnp.exp(logits - jnp.where(head0, m0, m1))
    d0 = jnp.sum(jnp.where(head0, p, 0.0), axis=1, keepdims=True) + jnp.exp(s0 - m0)
    d1 = jnp.sum(jnp.where(head0, 0.0, p), axis=1, keepdims=True) + jnp.exp(s1 - m1)
    o = _dot(p.astype(BF16), _blocked_kv(vwin, g)) / jnp.where(lane_o, d0, d1)
    return jnp.concatenate([o[:CHUNK], o[CHUNK:]], axis=1)


def _ssd_constants():
    r = lax.broadcasted_iota(jnp.int32, (CHUNK, CHUNK), 0)
    c = lax.broadcasted_iota(jnp.int32, (CHUNK, CHUNK), 1)
    tri = jnp.where(c <= r, 1.0, 0.0).astype(BF16)
    hrow = lax.broadcasted_iota(jnp.int32, (LANES, SSD_INNER), 0)
    hcol = lax.broadcasted_iota(jnp.int32, (LANES, SSD_INNER), 1)
    expand = jnp.where(hcol >> 6 == hrow, 1.0, 0.0).astype(BF16)
    return tri, expand


def _ssd_prepare(c, dt_raw, dtb_ref, a_ref, nvalid, tri, expand):
    dt = _softplus(dt_raw + dtb_ref[...])
    if nvalid < CHUNK:
        rows = lax.broadcasted_iota(jnp.int32, dt.shape, 0)
        dt = jnp.where(rows < nvalid, dt, 0.0)
    acum = _exact_dot_left(tri, dt * a_ref[...])
    c["dt_exp"] = _exact_dot_right(dt, expand, passes=1)
    c["a_col"] = _exact_dot_right(acum, expand, passes=2)


def _ssd_decay(c):
    a_col = c["a_col"]
    a_last = a_col[CHUNK - 1:CHUNK, :]
    c["xs"] = c["xbc_act"][:, :SSD_INNER]
    c["xdt"] = c["xs"] * c.pop("dt_exp")
    c["xw"] = c["xdt"] * jnp.exp(a_last - a_col)
    c["e_col"] = jnp.exp(a_col)
    c["e_last"] = jnp.exp(a_last)


def _ssd_group(c, gq, ht_ref, g_idx):
    l2 = lax.broadcasted_iota(jnp.int32, (CHUNK, LANES), 0)
    j2 = lax.broadcasted_iota(jnp.int32, (CHUNK, LANES), 1)
    s2 = j2 & (CHUNK - 1)
    diag_sel = jnp.where(s2 == l2, 1.0, 0.0)
    causal2 = s2 <= l2
    lane_lo = j2 < SSD_HEAD_DIM

    b0 = SSD_INNER + SSD_STATE * gq
    c0 = SSD_INNER + SSD_GROUPS * SSD_STATE + SSD_STATE * gq
    bg = c["xbc_act"][:, b0:b0 + SSD_STATE]
    cg = c["xbc_act"][:, c0:c0 + SSD_STATE].astype(BF16)
    bg_bf = bg.astype(BF16)
    cb2 = _dot_nt(cg, jnp.concatenate([bg_bf, bg_bf], axis=0))
    gsl = slice(GROUP_WIDTH * gq, GROUP_WIDTH * (gq + 1))
    h_prev = ht_ref[g_idx, :, gsl]
    y_off = _dot(cg, h_prev.astype(BF16)) * c["e_col"][:, gsl]
    yd = []
    for i in range(GROUP_WIDTH // LANES):
        psl = slice(GROUP_WIDTH * gq + LANES * i, GROUP_WIDTH * gq + LANES * (i + 1))
        ac = c["a_col"][:, psl]
        a_row = jnp.sum(ac * diag_sel, axis=0, keepdims=True)
        lmat = jnp.exp(jnp.where(causal2, ac - a_row, -jnp.inf))
        m2 = (cb2 * lmat).astype(BF16)
        xp = c["xdt"][:, psl]
        xblk = jnp.concatenate([jnp.where(lane_lo, xp, 0.0), jnp.where(lane_lo, 0.0, xp)],
                               axis=0).astype(BF16)
        yd.append(_dot(m2, xblk))
    c.setdefault("ys", []).append(jnp.concatenate(yd, axis=1) + y_off)
    st = _dot(bg.T.astype(BF16), c["xw"][:, gsl].astype(BF16))
    ht_ref[g_idx, :, gsl] = h_prev * c["e_last"][:, gsl] + st


class _LayerSinks:
    def __init__(self, ref, layer):
        self.ref, self.layer = ref, layer

    def __getitem__(self, head):
        return self.ref[self.layer, head]


def _mixer_kernel(*refs, layer, n_seq, n_chunk, n_tiles, nvalid, has_init):
    it = iter(refs)
    x_ref = next(it)
    bias_ref = next(it)
    sinks_ref = _LayerSinks(next(it), layer)
    gmix_ref = next(it)
    win_ref = next(it)
    wgate_ref = next(it)
    convw_ref = next(it)
    convb_ref = next(it)
    dtb_ref = next(it)
    a_ref = next(it)
    dskip_ref = next(it)
    gssd_ref = next(it)
    watt_ref = next(it)
    wssd_ref = next(it)
    wout_ref = next(it)
    if has_init:
        ki_ref = next(it)
        vi_ref = next(it)
        convi_ref = next(it)
        ssmi_ref = next(it)
    h_ref = next(it)
    ko_ref = next(it)
    vo_ref = next(it)
    convo_ref = next(it)
    ssmo_ref = next(it)
    hn_s = next(it)
    q_s = next(it)
    kv_s = next(it)
    xbc_s = next(it)
    dt_s = next(it)
    oatt_s = next(it)
    y_s = next(it)
    ht_s = next(it)
    gate_s = next(it)

    t = pl.program_id(1)
    seq_rows = n_chunk * CHUNK

    @pl.when(t == 0)
    def _init():
        if has_init:
            kv_s[:, 0:WINDOW, :KV_WIDTH] = ki_ref[...]
            kv_s[:, 0:WINDOW, KV_WIDTH:] = vi_ref[...]
            xbc_s[:, 0:HIST_ROWS, :] = jnp.zeros((n_seq, HIST_ROWS, CONV_DIM), F32)
            xbc_s[:, HIST_ROWS - (CONV_W - 1):HIST_ROWS, :] = convi_ref[...]
            for g in range(n_seq):
                ht_s[g] = ssmi_ref[g].T
        else:
            kv_s[:, 0:WINDOW, :] = jnp.zeros((n_seq, WINDOW, 2 * KV_WIDTH), F32)
            xbc_s[:, 0:HIST_ROWS, :] = jnp.zeros((n_seq, HIST_ROWS, CONV_DIM), F32)
            ht_s[...] = jnp.zeros(ht_s.shape, F32)

    rows = n_seq * seq_rows
    hn = _rmsnorm(x_ref[...].reshape(rows, D_MODEL), gmix_ref[...]).astype(BF16)
    hn_s[...] = hn
    xbc = _dot(hn, win_ref[:, OFF_XBC:OFF_DT])
    for g in range(n_seq):
        xbc_s[g, HIST_ROWS:HIST_ROWS + seq_rows, :] = xbc[g * seq_rows:(g + 1) * seq_rows]
    q_s[...] = _dot(hn, win_ref[:, OFF_Q:OFF_KV]) * (HEAD_DIM ** -0.5)
    kv = _dot(hn, win_ref[:, OFF_KV:OFF_XBC])
    for g in range(n_seq):
        kv_s[g, WINDOW:WINDOW + seq_rows, :] = kv[g * seq_rows:(g + 1) * seq_rows]
    dt_s[...] = _dot(hn, win_ref[:, OFF_DT:IN_PACKED])

    cols = lax.broadcasted_iota(jnp.int32, (2 * CHUNK, 2 * N_KEYS), 1)
    colmod = jnp.where(cols >= N_KEYS, cols - N_KEYS, cols)
    tri, expand = _ssd_constants()

    slab = GATE_WIDTH // GATE_SLABS
    slabs_per_iter = GATE_SLABS // n_chunk

    def iter_body(j, carry):
        k0 = pl.multiple_of(j * CHUNK, CHUNK) if n_chunk > 1 else 0
        per_slab = slab // MXU_WIDTH
        n_pieces = slabs_per_iter * per_slab
        n_points = FILL_POINTS * n_seq
        calls = [0]

        def fill():
            k = calls[0]
            calls[0] += 1
            for p in range(-(-k * n_pieces // n_points), -(-(k + 1) * n_pieces // n_points)):
                s = j * slabs_per_iter + p // per_slab
                c0 = (p % per_slab) * MXU_WIDTH
                gate_s[s, :, c0:c0 + MXU_WIDTH] = _dot(hn_s[...], wgate_ref[s, :, c0:c0 + MXU_WIDTH])

        if has_init:
            valid = colmod < WINDOW + nvalid
        else:
            first_valid = jnp.maximum(0, (2 - (t * n_chunk + j)) * CHUNK)
            valid = colmod >= first_valid
        chunks = [{"g": g, "rows": pl.ds(pl.multiple_of(g * seq_rows + k0, CHUNK), CHUNK)}
                  for g in range(n_seq)]
        for c in chunks:
            kvwin = kv_s[c["g"], pl.ds(k0, N_KEYS), :]
            c["vwin"] = kvwin[:, KV_WIDTH:]
            q = q_s[c["rows"], :]
            c["logits"] = [_attention_logits(q, kvwin[:, :KV_WIDTH], bias_ref, hg) for hg in range(N_KV_HEADS)]
            fill()
        for c in chunks:
            win = xbc_s[c["g"], pl.ds(k0, HIST_ROWS + CHUNK), :]
            conv = convb_ref[...] + convw_ref[CONV_W - 1:CONV_W, :] * win[HIST_ROWS:]
            for i in range(1, CONV_W):
                conv = conv + convw_ref[CONV_W - 1 - i:CONV_W - i, :] * win[HIST_ROWS - i:HIST_ROWS - i + CHUNK]
            c["xbc_act"] = _silu(conv)
            fill()
        for c in chunks:
            outs = [_attention_finish(c["logits"][hg], c["vwin"], sinks_ref, valid, hg)
                    for hg in range(N_KV_HEADS)]
            oatt_s[c["rows"], :] = jnp.concatenate(outs, axis=1)
            del c["logits"], c["vwin"]
            fill()
        for c in chunks:
            _ssd_prepare(c, dt_s[c["rows"], :], dtb_ref, a_ref, nvalid, tri, expand)
            fill()
        for c in chunks:
            _ssd_decay(c)
            fill()
        for gq in range(SSD_GROUPS):
            for c in chunks:
                _ssd_group(c, gq, ht_s, c["g"])
                fill()
        for c in chunks:
            y_s[c["rows"], :] = jnp.concatenate(c["ys"], axis=1) + c["xs"] * dskip_ref[...]
        assert calls[0] == n_points
        return carry

    lax.fori_loop(0, n_chunk, iter_body, 0)

    def gate_cols(lo, hi):
        pieces = []
        while lo < hi:
            s, off = divmod(lo, slab)
            take = min(hi - lo, slab - off)
            pieces.append(gate_s[s, :, off:off + take])
            lo += take
        return jnp.concatenate(pieces, axis=1)

    y_ssd = _rmsnorm(y_s[...] * _silu(gate_cols(GATE_Z, GATE_A)), gssd_ref[...]).astype(BF16)
    merged = (_sigmoid(gate_cols(GATE_A, GATE_S)) * _dot(oatt_s[...].astype(BF16), watt_ref[...])
              + _sigmoid(gate_cols(GATE_S, GATE_WIDTH)) * _dot(y_ssd, wssd_ref[...]))
    h = x_ref[...].reshape(rows, D_MODEL) + _dot(merged.astype(BF16), wout_ref[...])
    h_ref[...] = h.reshape(n_seq, seq_rows, D_MODEL)

    @pl.when(t == n_tiles - 1)
    def _emit_states():
        ko_ref[...] = kv_s[:, nvalid:nvalid + WINDOW, :KV_WIDTH]
        vo_ref[...] = kv_s[:, nvalid:nvalid + WINDOW, KV_WIDTH:]
        last = HIST_ROWS + nvalid
        convo_ref[...] = xbc_s[:, last - (CONV_W - 1):last, :]
        for g in range(n_seq):
            ssmo_ref[g] = ht_s[g].T

    if n_tiles > 1:
        kv_s[:, 0:WINDOW, :] = kv_s[:, seq_rows:seq_rows + WINDOW, :]
        xbc_s[:, 0:HIST_ROWS, :] = xbc_s[:, seq_rows:seq_rows + HIST_ROWS, :]


def _layer_spec(shape, layer):
    nd = len(shape)
    return pl.BlockSpec((None,) + tuple(shape), lambda *_, _l=layer, _nd=nd: (_l,) + (0,) * _nd,
                        pipeline_mode=pl.Buffered(1))


def _mixer(x3d, bias, wts, layer, init, *, n_seq, n_chunk, nvalid):
    batch, seq_pad, _ = x3d.shape
    seq_rows = n_chunk * CHUNK
    n_tiles = seq_pad // seq_rows
    assert batch % n_seq == 0 and seq_pad % seq_rows == 0 and GATE_SLABS % n_chunk == 0
    rows = n_seq * seq_rows
    has_init = init is not None
    last_valid = nvalid - (n_tiles - 1) * seq_rows

    tile_map = lambda b, t: (b, t, 0)
    seq_map = lambda b, t: (b, 0, 0)
    cache_map = lambda b, t: (layer, b, 0, 0)
    spec = functools.partial(_layer_spec, layer=layer)
    in_specs = [
        pl.BlockSpec((n_seq, seq_rows, D_MODEL), tile_map),
        pl.BlockSpec(bias.shape, lambda b, t: (0, 0, 0), pipeline_mode=pl.Buffered(1)),
        pl.BlockSpec(memory_space=pltpu.SMEM),
        spec((1, D_MODEL)),
        spec((D_MODEL, IN_PACKED)),
        spec((GATE_SLABS, D_MODEL, GATE_WIDTH // GATE_SLABS)),
        spec((CONV_W, CONV_DIM)),
        spec((1, CONV_DIM)),
        spec((1, LANES)),
        spec((1, LANES)),
        spec((1, SSD_INNER)),
        spec((1, SSD_INNER)),
        spec((ATT_WIDTH, D_MODEL)),
        spec((SSD_INNER, D_MODEL)),
        spec((D_MODEL, D_MODEL)),
    ]
    args = [x3d, bias, wts["sinks"], wts["g_mix"], wts["w_in"], wts["w_gates"], wts["conv_w"], wts["conv_b"],
            wts["dt_bias"], wts["a"], wts["d_skip"], wts["g_ssd"], wts["w_att_out"], wts["w_ssd_out"],
            wts["w_out"]]
    if has_init:
        in_specs += [
            pl.BlockSpec((None, n_seq, WINDOW, KV_WIDTH), cache_map),
            pl.BlockSpec((None, n_seq, WINDOW, KV_WIDTH), cache_map),
            pl.BlockSpec((None, n_seq, CONV_W - 1, CONV_DIM), cache_map),
            pl.BlockSpec((None, n_seq, SSD_INNER, SSD_STATE), cache_map),
        ]
        args += list(init)
    out_shape = [
        jax.ShapeDtypeStruct((batch, seq_pad, D_MODEL), F32),
        jax.ShapeDtypeStruct((batch, WINDOW, KV_WIDTH), F32),
        jax.ShapeDtypeStruct((batch, WINDOW, KV_WIDTH), F32),
        jax.ShapeDtypeStruct((batch, CONV_W - 1, CONV_DIM), F32),
        jax.ShapeDtypeStruct((batch, SSD_INNER, SSD_STATE), F32),
    ]
    out_specs = [
        pl.BlockSpec((n_seq, seq_rows, D_MODEL), tile_map),
        pl.BlockSpec((n_seq, WINDOW, KV_WIDTH), seq_map),
        pl.BlockSpec((n_seq, WINDOW, KV_WIDTH), seq_map),
        pl.BlockSpec((n_seq, CONV_W - 1, CONV_DIM), seq_map),
        pl.BlockSpec((n_seq, SSD_INNER, SSD_STATE), seq_map),
    ]
    scratch = [
        pltpu.VMEM((rows, D_MODEL), BF16),
        pltpu.VMEM((rows, ATT_WIDTH), F32),
        pltpu.VMEM((n_seq, WINDOW + seq_rows, 2 * KV_WIDTH), F32),
        pltpu.VMEM((n_seq, HIST_ROWS + seq_rows, CONV_DIM), F32),
        pltpu.VMEM((rows, LANES), F32),
        pltpu.VMEM((rows, ATT_WIDTH), F32),
        pltpu.VMEM((rows, SSD_INNER), F32),
        pltpu.VMEM((n_seq, SSD_STATE, SSD_INNER), F32),
        pltpu.VMEM((GATE_SLABS, rows, GATE_WIDTH // GATE_SLABS), F32),
    ]
    kern = functools.partial(_mixer_kernel, layer=layer, n_seq=n_seq, n_chunk=n_chunk, n_tiles=n_tiles,
                             nvalid=last_valid, has_init=has_init)
    return pl.pallas_call(
        kern,
        grid=(batch // n_seq, n_tiles),
        in_specs=in_specs,
        out_specs=out_specs,
        out_shape=out_shape,
        scratch_shapes=scratch,
        compiler_params=pltpu.CompilerParams(
            dimension_semantics=("arbitrary", "arbitrary"), vmem_limit_bytes=VMEM_LIMIT_BYTES),
        name="mixer_init" if has_init else "mixer",
    )(*args)


def _ffn_kernel(h_ref, g_ref, wg_ref, wu_ref, wd_ref, gfin_ref, o_ref, *, final):
    h = h_ref[...]
    hf = _rmsnorm(h, g_ref[...]).astype(BF16)
    act = (_silu(_dot(hf, wg_ref[...])) * _dot(hf, wu_ref[...])).astype(BF16)
    out = h + _dot(act, wd_ref[...])
    if final:
        out = _rmsnorm(out, gfin_ref[...])
    o_ref[...] = out


def _ffn(h2d, wts, layer, g_final, *, block_rows, final):
    n_rows = h2d.shape[0]
    assert n_rows % block_rows == 0
    spec = functools.partial(_layer_spec, layer=layer)
    return pl.pallas_call(
        functools.partial(_ffn_kernel, final=final),
        grid=(n_rows // block_rows,),
        in_specs=[
            pl.BlockSpec((block_rows, D_MODEL), lambda i: (i, 0)),
            spec((1, D_MODEL)),
            spec((D_MODEL, D_FF)),
            spec((D_MODEL, D_FF)),
            spec((D_FF, D_MODEL)),
            pl.BlockSpec((1, D_MODEL), lambda i: (0, 0), pipeline_mode=pl.Buffered(1)),
        ],
        out_specs=pl.BlockSpec((block_rows, D_MODEL), lambda i: (i, 0)),
        out_shape=jax.ShapeDtypeStruct((n_rows, D_MODEL), F32),
        compiler_params=pltpu.CompilerParams(
            dimension_semantics=("arbitrary",), vmem_limit_bytes=VMEM_LIMIT_BYTES),
        name="ffn_final" if final else "ffn",
    )(h2d, wts["g_ffn"], wts["w_gate"], wts["w_up"], wts["w_down"], g_final)


def _pack_w_in_kernel(w_ref, packed_ref, gates_ref):
    o_z = ATT_WIDTH + 2 * KV_WIDTH
    o_xbc = o_z + SSD_INNER
    o_dt = o_xbc + CONV_DIM
    o_ga = o_dt + SSD_HEADS
    o_gs = o_ga + D_MODEL
    packed_ref[:, OFF_Q:OFF_XBC] = w_ref[:, 0:o_z].astype(BF16)
    packed_ref[:, OFF_XBC:OFF_DT] = w_ref[:, o_xbc:o_dt].astype(BF16)
    lane = lax.broadcasted_iota(jnp.int32, (w_ref.shape[0], LANES), 1)
    packed_ref[:, OFF_DT:IN_PACKED] = jnp.where(lane < SSD_HEADS, w_ref[:, o_dt:o_dt + LANES], 0.0).astype(BF16)
    slab = GATE_WIDTH // GATE_SLABS
    for dst, src, width in ((GATE_Z, o_z, SSD_INNER), (GATE_A, o_ga, D_MODEL), (GATE_S, o_gs, D_MODEL)):
        done = 0
        while done < width:
            s, off = divmod(dst + done, slab)
            take = min(width - done, slab - off)
            gates_ref[s, :, off:off + take] = w_ref[:, src + done:src + done + take].astype(BF16)
            done += take


def _pack_w_in(w_in):
    depth, d, width = w_in.shape
    slab = GATE_WIDTH // GATE_SLABS
    return pl.pallas_call(
        _pack_w_in_kernel,
        grid=(depth, d // PREP_ROWS),
        in_specs=[pl.BlockSpec((None, PREP_ROWS, width), lambda l, i: (l, i, 0))],
        out_specs=[pl.BlockSpec((None, PREP_ROWS, IN_PACKED), lambda l, i: (l, i, 0)),
                   pl.BlockSpec((None, GATE_SLABS, PREP_ROWS, slab), lambda l, i: (l, 0, i, 0))],
        out_shape=[jax.ShapeDtypeStruct((depth, d, IN_PACKED), BF16),
                   jax.ShapeDtypeStruct((depth, GATE_SLABS, d, slab), BF16)],
        compiler_params=pltpu.CompilerParams(dimension_semantics=("arbitrary", "arbitrary")),
        name="pack_w_in",
    )(w_in)


def _cast_kernel(w_ref, o_ref):
    o_ref[...] = w_ref[...].astype(BF16)


def _cast_bf16(w):
    depth, r, c = w.shape
    assert (depth * r) % PREP_ROWS == 0
    out = pl.pallas_call(
        _cast_kernel,
        grid=(depth * r // PREP_ROWS,),
        in_specs=[pl.BlockSpec((PREP_ROWS, c), lambda i: (i, 0))],
        out_specs=pl.BlockSpec((PREP_ROWS, c), lambda i: (i, 0)),
        out_shape=jax.ShapeDtypeStruct((depth * r, c), BF16),
        compiler_params=pltpu.CompilerParams(dimension_semantics=("arbitrary",)),
        name="cast_bf16",
    )(w.reshape(depth * r, c))
    return out.reshape(depth, r, c)


def _prepare_weights(g_mix, w_in, conv_w, conv_b, dt_bias, a_log, d_skip, g_ssd, sinks, w_att_out, w_ssd_out,
                     w_out, g_ffn, w_gate, w_up, w_down):
    pad_h = ((0, 0), (0, LANES - SSD_HEADS))
    w_packed, w_gates = _pack_w_in(w_in)
    return {
        "sinks": sinks,
        "g_mix": g_mix[:, None, :],
        "w_in": w_packed,
        "w_gates": w_gates,
        "conv_w": conv_w,
        "conv_b": conv_b[:, None, :],
        "dt_bias": jnp.pad(dt_bias, pad_h)[:, None, :],
        "a": jnp.pad(-jnp.exp(a_log), pad_h)[:, None, :],
        "d_skip": jnp.repeat(d_skip, SSD_HEAD_DIM, axis=1)[:, None, :],
        "g_ssd": g_ssd[:, None, :],
        "w_att_out": _cast_bf16(w_att_out),
        "w_ssd_out": _cast_bf16(w_ssd_out),
        "w_out": _cast_bf16(w_out),
        "g_ffn": g_ffn[:, None, :],
        "w_gate": _cast_bf16(w_gate),
        "w_up": _cast_bf16(w_up),
        "w_down": _cast_bf16(w_down),
    }


PREP_ROWS = 256
PROMPT_SEQS_PER_TILE = 4
PROMPT_CHUNKS_PER_TILE = 2
SAMPLE_SEQS_PER_TILE = 4
FFN_BLOCK_ROWS = 512


def kernel(x_prompt, x_sample, cache_k, cache_v, state_conv, state_ssm, rel_table, g_mix, w_in, conv_w, conv_b, dt_bias, a_log, d_skip, g_ssd, sinks, w_att_out, w_ssd_out, w_out, g_ffn, w_gate, w_up, w_down, g_final):
    depth = w_in.shape[0]
    bp, sp, _ = x_prompt.shape
    bs, ts, _ = x_sample.shape
    kv_len = cache_k.shape[2]
    assert kv_len == WINDOW and ts <= CHUNK and ts % 8 == 0 and ts >= CONV_W - 1

    bias = _blocked_bias(rel_table)
    g_fin = g_final[None, :]

    wts = _prepare_weights(g_mix, w_in, conv_w, conv_b, dt_bias, a_log, d_skip, g_ssd, sinks, w_att_out,
                           w_ssd_out, w_out, g_ffn, w_gate, w_up, w_down)
    init = (cache_k.reshape(depth, bs, WINDOW, KV_WIDTH), cache_v.reshape(depth, bs, WINDOW, KV_WIDTH),
            state_conv, state_ssm.reshape(depth, bs, SSD_INNER, SSD_STATE))

    xp = x_prompt
    xs = x_sample.reshape(bs * ts, D_MODEL)
    st_p, st_s = [], []
    for l in range(depth):
        final = l == depth - 1
        hp, *state_p = _mixer(xp, bias, wts, l, None, n_seq=PROMPT_SEQS_PER_TILE,
                              n_chunk=PROMPT_CHUNKS_PER_TILE, nvalid=sp)
        xp = _ffn(hp.reshape(bp * sp, D_MODEL), wts, l, g_fin, block_rows=FFN_BLOCK_ROWS,
                  final=final).reshape(bp, sp, D_MODEL)
        st_p.append(state_p)
        xs_pad = jnp.pad(xs.reshape(bs, ts, D_MODEL), ((0, 0), (0, CHUNK - ts), (0, 0)))
        hs, *state_s = _mixer(xs_pad, bias, wts, l, init, n_seq=SAMPLE_SEQS_PER_TILE, n_chunk=1, nvalid=ts)
        xs = _ffn(hs[:, :ts].reshape(bs * ts, D_MODEL), wts, l, g_fin, block_rows=bs * ts, final=final)
        st_s.append(state_s)

    def states(sts, b):
        k, v, conv, ssm = (jnp.stack(leaves) for leaves in zip(*sts))
        return (k.reshape(depth, b, WINDOW, N_KV_HEADS, HEAD_DIM),
                v.reshape(depth, b, WINDOW, N_KV_HEADS, HEAD_DIM), conv,
                ssm.reshape(depth, b, SSD_HEADS, SSD_HEAD_DIM, SSD_STATE))

    return (xp, xs.reshape(x_sample.shape), *states(st_p, bp), *states(st_s, bs))
```

```python
import functools
import math

import jax
import jax.numpy as jnp
from jax import lax
from jax.experimental import pallas as pl
from jax.experimental.pallas import tpu as pltpu

D_MODEL = 1024
CHUNK = 64
EPS = 1e-6
NEG_INF = -1e30
N_HEADS = 8
N_KV_HEADS = 2
HEAD_DIM = 64
ATT_WIDTH = N_HEADS * HEAD_DIM
KV_WIDTH = N_KV_HEADS * HEAD_DIM
WINDOW = 128
N_KEYS = WINDOW + CHUNK
NUM_BUCKETS = 32
MAX_DISTANCE = 128
SSD_INNER = 1024
SSD_HEADS = 16
SSD_HEAD_DIM = 64
SSD_GROUPS = 2
SSD_STATE = 128
GROUP_WIDTH = SSD_INNER // SSD_GROUPS
CONV_W = 4
CONV_DIM = SSD_INNER + 2 * SSD_GROUPS * SSD_STATE
D_FF = 2816
LANES = 128
HIST_ROWS = 8
MXU_WIDTH = 256
FILL_POINTS = 7

OFF_Q = 0
OFF_KV = OFF_Q + ATT_WIDTH
OFF_XBC = OFF_KV + 2 * KV_WIDTH
OFF_DT = OFF_XBC + CONV_DIM
IN_PACKED = OFF_DT + LANES
GATE_Z = 0
GATE_A = GATE_Z + SSD_INNER
GATE_S = GATE_A + D_MODEL
GATE_WIDTH = GATE_S + D_MODEL
GATE_SLABS = 2

VMEM_LIMIT_BYTES = 60 * 1024 * 1024

F32 = jnp.float32
BF16 = jnp.bfloat16


def _dot(a, b):
    return jnp.dot(a, b, preferred_element_type=F32)


def _dot_nt(a, b):
    return lax.dot_general(a, b, (((1,), (1,)), ((), ())), preferred_element_type=F32)


def _split3(x):
    hi = x.astype(BF16)
    r1 = x - hi.astype(F32)
    mid = r1.astype(BF16)
    lo = (r1 - mid.astype(F32)).astype(BF16)
    return hi, mid, lo


def _exact_dot_right(x, sel, passes=3):
    parts = _split3(x)[:passes]
    out = _dot(parts[0], sel)
    for part in parts[1:]:
        out = out + _dot(part, sel)
    return out


def _exact_dot_left(sel, x):
    hi, mid, lo = _split3(x)
    return _dot(sel, hi) + _dot(sel, mid) + _dot(sel, lo)


def _rmsnorm(x, g):
    return x * lax.rsqrt(jnp.mean(x * x, axis=-1, keepdims=True) + EPS) * g


def _sigmoid(x):
    return 1.0 / (1.0 + jnp.exp(-x))


def _silu(x):
    return x * _sigmoid(x)


def _softplus(x):
    return jnp.maximum(x, 0.0) + jnp.log1p(jnp.exp(-jnp.abs(x)))


def _bias_kernel(table_ref, bucket_ref, o_ref):
    bucket = bucket_ref[...]
    rows = lax.broadcasted_iota(jnp.int32, bucket.shape, 0)
    cols = lax.broadcasted_iota(jnp.int32, bucket.shape, 1)
    row_hi = rows >= CHUNK
    col_hi = cols >= N_KEYS
    for g in range(N_KV_HEADS):
        acc = jnp.zeros(bucket.shape, F32)
        for b in range(NUM_BUCKETS):
            t0 = table_ref[b, 4 * g + 0]
            t1 = table_ref[b, 4 * g + 1]
            t2 = table_ref[b, 4 * g + 2]
            t3 = table_ref[b, 4 * g + 3]
            tv = jnp.where(row_hi, jnp.where(col_hi, t3, t2), jnp.where(col_hi, t1, t0))
            acc = jnp.where(bucket == b, tv, acc)
        o_ref[g] = acc


def _t5_bucket(rel):
    nb = NUM_BUCKETS // 2
    max_exact = nb // 2
    ret = jnp.where(rel > 0, nb, 0)
    n = jnp.abs(rel)
    nf = jnp.maximum(n, 1).astype(jnp.float32)
    large = max_exact + (jnp.log(nf / max_exact) / math.log(MAX_DISTANCE / max_exact)
                         * (nb - max_exact)).astype(jnp.int32)
    large = jnp.minimum(large, nb - 1)
    return ret + jnp.where(n < max_exact, n, large)


def _blocked_bias(rel_table):
    qi = jnp.arange(CHUNK, dtype=jnp.int32)
    kj = jnp.arange(N_KEYS, dtype=jnp.int32)
    bucket = _t5_bucket(kj[None, :] - WINDOW - qi[:, None]).astype(jnp.int32)
    bucket = jnp.tile(bucket, (2, 2))
    return pl.pallas_call(
        _bias_kernel,
        out_shape=jax.ShapeDtypeStruct((N_KV_HEADS, 2 * CHUNK, 2 * N_KEYS), F32),
        in_specs=[pl.BlockSpec(memory_space=pltpu.SMEM),
                  pl.BlockSpec(memory_space=pltpu.VMEM)],
        out_specs=pl.BlockSpec(memory_space=pltpu.VMEM),
        name="rel_bias",
    )(rel_table, bucket)


def _blocked_kv(win, g):
    lo_half = lax.broadcasted_iota(jnp.int32, (N_KEYS, LANES), 1) < HEAD_DIM
    if g == 0:
        a0 = jnp.where(lo_half, win, 0.0)
        a1 = pltpu.roll(a0, HEAD_DIM, axis=1)
    else:
        a1 = jnp.where(lo_half, 0.0, win)
        a0 = pltpu.roll(a1, HEAD_DIM, axis=1)
    return jnp.concatenate([a0, a1], axis=0).astype(BF16)


def _attention_logits(q, kwin, bias_ref, g):
    qs = jnp.concatenate([q[:, 256 * g:256 * g + LANES],
                          q[:, 256 * g + LANES:256 * (g + 1)]], axis=0).astype(BF16)
    return _dot_nt(qs, _blocked_kv(kwin, g)) + bias_ref[g]


def _attention_finish(logits, vwin, sinks_ref, valid, g):
    head0 = lax.broadcasted_iota(jnp.int32, (2 * CHUNK, 2 * N_KEYS), 1) < N_KEYS
    row_lo = lax.broadcasted_iota(jnp.int32, (2 * CHUNK, 1), 0) < CHUNK
    lane_o = lax.broadcasted_iota(jnp.int32, (2 * CHUNK, LANES), 1) < HEAD_DIM
    logits = jnp.where(valid, logits, NEG_INF)
    s0 = jnp.where(row_lo, sinks_ref[4 * g + 0], sinks_ref[4 * g + 2])
    s1 = jnp.where(row_lo, sinks_ref[4 * g + 1], sinks_ref[4 * g + 3])
    m0 = jnp.max(jnp.where(head0, logits, -jnp.inf), axis=1, keepdims=True)
    m1 = jnp.max(jnp.where(head0, -jnp.inf, logits), axis=1, keepdims=True)
    m0 = jnp.maximum(m0, s0)
    m1 = jnp.maximum(m1, s1)
    p = jnp.exp(logits - jnp.where(head0, m0, m1))
    d0 = jnp.sum(jnp.where(head0, p, 0.0), axis=1, keepdims=True) + jnp.exp(s0 - m0)
    d1 = jnp.sum(jnp.where(head0, 0.0, p), axis=1, keepdims=True) + jnp.exp(s1 - m1)
    o = _dot(p.astype(BF16), _blocked_kv(vwin, g)) / jnp.where(lane_o, d0, d1)
    return jnp.concatenate([o[:CHUNK], o[CHUNK:]], axis=1)


def _ssd_constants():
    r = lax.broadcasted_iota(jnp.int32, (CHUNK, CHUNK), 0)
    c = lax.broadcasted_iota(jnp.int32, (CHUNK, CHUNK), 1)
    tri = jnp.where(c <= r, 1.0, 0.0).astype(BF16)
    hrow = lax.broadcasted_iota(jnp.int32, (LANES, SSD_INNER), 0)
    hcol = lax.broadcasted_iota(jnp.int32, (LANES, SSD_INNER), 1)
    expand = jnp.where(hcol >> 6 == hrow, 1.0, 0.0).astype(BF16)
    return tri, expand


def _ssd_prepare(c, dt_raw, dtb_ref, a_ref, nvalid, tri, expand):
    dt = _softplus(dt_raw + dtb_ref[...])
    if nvalid < CHUNK:
        rows = lax.broadcasted_iota(jnp.int32, dt.shape, 0)
        dt = jnp.where(rows < nvalid, dt, 0.0)
    acum = _exact_dot_left(tri, dt * a_ref[...])
    c["dt_exp"] = _exact_dot_right(dt, expand, passes=1)
    c["a_col"] = _exact_dot_right(acum, expand, passes=2)


def _ssd_decay(c):
    a_col = c["a_col"]
    a_last = a_col[CHUNK - 1:CHUNK, :]
    c["xs"] = c["xbc_act"][:, :SSD_INNER]
    c["xdt"] = c["xs"] * c.pop("dt_exp")
    c["xw"] = c["xdt"] * jnp.exp(a_last - a_col)
    c["e_col"] = jnp.exp(a_col)
    c["e_last"] = jnp.exp(a_last)


def _ssd_group(c, gq, ht_ref, g_idx):
    l2 = lax.broadcasted_iota(jnp.int32, (CHUNK, LANES), 0)
    j2 = lax.broadcasted_iota(jnp.int32, (CHUNK, LANES), 1)
    s2 = j2 & (CHUNK - 1)
    diag_sel = jnp.where(s2 == l2, 1.0, 0.0)
    causal2 = s2 <= l2
    lane_lo = j2 < SSD_HEAD_DIM

    b0 = SSD_INNER + SSD_STATE * gq
    c0 = SSD_INNER + SSD_GROUPS * SSD_STATE + SSD_STATE * gq
    bg = c["xbc_act"][:, b0:b0 + SSD_STATE]
    cg = c["xbc_act"][:, c0:c0 + SSD_STATE].astype(BF16)
    bg_bf = bg.astype(BF16)
    cb2 = _dot_nt(cg, jnp.concatenate([bg_bf, bg_bf], axis=0))
    gsl = slice(GROUP_WIDTH * gq, GROUP_WIDTH * (gq + 1))
    h_prev = ht_ref[g_idx, :, gsl]
    y_off = _dot(cg, h_prev.astype(BF16)) * c["e_col"][:, gsl]
    yd = []
    for i in range(GROUP_WIDTH // LANES):
        psl = slice(GROUP_WIDTH * gq + LANES * i, GROUP_WIDTH * gq + LANES * (i + 1))
        ac = c["a_col"][:, psl]
        a_row = jnp.sum(ac * diag_sel, axis=0, keepdims=True)
        lmat = jnp.exp(jnp.where(causal2, ac - a_row, -jnp.inf))
        m2 = (cb2 * lmat).astype(BF16)
        xp = c["xdt"][:, psl]
        xblk = jnp.concatenate([jnp.where(lane_lo, xp, 0.0), jnp.where(lane_lo, 0.0, xp)],
                               axis=0).astype(BF16)
        yd.append(_dot(m2, xblk))
    c.setdefault("ys", []).append(jnp.concatenate(yd, axis=1) + y_off)
    st = _dot(bg.T.astype(BF16), c["xw"][:, gsl].astype(BF16))
    ht_ref[g_idx, :, gsl] = h_prev * c["e_last"][:, gsl] + st


class _LayerSinks:
    def __init__(self, ref, layer):
        self.ref, self.layer = ref, layer

    def __getitem__(self, head):
        return self.ref[self.layer, head]


def _mixer_kernel(*refs, layer, n_seq, n_chunk, n_tiles, nvalid, has_init):
    it = iter(refs)
    x_ref = next(it)
    bias_ref = next(it)
    sinks_ref = _LayerSinks(next(it), layer)
    gmix_ref = next(it)
    win_ref = next(it)
    wgate_ref = next(it)
    convw_ref = next(it)
    convb_ref = next(it)
    dtb_ref = next(it)
    a_ref = next(it)
    dskip_ref = next(it)
    gssd_ref = next(it)
    watt_ref = next(it)
    wssd_ref = next(it)
    wout_ref = next(it)
    if has_init:
        ki_ref = next(it)
        vi_ref = next(it)
        convi_ref = next(it)
        ssmi_ref = next(it)
    h_ref = next(it)
    ko_ref = next(it)
    vo_ref = next(it)
    convo_ref = next(it)
    ssmo_ref = next(it)
    hn_s = next(it)
    q_s = next(it)
    kv_s = next(it)
    xbc_s = next(it)
    dt_s = next(it)
    oatt_s = next(it)
    y_s = next(it)
    ht_s = next(it)
    gate_s = next(it)

    t = pl.program_id(1)
    seq_rows = n_chunk * CHUNK

    @pl.when(t == 0)
    def _init():
        if has_init:
            kv_s[:, 0:WINDOW, :KV_WIDTH] = ki_ref[...]
            kv_s[:, 0:WINDOW, KV_WIDTH:] = vi_ref[...]
            xbc_s[:, 0:HIST_ROWS, :] = jnp.zeros((n_seq, HIST_ROWS, CONV_DIM), F32)
            xbc_s[:, HIST_ROWS - (CONV_W - 1):HIST_ROWS, :] = convi_ref[...]
            for g in range(n_seq):
                ht_s[g] = ssmi_ref[g].T
        else:
            kv_s[:, 0:WINDOW, :] = jnp.zeros((n_seq, WINDOW, 2 * KV_WIDTH), F32)
            xbc_s[:, 0:HIST_ROWS, :] = jnp.zeros((n_seq, HIST_ROWS, CONV_DIM), F32)
            ht_s[...] = jnp.zeros(ht_s.shape, F32)

    rows = n_seq * seq_rows
    hn = _rmsnorm(x_ref[...].reshape(rows, D_MODEL), gmix_ref[...]).astype(BF16)
    hn_s[...] = hn
    xbc = _dot(hn, win_ref[:, OFF_XBC:OFF_DT])
    for g in range(n_seq):
        xbc_s[g, HIST_ROWS:HIST_ROWS + seq_rows, :] = xbc[g * seq_rows:(g + 1) * seq_rows]
    q_s[...] = _dot(hn, win_ref[:, OFF_Q:OFF_KV]) * (HEAD_DIM ** -0.5)
    kv = _dot(hn, win_ref[:, OFF_KV:OFF_XBC])
    for g in range(n_seq):
        kv_s[g, WINDOW:WINDOW + seq_rows, :] = kv[g * seq_rows:(g + 1) * seq_rows]
    dt_s[...] = _dot(hn, win_ref[:, OFF_DT:IN_PACKED])

    cols = lax.broadcasted_iota(jnp.int32, (2 * CHUNK, 2 * N_KEYS), 1)
    colmod = jnp.where(cols >= N_KEYS, cols - N_KEYS, cols)
    tri, expand = _ssd_constants()

    slab = GATE_WIDTH // GATE_SLABS
    slabs_per_iter = GATE_SLABS // n_chunk

    def iter_body(j, carry):
        k0 = pl.multiple_of(j * CHUNK, CHUNK) if n_chunk > 1 else 0
        per_slab = slab // MXU_WIDTH
        n_pieces = slabs_per_iter * per_slab
        n_points = FILL_POINTS * n_seq
        calls = [0]

        def fill():
            k = calls[0]
            calls[0] += 1
            for p in range(-(-k * n_pieces // n_points), -(-(k + 1) * n_pieces // n_points)):
                s = j * slabs_per_iter + p // per_slab
                c0 = (p % per_slab) * MXU_WIDTH
                gate_s[s, :, c0:c0 + MXU_WIDTH] = _dot(hn_s[...], wgate_ref[s, :, c0:c0 + MXU_WIDTH])

        if has_init:
            valid = colmod < WINDOW + nvalid
        else:
            first_valid = jnp.maximum(0, (2 - (t * n_chunk + j)) * CHUNK)
            valid = colmod >= first_valid
        chunks = [{"g": g, "rows": pl.ds(pl.multiple_of(g * seq_rows + k0, CHUNK), CHUNK)}
                  for g in range(n_seq)]
        for c in chunks:
            kvwin = kv_s[c["g"], pl.ds(k0, N_KEYS), :]
            c["vwin"] = kvwin[:, KV_WIDTH:]
            q = q_s[c["rows"], :]
            c["logits"] = [_attention_logits(q, kvwin[:, :KV_WIDTH], bias_ref, hg) for hg in range(N_KV_HEADS)]
            fill()
        for c in chunks:
            win = xbc_s[c["g"], pl.ds(k0, HIST_ROWS + CHUNK), :]
            conv = convb_ref[...] + convw_ref[CONV_W - 1:CONV_W, :] * win[HIST_ROWS:]
            for i in range(1, CONV_W):
                conv = conv + convw_ref[CONV_W - 1 - i:CONV_W - i, :] * win[HIST_ROWS - i:HIST_ROWS - i + CHUNK]
            c["xbc_act"] = _silu(conv)
            fill()
        for c in chunks:
            outs = [_attention_finish(c["logits"][hg], c["vwin"], sinks_ref, valid, hg)
                    for hg in range(N_KV_HEADS)]
            oatt_s[c["rows"], :] = jnp.concatenate(outs, axis=1)
            del c["logits"], c["vwin"]
            fill()
        for c in chunks:
            _ssd_prepare(c, dt_s[c["rows"], :], dtb_ref, a_ref, nvalid, tri, expand)
            fill()
        for c in chunks:
            _ssd_decay(c)
            fill()
        for gq in range(SSD_GROUPS):
            for c in chunks:
                _ssd_group(c, gq, ht_s, c["g"])
                fill()
        for c in chunks:
            y_s[c["rows"], :] = jnp.concatenate(c["ys"], axis=1) + c["xs"] * dskip_ref[...]
        assert calls[0] == n_points
        return carry

    for j in range(n_chunk):
        iter_body(j, 0)

    def gate_cols(lo, hi):
        pieces = []
        while lo < hi:
            s, off = divmod(lo, slab)
            take = min(hi - lo, slab - off)
            pieces.append(gate_s[s, :, off:off + take])
            lo += take
        return jnp.concatenate(pieces, axis=1)

    y_ssd = _rmsnorm(y_s[...] * _silu(gate_cols(GATE_Z, GATE_A)), gssd_ref[...]).astype(BF16)
    merged = (_sigmoid(gate_cols(GATE_A, GATE_S)) * _dot(oatt_s[...].astype(BF16), watt_ref[...])
              + _sigmoid(gate_cols(GATE_S, GATE_WIDTH)) * _dot(y_ssd, wssd_ref[...]))
    h = x_ref[...].reshape(rows, D_MODEL) + _dot(merged.astype(BF16), wout_ref[...])
    h_ref[...] = h.reshape(n_seq, seq_rows, D_MODEL)

    @pl.when(t == n_tiles - 1)
    def _emit_states():
        ko_ref[...] = kv_s[:, nvalid:nvalid + WINDOW, :KV_WIDTH]
        vo_ref[...] = kv_s[:, nvalid:nvalid + WINDOW, KV_WIDTH:]
        last = HIST_ROWS + nvalid
        convo_ref[...] = xbc_s[:, last - (CONV_W - 1):last, :]
        for g in range(n_seq):
            ssmo_ref[g] = ht_s[g].T

    if n_tiles > 1:
        kv_s[:, 0:WINDOW, :] = kv_s[:, seq_rows:seq_rows + WINDOW, :]
        xbc_s[:, 0:HIST_ROWS, :] = xbc_s[:, seq_rows:seq_rows + HIST_ROWS, :]


def _layer_spec(shape, layer):
    nd = len(shape)
    return pl.BlockSpec((None,) + tuple(shape), lambda *_, _l=layer, _nd=nd: (_l,) + (0,) * _nd,
                        pipeline_mode=pl.Buffered(1))


def _mixer(x3d, bias, wts, layer, init, *, n_seq, n_chunk, nvalid):
    batch, seq_pad, _ = x3d.shape
    seq_rows = n_chunk * CHUNK
    n_tiles = seq_pad // seq_rows
    assert batch % n_seq == 0 and seq_pad % seq_rows == 0 and GATE_SLABS % n_chunk == 0
    rows = n_seq * seq_rows
    has_init = init is not None
    last_valid = nvalid - (n_tiles - 1) * seq_rows

    tile_map = lambda b, t: (b, t, 0)
    seq_map = lambda b, t: (b, 0, 0)
    cache_map = lambda b, t: (layer, b, 0, 0)
    spec = functools.partial(_layer_spec, layer=layer)
    in_specs = [
        pl.BlockSpec((n_seq, seq_rows, D_MODEL), tile_map),
        pl.BlockSpec(bias.shape, lambda b, t: (0, 0, 0), pipeline_mode=pl.Buffered(1)),
        pl.BlockSpec(memory_space=pltpu.SMEM),
        spec((1, D_MODEL)),
        spec((D_MODEL, IN_PACKED)),
        spec((GATE_SLABS, D_MODEL, GATE_WIDTH // GATE_SLABS)),
        spec((CONV_W, CONV_DIM)),
        spec((1, CONV_DIM)),
        spec((1, LANES)),
        spec((1, LANES)),
        spec((1, SSD_INNER)),
        spec((1, SSD_INNER)),
        spec((ATT_WIDTH, D_MODEL)),
        spec((SSD_INNER, D_MODEL)),
        spec((D_MODEL, D_MODEL)),
    ]
    args = [x3d, bias, wts["sinks"], wts["g_mix"], wts["w_in"], wts["w_gates"], wts["conv_w"], wts["conv_b"],
            wts["dt_bias"], wts["a"], wts["d_skip"], wts["g_ssd"], wts["w_att_out"], wts["w_ssd_out"],
            wts["w_out"]]
    if has_init:
        in_specs += [
            pl.BlockSpec((None, n_seq, WINDOW, KV_WIDTH), cache_map),
            pl.BlockSpec((None, n_seq, WINDOW, KV_WIDTH), cache_map),
            pl.BlockSpec((None, n_seq, CONV_W - 1, CONV_DIM), cache_map),
            pl.BlockSpec((None, n_seq, SSD_INNER, SSD_STATE), cache_map),
        ]
        args += list(init)
    out_shape = [
        jax.ShapeDtypeStruct((batch, seq_pad, D_MODEL), F32),
        jax.ShapeDtypeStruct((batch, WINDOW, KV_WIDTH), F32),
        jax.ShapeDtypeStruct((batch, WINDOW, KV_WIDTH), F32),
        jax.ShapeDtypeStruct((batch, CONV_W - 1, CONV_DIM), F32),
        jax.ShapeDtypeStruct((batch, SSD_INNER, SSD_STATE), F32),
    ]
    out_specs = [
        pl.BlockSpec((n_seq, seq_rows, D_MODEL), tile_map),
        pl.BlockSpec((n_seq, WINDOW, KV_WIDTH), seq_map),
        pl.BlockSpec((n_seq, WINDOW, KV_WIDTH), seq_map),
        pl.BlockSpec((n_seq, CONV_W - 1, CONV_DIM), seq_map),
        pl.BlockSpec((n_seq, SSD_INNER, SSD_STATE), seq_map),
    ]
    scratch = [
        pltpu.VMEM((rows, D_MODEL), BF16),
        pltpu.VMEM((rows, ATT_WIDTH), F32),
        pltpu.VMEM((n_seq, WINDOW + seq_rows, 2 * KV_WIDTH), F32),
        pltpu.VMEM((n_seq, HIST_ROWS + seq_rows, CONV_DIM), F32),
        pltpu.VMEM((rows, LANES), F32),
        pltpu.VMEM((rows, ATT_WIDTH), F32),
        pltpu.VMEM((rows, SSD_INNER), F32),
        pltpu.VMEM((n_seq, SSD_STATE, SSD_INNER), F32),
        pltpu.VMEM((GATE_SLABS, rows, GATE_WIDTH // GATE_SLABS), F32),
    ]
    kern = functools.partial(_mixer_kernel, layer=layer, n_seq=n_seq, n_chunk=n_chunk, n_tiles=n_tiles,
                             nvalid=last_valid, has_init=has_init)
    return pl.pallas_call(
        kern,
        grid=(batch // n_seq, n_tiles),
        in_specs=in_specs,
        out_specs=out_specs,
        out_shape=out_shape,
        scratch_shapes=scratch,
        compiler_params=pltpu.CompilerParams(
            dimension_semantics=("arbitrary", "arbitrary"), vmem_limit_bytes=VMEM_LIMIT_BYTES),
        name="mixer_init" if has_init else "mixer",
    )(*args)


def _ffn_kernel(h_ref, g_ref, wg_ref, wu_ref, wd_ref, gfin_ref, o_ref, *, final):
    h = h_ref[...]
    hf = _rmsnorm(h, g_ref[...]).astype(BF16)
    act = (_silu(_dot(hf, wg_ref[...])) * _dot(hf, wu_ref[...])).astype(BF16)
    out = h + _dot(act, wd_ref[...])
    if final:
        out = _rmsnorm(out, gfin_ref[...])
    o_ref[...] = out


def _ffn(h2d, wts, layer, g_final, *, block_rows, final):
    n_rows = h2d.shape[0]
    assert n_rows % block_rows == 0
    spec = functools.partial(_layer_spec, layer=layer)
    return pl.pallas_call(
        functools.partial(_ffn_kernel, final=final),
        grid=(n_rows // block_rows,),
        in_specs=[
            pl.BlockSpec((block_rows, D_MODEL), lambda i: (i, 0)),
            spec((1, D_MODEL)),
            spec((D_MODEL, D_FF)),
            spec((D_MODEL, D_FF)),
            spec((D_FF, D_MODEL)),
            pl.BlockSpec((1, D_MODEL), lambda i: (0, 0), pipeline_mode=pl.Buffered(1)),
        ],
        out_specs=pl.BlockSpec((block_rows, D_MODEL), lambda i: (i, 0)),
        out_shape=jax.ShapeDtypeStruct((n_rows, D_MODEL), F32),
        compiler_params=pltpu.CompilerParams(
            dimension_semantics=("arbitrary",), vmem_limit_bytes=VMEM_LIMIT_BYTES),
        name="ffn_final" if final else "ffn",
    )(h2d, wts["g_ffn"], wts["w_gate"], wts["w_up"], wts["w_down"], g_final)


def _pack_w_in_kernel(w_ref, packed_ref, gates_ref):
    o_z = ATT_WIDTH + 2 * KV_WIDTH
    o_xbc = o_z + SSD_INNER
    o_dt = o_xbc + CONV_DIM
    o_ga = o_dt + SSD_HEADS
    o_gs = o_ga + D_MODEL
    packed_ref[:, OFF_Q:OFF_XBC] = w_ref[:, 0:o_z].astype(BF16)
    packed_ref[:, OFF_XBC:OFF_DT] = w_ref[:, o_xbc:o_dt].astype(BF16)
    lane = lax.broadcasted_iota(jnp.int32, (w_ref.shape[0], LANES), 1)
    packed_ref[:, OFF_DT:IN_PACKED] = jnp.where(lane < SSD_HEADS, w_ref[:, o_dt:o_dt + LANES], 0.0).astype(BF16)
    slab = GATE_WIDTH // GATE_SLABS
    for dst, src, width in ((GATE_Z, o_z, SSD_INNER), (GATE_A, o_ga, D_MODEL), (GATE_S, o_gs, D_MODEL)):
        done = 0
        while done < width:
            s, off = divmod(dst + done, slab)
            take = min(width - done, slab - off)
            gates_ref[s, :, off:off + take] = w_ref[:, src + done:src + done + take].astype(BF16)
            done += take


def _pack_w_in(w_in):
    depth, d, width = w_in.shape
    slab = GATE_WIDTH // GATE_SLABS
    return pl.pallas_call(
        _pack_w_in_kernel,
        grid=(depth, d // PREP_ROWS),
        in_specs=[pl.BlockSpec((None, PREP_ROWS, width), lambda l, i: (l, i, 0))],
        out_specs=[pl.BlockSpec((None, PREP_ROWS, IN_PACKED), lambda l, i: (l, i, 0)),
                   pl.BlockSpec((None, GATE_SLABS, PREP_ROWS, slab), lambda l, i: (l, 0, i, 0))],
        out_shape=[jax.ShapeDtypeStruct((depth, d, IN_PACKED), BF16),
                   jax.ShapeDtypeStruct((depth, GATE_SLABS, d, slab), BF16)],
        compiler_params=pltpu.CompilerParams(dimension_semantics=("arbitrary", "arbitrary")),
        name="pack_w_in",
    )(w_in)


def _cast_kernel(w_ref, o_ref):
    o_ref[...] = w_ref[...].astype(BF16)


def _cast_bf16(w):
    depth, r, c = w.shape
    assert (depth * r) % PREP_ROWS == 0
    out = pl.pallas_call(
        _cast_kernel,
        grid=(depth * r // PREP_ROWS,),
        in_specs=[pl.BlockSpec((PREP_ROWS, c), lambda i: (i, 0))],
        out_specs=pl.BlockSpec((PREP_ROWS, c), lambda i: (i, 0)),
        out_shape=jax.ShapeDtypeStruct((depth * r, c), BF16),
        compiler_params=pltpu.CompilerParams(dimension_semantics=("arbitrary",)),
        name="cast_bf16",
    )(w.reshape(depth * r, c))
    return out.reshape(depth, r, c)


def _prepare_weights(g_mix, w_in, conv_w, conv_b, dt_bias, a_log, d_skip, g_ssd, sinks, w_att_out, w_ssd_out,
                     w_out, g_ffn, w_gate, w_up, w_down):
    pad_h = ((0, 0), (0, LANES - SSD_HEADS))
    w_packed, w_gates = _pack_w_in(w_in)
    return {
        "sinks": sinks,
        "g_mix": g_mix[:, None, :],
        "w_in": w_packed,
        "w_gates": w_gates,
        "conv_w": conv_w,
        "conv_b": conv_b[:, None, :],
        "dt_bias": jnp.pad(dt_bias, pad_h)[:, None, :],
        "a": jnp.pad(-jnp.exp(a_log), pad_h)[:, None, :],
        "d_skip": jnp.repeat(d_skip, SSD_HEAD_DIM, axis=1)[:, None, :],
        "g_ssd": g_ssd[:, None, :],
        "w_att_out": _cast_bf16(w_att_out),
        "w_ssd_out": _cast_bf16(w_ssd_out),
        "w_out": _cast_bf16(w_out),
        "g_ffn": g_ffn[:, None, :],
        "w_gate": _cast_bf16(w_gate),
        "w_up": _cast_bf16(w_up),
        "w_down": _cast_bf16(w_down),
    }


PREP_ROWS = 256
PROMPT_SEQS_PER_TILE = 4
PROMPT_CHUNKS_PER_TILE = 2
SAMPLE_SEQS_PER_TILE = 4
FFN_BLOCK_ROWS = 512


def kernel(x_prompt, x_sample, cache_k, cache_v, state_conv, state_ssm, rel_table, g_mix, w_in, conv_w, conv_b, dt_bias, a_log, d_skip, g_ssd, sinks, w_att_out, w_ssd_out, w_out, g_ffn, w_gate, w_up, w_down, g_final):
    depth = w_in.shape[0]
    bp, sp, _ = x_prompt.shape
    bs, ts, _ = x_sample.shape
    kv_len = cache_k.shape[2]
    assert kv_len == WINDOW and ts <= CHUNK and ts % 8 == 0 and ts >= CONV_W - 1

    bias = _blocked_bias(rel_table)
    g_fin = g_final[None, :]

    wts = _prepare_weights(g_mix, w_in, conv_w, conv_b, dt_bias, a_log, d_skip, g_ssd, sinks, w_att_out,
                           w_ssd_out, w_out, g_ffn, w_gate, w_up, w_down)
    init = (cache_k.reshape(depth, bs, WINDOW, KV_WIDTH), cache_v.reshape(depth, bs, WINDOW, KV_WIDTH),
            state_conv, state_ssm.reshape(depth, bs, SSD_INNER, SSD_STATE))

    xp = x_prompt
    xs = x_sample.reshape(bs * ts, D_MODEL)
    st_p, st_s = [], []
    for l in range(depth):
        final = l == depth - 1
        hp, *state_p = _mixer(xp, bias, wts, l, None, n_seq=PROMPT_SEQS_PER_TILE,
                              n_chunk=PROMPT_CHUNKS_PER_TILE, nvalid=sp)
        xp = _ffn(hp.reshape(bp * sp, D_MODEL), wts, l, g_fin, block_rows=FFN_BLOCK_ROWS,
                  final=final).reshape(bp, sp, D_MODEL)
        st_p.append(state_p)
        xs_pad = jnp.pad(xs.reshape(bs, ts, D_MODEL), ((0, 0), (0, CHUNK - ts), (0, 0)))
        hs, *state_s = _mixer(xs_pad, bias, wts, l, init, n_seq=SAMPLE_SEQS_PER_TILE, n_chunk=1, nvalid=ts)
        xs = _ffn(hs[:, :ts].reshape(bs * ts, D_MODEL), wts, l, g_fin, block_rows=bs * ts, final=final)
        st_s.append(state_s)

    def states(sts, b):
        k, v, conv, ssm = (jnp.stack(leaves) for leaves in zip(*sts))
        return (k.reshape(depth, b, WINDOW, N_KV_HEADS, HEAD_DIM),
                v.reshape(depth, b, WINDOW, N_KV_HEADS, HEAD_DIM), conv,
                ssm.reshape(depth, b, SSD_HEADS, SSD_HEAD_DIM, SSD_STATE))

    return (xp, xs.reshape(x_sample.shape), *states(st_p, bp), *states(st_s, bs))
```

```python
import functools
import math

import jax
import jax.numpy as jnp
from jax import lax
from jax.experimental import pallas as pl
from jax.experimental.pallas import tpu as pltpu

D_MODEL = 1024
CHUNK = 64
EPS = 1e-6
NEG_INF = -1e30
N_HEADS = 8
N_KV_HEADS = 2
HEAD_DIM = 64
ATT_WIDTH = N_HEADS * HEAD_DIM
KV_WIDTH = N_KV_HEADS * HEAD_DIM
WINDOW = 128
N_KEYS = WINDOW + CHUNK
NUM_BUCKETS = 32
MAX_DISTANCE = 128
SSD_INNER = 1024
SSD_HEADS = 16
SSD_HEAD_DIM = 64
SSD_GROUPS = 2
SSD_STATE = 128
GROUP_WIDTH = SSD_INNER // SSD_GROUPS
CONV_W = 4
CONV_DIM = SSD_INNER + 2 * SSD_GROUPS * SSD_STATE
D_FF = 2816
LANES = 128
HIST_ROWS = 8
MXU_WIDTH = 256
XBC_BLOCKS = CONV_DIM // MXU_WIDTH
FILL_POINTS = 7
FILL_FIRST_PHASE = 3

OFF_Q = 0
OFF_KV = OFF_Q + ATT_WIDTH
OFF_XBC = OFF_KV + 2 * KV_WIDTH
OFF_DT = OFF_XBC + CONV_DIM
IN_PACKED = OFF_DT + LANES
GATE_Z = 0
GATE_A = GATE_Z + SSD_INNER
GATE_S = GATE_A + D_MODEL
GATE_WIDTH = GATE_S + D_MODEL
GATE_SLABS = 2

VMEM_LIMIT_BYTES = 60 * 1024 * 1024

F32 = jnp.float32
BF16 = jnp.bfloat16


def _dot(a, b):
    return jnp.dot(a, b, preferred_element_type=F32)


def _dot_nt(a, b):
    return lax.dot_general(a, b, (((1,), (1,)), ((), ())), preferred_element_type=F32)


def _split3(x):
    hi = x.astype(BF16)
    r1 = x - hi.astype(F32)
    mid = r1.astype(BF16)
    lo = (r1 - mid.astype(F32)).astype(BF16)
    return hi, mid, lo


def _exact_dot_right(x, sel, passes=3):
    parts = _split3(x)[:passes]
    out = _dot(parts[0], sel)
    for part in parts[1:]:
        out = out + _dot(part, sel)
    return out


def _exact_dot_left(sel, x):
    hi, mid, lo = _split3(x)
    return _dot(sel, hi) + _dot(sel, mid) + _dot(sel, lo)


def _rmsnorm(x, g):
    return x * lax.rsqrt(jnp.mean(x * x, axis=-1, keepdims=True) + EPS) * g


def _sigmoid(x):
    return 1.0 / (1.0 + jnp.exp(-x))


def _silu(x):
    return x * _sigmoid(x)


def _softplus(x):
    return jnp.maximum(x, 0.0) + jnp.log1p(jnp.exp(-jnp.abs(x)))


def _bias_kernel(table_ref, bucket_ref, o_ref):
    bucket = bucket_ref[...]
    rows = lax.broadcasted_iota(jnp.int32, bucket.shape, 0)
    cols = lax.broadcasted_iota(jnp.int32, bucket.shape, 1)
    row_hi = rows >= CHUNK
    col_hi = (cols & HEAD_DIM) != 0
    for g in range(N_KV_HEADS):
        acc = jnp.zeros(bucket.shape, F32)
        for b in range(NUM_BUCKETS):
            t0 = table_ref[b, 4 * g + 0]
            t1 = table_ref[b, 4 * g + 1]
            t2 = table_ref[b, 4 * g + 2]
            t3 = table_ref[b, 4 * g + 3]
            tv = jnp.where(row_hi, jnp.where(col_hi, t3, t2), jnp.where(col_hi, t1, t0))
            acc = jnp.where(bucket == b, tv, acc)
        o_ref[g] = acc


def _t5_bucket(rel):
    nb = NUM_BUCKETS // 2
    max_exact = nb // 2
    ret = jnp.where(rel > 0, nb, 0)
    n = jnp.abs(rel)
    nf = jnp.maximum(n, 1).astype(jnp.float32)
    large = max_exact + (jnp.log(nf / max_exact) / math.log(MAX_DISTANCE / max_exact)
                         * (nb - max_exact)).astype(jnp.int32)
    large = jnp.minimum(large, nb - 1)
    return ret + jnp.where(n < max_exact, n, large)


def _blocked_bias(rel_table):
    qi = jnp.arange(CHUNK, dtype=jnp.int32)
    kj = jnp.arange(N_KEYS, dtype=jnp.int32)
    bucket = _t5_bucket(kj[None, :] - WINDOW - qi[:, None]).astype(jnp.int32)
    bucket = jnp.tile(bucket.reshape(CHUNK, N_KEYS // CHUNK, 1, CHUNK), (2, 1, 2, 1)).reshape(2 * CHUNK, 2 * N_KEYS)
    return pl.pallas_call(
        _bias_kernel,
        out_shape=jax.ShapeDtypeStruct((N_KV_HEADS, 2 * CHUNK, 2 * N_KEYS), F32),
        in_specs=[pl.BlockSpec(memory_space=pltpu.SMEM),
                  pl.BlockSpec(memory_space=pltpu.VMEM)],
        out_specs=pl.BlockSpec(memory_space=pltpu.VMEM),
        name="rel_bias",
    )(rel_table, bucket)


def _blocked_kv(win, g):
    lo_half = lax.broadcasted_iota(jnp.int32, (N_KEYS, LANES), 1) < HEAD_DIM
    if g == 0:
        a0 = jnp.where(lo_half, win, 0.0)
        a1 = pltpu.roll(a0, HEAD_DIM, axis=1)
    else:
        a1 = jnp.where(lo_half, 0.0, win)
        a0 = pltpu.roll(a1, HEAD_DIM, axis=1)
    pieces = []
    for r in range(0, N_KEYS, CHUNK):
        pieces += [a0[r:r + CHUNK], a1[r:r + CHUNK]]
    return jnp.concatenate(pieces, axis=0).astype(BF16)


def _attention_logits(q, kwin, bias_ref, g):
    qs = jnp.concatenate([q[:, 256 * g:256 * g + LANES],
                          q[:, 256 * g + LANES:256 * (g + 1)]], axis=0).astype(BF16)
    return _dot_nt(qs, _blocked_kv(kwin, g)) + bias_ref[g]


def _attention_finish(logits, vwin, sinks_ref, valid, g):
    row_lo = lax.broadcasted_iota(jnp.int32, (2 * CHUNK, 1), 0) < CHUNK
    even = lax.broadcasted_iota(jnp.int32, (2 * CHUNK, LANES), 1) < HEAD_DIM
    logits = jnp.where(valid, logits, NEG_INF)
    tiles = [logits[:, c:c + LANES] for c in range(0, 2 * N_KEYS, LANES)]
    s0 = jnp.where(row_lo, sinks_ref[4 * g + 0], sinks_ref[4 * g + 2])
    s1 = jnp.where(row_lo, sinks_ref[4 * g + 1], sinks_ref[4 * g + 3])
    tmax = functools.reduce(jnp.maximum, tiles)
    m0 = jnp.maximum(jnp.max(jnp.where(even, tmax, -jnp.inf), axis=1, keepdims=True), s0)
    m1 = jnp.maximum(jnp.max(jnp.where(even, -jnp.inf, tmax), axis=1, keepdims=True), s1)
    m = jnp.where(even, m0, m1)
    p = [jnp.exp(tile - m) for tile in tiles]
    psum = functools.reduce(jnp.add, p)
    d0 = jnp.sum(jnp.where(even, psum, 0.0), axis=1, keepdims=True) + jnp.exp(s0 - m0)
    d1 = jnp.sum(jnp.where(even, 0.0, psum), axis=1, keepdims=True) + jnp.exp(s1 - m1)
    o = _dot(jnp.concatenate(p, axis=1).astype(BF16), _blocked_kv(vwin, g)) / jnp.where(even, d0, d1)
    return jnp.concatenate([o[:CHUNK], o[CHUNK:]], axis=1)


def _ssd_constants(n_blocks):
    r = lax.broadcasted_iota(jnp.int32, (n_blocks * CHUNK, n_blocks * CHUNK), 0)
    c = lax.broadcasted_iota(jnp.int32, (n_blocks * CHUNK, n_blocks * CHUNK), 1)
    tri = jnp.where((c <= r) & ((c >> 6) == (r >> 6)), 1.0, 0.0).astype(BF16)
    hrow = lax.broadcasted_iota(jnp.int32, (LANES, SSD_INNER), 0)
    hcol = lax.broadcasted_iota(jnp.int32, (LANES, SSD_INNER), 1)
    expand = jnp.where(hcol >> 6 == hrow, 1.0, 0.0).astype(BF16)
    return tri, expand


def _ssd_prepare(chunks, dt_raws, dtb_ref, a_ref, nvalid, tri, expand):
    n = len(chunks)
    dt = _softplus(jnp.concatenate(dt_raws, axis=0) + dtb_ref[...])
    if nvalid < CHUNK:
        rows = lax.broadcasted_iota(jnp.int32, dt.shape, 0) & (CHUNK - 1)
        dt = jnp.where(rows < nvalid, dt, 0.0)
    acum3 = _dot(tri, jnp.concatenate(_split3(dt * a_ref[...]), axis=1))
    acum = acum3[:, :LANES] + acum3[:, LANES:2 * LANES] + acum3[:, 2 * LANES:]
    a_hi, a_mid, _ = _split3(acum)
    wide = _dot(jnp.concatenate([dt.astype(BF16), a_hi, a_mid], axis=0), expand)
    m = n * CHUNK
    for i, c in enumerate(chunks):
        sl = slice(i * CHUNK, (i + 1) * CHUNK)
        c["dt_exp"] = wide[sl]
        c["a_col"] = wide[m + i * CHUNK:m + (i + 1) * CHUNK] + wide[2 * m + i * CHUNK:2 * m + (i + 1) * CHUNK]


def _ssd_decay(c):
    a_col = c["a_col"]
    a_last = a_col[CHUNK - 1:CHUNK, :]
    c["xs"] = c["xbc_act"][:, :SSD_INNER]
    c["xdt"] = c["xs"] * c.pop("dt_exp")
    c["xw"] = c["xdt"] * jnp.exp(a_last - a_col)
    c["e_col"] = jnp.exp(a_col)
    c["e_last"] = jnp.exp(a_last)


def _ssd_group(c, gq, ht_ref, g_idx):
    l2 = lax.broadcasted_iota(jnp.int32, (CHUNK, LANES), 0)
    j2 = lax.broadcasted_iota(jnp.int32, (CHUNK, LANES), 1)
    s2 = j2 & (CHUNK - 1)
    diag_sel = jnp.where(s2 == l2, 1.0, 0.0)
    causal2 = s2 <= l2
    lane_lo = j2 < SSD_HEAD_DIM

    b0 = SSD_INNER + SSD_STATE * gq
    c0 = SSD_INNER + SSD_GROUPS * SSD_STATE + SSD_STATE * gq
    bg = c["xbc_act"][:, b0:b0 + SSD_STATE]
    cg = c["xbc_act"][:, c0:c0 + SSD_STATE].astype(BF16)
    bg_bf = bg.astype(BF16)
    cb2 = _dot_nt(cg, jnp.concatenate([bg_bf, bg_bf], axis=0))
    gsl = slice(GROUP_WIDTH * gq, GROUP_WIDTH * (gq + 1))
    h_prev = ht_ref[g_idx, :, gsl]
    y_off = _dot(cg, h_prev.astype(BF16)) * c["e_col"][:, gsl]
    yd = []
    for i in range(GROUP_WIDTH // LANES):
        psl = slice(GROUP_WIDTH * gq + LANES * i, GROUP_WIDTH * gq + LANES * (i + 1))
        ac = c["a_col"][:, psl]
        a_row = jnp.sum(ac * diag_sel, axis=0, keepdims=True)
        lmat = jnp.exp(jnp.where(causal2, ac - a_row, -jnp.inf))
        m2 = (cb2 * lmat).astype(BF16)
        xp = c["xdt"][:, psl]
        xblk = jnp.concatenate([jnp.where(lane_lo, xp, 0.0), jnp.where(lane_lo, 0.0, xp)],
                               axis=0).astype(BF16)
        yd.append(_dot(m2, xblk))
    c.setdefault("ys", []).append(jnp.concatenate(yd, axis=1) + y_off)
    st = _dot(bg.T.astype(BF16), c["xw"][:, gsl].astype(BF16))
    ht_ref[g_idx, :, gsl] = h_prev * c["e_last"][:, gsl] + st


class _LayerSinks:
    def __init__(self, ref, layer):
        self.ref, self.layer = ref, layer

    def __getitem__(self, head):
        return self.ref[self.layer, head]


def _mixer_kernel(*refs, layer, n_seq, n_chunk, n_tiles, nvalid, has_init):
    it = iter(refs)
    x_ref = next(it)
    bias_ref = next(it)
    sinks_ref = _LayerSinks(next(it), layer)
    gmix_ref = next(it)
    win_ref = next(it)
    wgate_ref = next(it)
    convw_ref = next(it)
    convb_ref = next(it)
    dtb_ref = next(it)
    a_ref = next(it)
    dskip_ref = next(it)
    gssd_ref = next(it)
    watt_ref = next(it)
    wssd_ref = next(it)
    wout_ref = next(it)
    if has_init:
        ki_ref = next(it)
        vi_ref = next(it)
        convi_ref = next(it)
        ssmi_ref = next(it)
    h_ref = next(it)
    ko_ref = next(it)
    vo_ref = next(it)
    convo_ref = next(it)
    ssmo_ref = next(it)
    hn_s = next(it)
    q_s = next(it)
    kv_s = next(it)
    xbc_s = next(it)
    xact_s = next(it)
    dt_s = next(it)
    oatt_s = next(it)
    y_s = next(it)
    ht_s = next(it)
    gate_s = next(it)

    t = pl.program_id(1)
    seq_rows = n_chunk * CHUNK

    @pl.when(t == 0)
    def _init():
        if has_init:
            kv_s[:, 0:WINDOW, :KV_WIDTH] = ki_ref[...]
            kv_s[:, 0:WINDOW, KV_WIDTH:] = vi_ref[...]
            xbc_s[:, :, 0:HIST_ROWS, :] = jnp.zeros((XBC_BLOCKS, n_seq, HIST_ROWS, MXU_WIDTH), F32)
            for cb in range(XBC_BLOCKS):
                xbc_s[cb, :, HIST_ROWS - (CONV_W - 1):HIST_ROWS, :] = convi_ref[
                    :, :, cb * MXU_WIDTH:(cb + 1) * MXU_WIDTH]
            for g in range(n_seq):
                ht_s[g] = ssmi_ref[g].T
        else:
            kv_s[:, 0:WINDOW, :] = jnp.zeros((n_seq, WINDOW, 2 * KV_WIDTH), F32)
            xbc_s[:, :, 0:HIST_ROWS, :] = jnp.zeros((XBC_BLOCKS, n_seq, HIST_ROWS, MXU_WIDTH), F32)
            ht_s[...] = jnp.zeros(ht_s.shape, F32)

    rows = n_seq * seq_rows
    hn = _rmsnorm(x_ref[...].reshape(rows, D_MODEL), gmix_ref[...]).astype(BF16)
    hn_s[...] = hn

    def project_block(cb):
        c0 = OFF_XBC + cb * MXU_WIDTH
        blk = _dot(hn, win_ref[:, c0:c0 + MXU_WIDTH])
        for g in range(n_seq):
            xbc_s[cb, g, HIST_ROWS:HIST_ROWS + seq_rows, :] = blk[g * seq_rows:(g + 1) * seq_rows]

    def conv_block(cb):
        csl = slice(cb * MXU_WIDTH, (cb + 1) * MXU_WIDTH)
        for g in range(n_seq):
            conv = convb_ref[:, csl] + convw_ref[CONV_W - 1:CONV_W, csl] * xbc_s[cb, g, HIST_ROWS:HIST_ROWS + seq_rows, :]
            for i in range(1, CONV_W):
                conv = conv + (convw_ref[CONV_W - 1 - i:CONV_W - i, csl]
                               * xbc_s[cb, g, HIST_ROWS - i:HIST_ROWS - i + seq_rows, :])
            xact_s[cb, g * seq_rows:(g + 1) * seq_rows, :] = _silu(conv)

    project_block(0)
    for cb in range(1, XBC_BLOCKS):
        project_block(cb)
        conv_block(cb - 1)
    q_s[...] = _dot(hn, win_ref[:, OFF_Q:OFF_KV]) * (HEAD_DIM ** -0.5)
    conv_block(XBC_BLOCKS - 1)
    kv = _dot(hn, win_ref[:, OFF_KV:OFF_XBC])
    for g in range(n_seq):
        kv_s[g, WINDOW:WINDOW + seq_rows, :] = kv[g * seq_rows:(g + 1) * seq_rows]
    dt_s[...] = _dot(hn, win_ref[:, OFF_DT:IN_PACKED])

    cols = lax.broadcasted_iota(jnp.int32, (2 * CHUNK, 2 * N_KEYS), 1)
    colmod = ((cols >> 7) << 6) + (cols & (CHUNK - 1))
    tri, expand = _ssd_constants(n_seq)

    slab = GATE_WIDTH // GATE_SLABS
    slabs_per_iter = GATE_SLABS // n_chunk

    def iter_body(j, carry):
        k0 = pl.multiple_of(j * CHUNK, CHUNK) if n_chunk > 1 else 0
        per_slab = slab // MXU_WIDTH
        n_pieces = slabs_per_iter * per_slab
        n_points = FILL_POINTS * n_seq
        calls = [0]

        first = FILL_FIRST_PHASE * n_seq
        span = n_points - first

        def fill():
            k = calls[0] - first
            calls[0] += 1
            if k < 0:
                return
            for p in range(-(-k * n_pieces // span), -(-(k + 1) * n_pieces // span)):
                s = j * slabs_per_iter + p // per_slab
                c0 = (p % per_slab) * MXU_WIDTH
                gate_s[s, :, c0:c0 + MXU_WIDTH] = _dot(hn_s[...], wgate_ref[s, :, c0:c0 + MXU_WIDTH])

        if has_init:
            valid = colmod < WINDOW + nvalid
        else:
            first_valid = jnp.maximum(0, (2 - (t * n_chunk + j)) * CHUNK)
            valid = colmod >= first_valid
        chunks = [{"g": g, "rows": pl.ds(pl.multiple_of(g * seq_rows + k0, CHUNK), CHUNK)}
                  for g in range(n_seq)]
        for c in chunks:
            kvwin = kv_s[c["g"], pl.ds(k0, N_KEYS), :]
            c["vwin"] = kvwin[:, KV_WIDTH:]
            q = q_s[c["rows"], :]
            c["logits"] = [_attention_logits(q, kvwin[:, :KV_WIDTH], bias_ref, hg) for hg in range(N_KV_HEADS)]
            fill()
        for c in chunks:
            c["xbc_act"] = jnp.concatenate([xact_s[cb, c["rows"], :] for cb in range(XBC_BLOCKS)], axis=1)
            fill()
        for c in chunks:
            outs = [_attention_finish(c["logits"][hg], c["vwin"], sinks_ref, valid, hg)
                    for hg in range(N_KV_HEADS)]
            oatt_s[c["rows"], :] = jnp.concatenate(outs, axis=1)
            del c["logits"], c["vwin"]
            fill()
        _ssd_prepare(chunks, [dt_s[c["rows"], :] for c in chunks], dtb_ref, a_ref, nvalid, tri, expand)
        for _ in chunks:
            fill()
        for c in chunks:
            _ssd_decay(c)
            fill()
        for gq in range(SSD_GROUPS):
            for c in chunks:
                _ssd_group(c, gq, ht_s, c["g"])
                fill()
        for c in chunks:
            y_s[c["rows"], :] = jnp.concatenate(c["ys"], axis=1) + c["xs"] * dskip_ref[...]
        assert calls[0] == n_points
        return carry

    for j in range(n_chunk):
        iter_body(j, 0)

    def gate_cols(lo, hi):
        pieces = []
        while lo < hi:
            s, off = divmod(lo, slab)
            take = min(hi - lo, slab - off)
            pieces.append(gate_s[s, :, off:off + take])
            lo += take
        return jnp.concatenate(pieces, axis=1)

    y_ssd = _rmsnorm(y_s[...] * _silu(gate_cols(GATE_Z, GATE_A)), gssd_ref[...]).astype(BF16)
    merged = (_sigmoid(gate_cols(GATE_A, GATE_S)) * _dot(oatt_s[...].astype(BF16), watt_ref[...])
              + _sigmoid(gate_cols(GATE_S, GATE_WIDTH)) * _dot(y_ssd, wssd_ref[...]))
    h = x_ref[...].reshape(rows, D_MODEL) + _dot(merged.astype(BF16), wout_ref[...])
    h_ref[...] = h.reshape(n_seq, seq_rows, D_MODEL)

    @pl.when(t == n_tiles - 1)
    def _emit_states():
        ko_ref[...] = kv_s[:, nvalid:nvalid + WINDOW, :KV_WIDTH]
        vo_ref[...] = kv_s[:, nvalid:nvalid + WINDOW, KV_WIDTH:]
        last = HIST_ROWS + nvalid
        convo_ref[...] = jnp.concatenate(
            [xbc_s[cb, :, last - (CONV_W - 1):last, :] for cb in range(XBC_BLOCKS)], axis=-1)
        for g in range(n_seq):
            ssmo_ref[g] = ht_s[g].T

    if n_tiles > 1:
        kv_s[:, 0:WINDOW, :] = kv_s[:, seq_rows:seq_rows + WINDOW, :]
        xbc_s[:, :, 0:HIST_ROWS, :] = xbc_s[:, :, seq_rows:seq_rows + HIST_ROWS, :]


def _layer_spec(shape, layer):
    nd = len(shape)
    return pl.BlockSpec((None,) + tuple(shape), lambda *_, _l=layer, _nd=nd: (_l,) + (0,) * _nd,
                        pipeline_mode=pl.Buffered(1))


def _mixer(x3d, bias, wts, layer, init, *, n_seq, n_chunk, nvalid):
    batch, seq_pad, _ = x3d.shape
    seq_rows = n_chunk * CHUNK
    n_tiles = seq_pad // seq_rows
    assert batch % n_seq == 0 and seq_pad % seq_rows == 0 and GATE_SLABS % n_chunk == 0
    rows = n_seq * seq_rows
    has_init = init is not None
    last_valid = nvalid - (n_tiles - 1) * seq_rows

    tile_map = lambda b, t: (b, t, 0)
    seq_map = lambda b, t: (b, 0, 0)
    cache_map = lambda b, t: (layer, b, 0, 0)
    spec = functools.partial(_layer_spec, layer=layer)
    in_specs = [
        pl.BlockSpec((n_seq, seq_rows, D_MODEL), tile_map),
        pl.BlockSpec(bias.shape, lambda b, t: (0, 0, 0), pipeline_mode=pl.Buffered(1)),
        pl.BlockSpec(memory_space=pltpu.SMEM),
        spec((1, D_MODEL)),
        spec((D_MODEL, IN_PACKED)),
        spec((GATE_SLABS, D_MODEL, GATE_WIDTH // GATE_SLABS)),
        spec((CONV_W, CONV_DIM)),
        spec((1, CONV_DIM)),
        spec((1, LANES)),
        spec((1, LANES)),
        spec((1, SSD_INNER)),
        spec((1, SSD_INNER)),
        spec((ATT_WIDTH, D_MODEL)),
        spec((SSD_INNER, D_MODEL)),
        spec((D_MODEL, D_MODEL)),
    ]
    args = [x3d, bias, wts["sinks"], wts["g_mix"], wts["w_in"], wts["w_gates"], wts["conv_w"], wts["conv_b"],
            wts["dt_bias"], wts["a"], wts["d_skip"], wts["g_ssd"], wts["w_att_out"], wts["w_ssd_out"],
            wts["w_out"]]
    if has_init:
        in_specs += [
            pl.BlockSpec((None, n_seq, WINDOW, KV_WIDTH), cache_map),
            pl.BlockSpec((None, n_seq, WINDOW, KV_WIDTH), cache_map),
            pl.BlockSpec((None, n_seq, CONV_W - 1, CONV_DIM), cache_map),
            pl.BlockSpec((None, n_seq, SSD_INNER, SSD_STATE), cache_map),
        ]
        args += list(init)
    out_shape = [
        jax.ShapeDtypeStruct((batch, seq_pad, D_MODEL), F32),
        jax.ShapeDtypeStruct((batch, WINDOW, KV_WIDTH), F32),
        jax.ShapeDtypeStruct((batch, WINDOW, KV_WIDTH), F32),
        jax.ShapeDtypeStruct((batch, CONV_W - 1, CONV_DIM), F32),
        jax.ShapeDtypeStruct((batch, SSD_INNER, SSD_STATE), F32),
    ]
    out_specs = [
        pl.BlockSpec((n_seq, seq_rows, D_MODEL), tile_map),
        pl.BlockSpec((n_seq, WINDOW, KV_WIDTH), seq_map),
        pl.BlockSpec((n_seq, WINDOW, KV_WIDTH), seq_map),
        pl.BlockSpec((n_seq, CONV_W - 1, CONV_DIM), seq_map),
        pl.BlockSpec((n_seq, SSD_INNER, SSD_STATE), seq_map),
    ]
    scratch = [
        pltpu.VMEM((rows, D_MODEL), BF16),
        pltpu.VMEM((rows, ATT_WIDTH), F32),
        pltpu.VMEM((n_seq, WINDOW + seq_rows, 2 * KV_WIDTH), F32),
        pltpu.VMEM((XBC_BLOCKS, n_seq, HIST_ROWS + seq_rows, MXU_WIDTH), F32),
        pltpu.VMEM((XBC_BLOCKS, rows, MXU_WIDTH), F32),
        pltpu.VMEM((rows, LANES), F32),
        pltpu.VMEM((rows, ATT_WIDTH), F32),
        pltpu.VMEM((rows, SSD_INNER), F32),
        pltpu.VMEM((n_seq, SSD_STATE, SSD_INNER), F32),
        pltpu.VMEM((GATE_SLABS, rows, GATE_WIDTH // GATE_SLABS), F32),
    ]
    kern = functools.partial(_mixer_kernel, layer=layer, n_seq=n_seq, n_chunk=n_chunk, n_tiles=n_tiles,
                             nvalid=last_valid, has_init=has_init)
    return pl.pallas_call(
        kern,
        grid=(batch // n_seq, n_tiles),
        in_specs=in_specs,
        out_specs=out_specs,
        out_shape=out_shape,
        scratch_shapes=scratch,
        compiler_params=pltpu.CompilerParams(
            dimension_semantics=("arbitrary", "arbitrary"), vmem_limit_bytes=VMEM_LIMIT_BYTES),
        name="mixer_init" if has_init else "mixer",
    )(*args)


def _ffn_kernel(h_ref, g_ref, wg_ref, wu_ref, wd_ref, gfin_ref, o_ref, *, final):
    h = h_ref[...]
    hf = _rmsnorm(h, g_ref[...]).astype(BF16)
    act = (_silu(_dot(hf, wg_ref[...])) * _dot(hf, wu_ref[...])).astype(BF16)
    out = h + _dot(act, wd_ref[...])
    if final:
        out = _rmsnorm(out, gfin_ref[...])
    o_ref[...] = out


def _ffn(h2d, wts, layer, g_final, *, block_rows, final):
    n_rows = h2d.shape[0]
    assert n_rows % block_rows == 0
    spec = functools.partial(_layer_spec, layer=layer)
    return pl.pallas_call(
        functools.partial(_ffn_kernel, final=final),
        grid=(n_rows // block_rows,),
        in_specs=[
            pl.BlockSpec((block_rows, D_MODEL), lambda i: (i, 0)),
            spec((1, D_MODEL)),
            spec((D_MODEL, D_FF)),
            spec((D_MODEL, D_FF)),
            spec((D_FF, D_MODEL)),
            pl.BlockSpec((1, D_MODEL), lambda i: (0, 0), pipeline_mode=pl.Buffered(1)),
        ],
        out_specs=pl.BlockSpec((block_rows, D_MODEL), lambda i: (i, 0)),
        out_shape=jax.ShapeDtypeStruct((n_rows, D_MODEL), F32),
        compiler_params=pltpu.CompilerParams(
            dimension_semantics=("arbitrary",), vmem_limit_bytes=VMEM_LIMIT_BYTES),
        name="ffn_final" if final else "ffn",
    )(h2d, wts["g_ffn"], wts["w_gate"], wts["w_up"], wts["w_down"], g_final)


def _pack_w_in_kernel(w_ref, packed_ref, gates_ref):
    o_z = ATT_WIDTH + 2 * KV_WIDTH
    o_xbc = o_z + SSD_INNER
    o_dt = o_xbc + CONV_DIM
    o_ga = o_dt + SSD_HEADS
    o_gs = o_ga + D_MODEL
    packed_ref[:, OFF_Q:OFF_XBC] = w_ref[:, 0:o_z].astype(BF16)
    packed_ref[:, OFF_XBC:OFF_DT] = w_ref[:, o_xbc:o_dt].astype(BF16)
    lane = lax.broadcasted_iota(jnp.int32, (w_ref.shape[0], LANES), 1)
    packed_ref[:, OFF_DT:IN_PACKED] = jnp.where(lane < SSD_HEADS, w_ref[:, o_dt:o_dt + LANES], 0.0).astype(BF16)
    slab = GATE_WIDTH // GATE_SLABS
    for dst, src, width in ((GATE_Z, o_z, SSD_INNER), (GATE_A, o_ga, D_MODEL), (GATE_S, o_gs, D_MODEL)):
        done = 0
        while done < width:
            s, off = divmod(dst + done, slab)
            take = min(width - done, slab - off)
            gates_ref[s, :, off:off + take] = w_ref[:, src + done:src + done + take].astype(BF16)
            done += take


def _pack_w_in(w_in):
    depth, d, width = w_in.shape
    slab = GATE_WIDTH // GATE_SLABS
    return pl.pallas_call(
        _pack_w_in_kernel,
        grid=(depth, d // PREP_ROWS),
        in_specs=[pl.BlockSpec((None, PREP_ROWS, width), lambda l, i: (l, i, 0))],
        out_specs=[pl.BlockSpec((None, PREP_ROWS, IN_PACKED), lambda l, i: (l, i, 0)),
                   pl.BlockSpec((None, GATE_SLABS, PREP_ROWS, slab), lambda l, i: (l, 0, i, 0))],
        out_shape=[jax.ShapeDtypeStruct((depth, d, IN_PACKED), BF16),
                   jax.ShapeDtypeStruct((depth, GATE_SLABS, d, slab), BF16)],
        compiler_params=pltpu.CompilerParams(dimension_semantics=("arbitrary", "arbitrary")),
        name="pack_w_in",
    )(w_in)


def _cast_kernel(w_ref, o_ref):
    o_ref[...] = w_ref[...].astype(BF16)


def _cast_bf16(w):
    depth, r, c = w.shape
    assert (depth * r) % PREP_ROWS == 0
    out = pl.pallas_call(
        _cast_kernel,
        grid=(depth * r // PREP_ROWS,),
        in_specs=[pl.BlockSpec((PREP_ROWS, c), lambda i: (i, 0))],
        out_specs=pl.BlockSpec((PREP_ROWS, c), lambda i: (i, 0)),
        out_shape=jax.ShapeDtypeStruct((depth * r, c), BF16),
        compiler_params=pltpu.CompilerParams(dimension_semantics=("arbitrary",)),
        name="cast_bf16",
    )(w.reshape(depth * r, c))
    return out.reshape(depth, r, c)


def _prepare_weights(g_mix, w_in, conv_w, conv_b, dt_bias, a_log, d_skip, g_ssd, sinks, w_att_out, w_ssd_out,
                     w_out, g_ffn, w_gate, w_up, w_down):
    pad_h = ((0, 0), (0, LANES - SSD_HEADS))
    w_packed, w_gates = _pack_w_in(w_in)
    return {
        "sinks": sinks,
        "g_mix": g_mix[:, None, :],
        "w_in": w_packed,
        "w_gates": w_gates,
        "conv_w": conv_w,
        "conv_b": conv_b[:, None, :],
        "dt_bias": jnp.pad(dt_bias, pad_h)[:, None, :],
        "a": jnp.pad(-jnp.exp(a_log), pad_h)[:, None, :],
        "d_skip": jnp.repeat(d_skip, SSD_HEAD_DIM, axis=1)[:, None, :],
        "g_ssd": g_ssd[:, None, :],
        "w_att_out": _cast_bf16(w_att_out),
        "w_ssd_out": _cast_bf16(w_ssd_out),
        "w_out": _cast_bf16(w_out),
        "g_ffn": g_ffn[:, None, :],
        "w_gate": _cast_bf16(w_gate),
        "w_up": _cast_bf16(w_up),
        "w_down": _cast_bf16(w_down),
    }


PREP_ROWS = 256
PROMPT_SEQS_PER_TILE = 4
PROMPT_CHUNKS_PER_TILE = 2
SAMPLE_SEQS_PER_TILE = 4
FFN_BLOCK_ROWS = 512


def kernel(x_prompt, x_sample, cache_k, cache_v, state_conv, state_ssm, rel_table, g_mix, w_in, conv_w, conv_b, dt_bias, a_log, d_skip, g_ssd, sinks, w_att_out, w_ssd_out, w_out, g_ffn, w_gate, w_up, w_down, g_final):
    depth = w_in.shape[0]
    bp, sp, _ = x_prompt.shape
    bs, ts, _ = x_sample.shape
    kv_len = cache_k.shape[2]
    assert kv_len == WINDOW and ts <= CHUNK and ts % 8 == 0 and ts >= CONV_W - 1

    bias = _blocked_bias(rel_table)
    g_fin = g_final[None, :]

    wts = _prepare_weights(g_mix, w_in, conv_w, conv_b, dt_bias, a_log, d_skip, g_ssd, sinks, w_att_out,
                           w_ssd_out, w_out, g_ffn, w_gate, w_up, w_down)
    init = (cache_k.reshape(depth, bs, WINDOW, KV_WIDTH), cache_v.reshape(depth, bs, WINDOW, KV_WIDTH),
            state_conv, state_ssm.reshape(depth, bs, SSD_INNER, SSD_STATE))

    xp = x_prompt
    xs = x_sample.reshape(bs * ts, D_MODEL)
    st_p, st_s = [], []
    for l in range(depth):
        final = l == depth - 1
        hp, *state_p = _mixer(xp, bias, wts, l, None, n_seq=PROMPT_SEQS_PER_TILE,
                              n_chunk=PROMPT_CHUNKS_PER_TILE, nvalid=sp)
        xp = _ffn(hp.reshape(bp * sp, D_MODEL), wts, l, g_fin, block_rows=FFN_BLOCK_ROWS,
                  final=final).reshape(bp, sp, D_MODEL)
        st_p.append(state_p)
        xs_pad = jnp.pad(xs.reshape(bs, ts, D_MODEL), ((0, 0), (0, CHUNK - ts), (0, 0)))
        hs, *state_s = _mixer(xs_pad, bias, wts, l, init, n_seq=SAMPLE_SEQS_PER_TILE, n_chunk=1, nvalid=ts)
        xs = _ffn(hs[:, :ts].reshape(bs * ts, D_MODEL), wts, l, g_fin, block_rows=bs * ts, final=final)
        st_s.append(state_s)

    def states(sts, b):
        k, v, conv, ssm = (jnp.stack(leaves) for leaves in zip(*sts))
        return (k.reshape(depth, b, WINDOW, N_KV_HEADS, HEAD_DIM),
                v.reshape(depth, b, WINDOW, N_KV_HEADS, HEAD_DIM), conv,
                ssm.reshape(depth, b, SSD_HEADS, SSD_HEAD_DIM, SSD_STATE))

    return (xp, xs.reshape(x_sample.shape), *states(st_p, bp), *states(st_s, bs))
```

```python
import functools
import math

import jax
import jax.numpy as jnp
from jax import lax
from jax.experimental import pallas as pl
from jax.experimental.pallas import tpu as pltpu

D_MODEL = 1024
CHUNK = 64
EPS = 1e-6
NEG_INF = -1e30
N_HEADS = 8
N_KV_HEADS = 2
HEAD_DIM = 64
ATT_WIDTH = N_HEADS * HEAD_DIM
KV_WIDTH = N_KV_HEADS * HEAD_DIM
WINDOW = 128
N_KEYS = WINDOW + CHUNK
NUM_BUCKETS = 32
MAX_DISTANCE = 128
SSD_INNER = 1024
SSD_HEADS = 16
SSD_HEAD_DIM = 64
SSD_GROUPS = 2
SSD_STATE = 128
GROUP_WIDTH = SSD_INNER // SSD_GROUPS
CONV_W = 4
CONV_DIM = SSD_INNER + 2 * SSD_GROUPS * SSD_STATE
D_FF = 2816
LANES = 128
HIST_ROWS = 8
MXU_WIDTH = 256
XBC_BLOCKS = CONV_DIM // MXU_WIDTH
FILL_POINTS = 7
FILL_FIRST_PHASE = 3

OFF_Q = 0
OFF_KV = OFF_Q + ATT_WIDTH
OFF_XBC = OFF_KV + 2 * KV_WIDTH
OFF_DT = OFF_XBC + CONV_DIM
IN_PACKED = OFF_DT + LANES
GATE_Z = 0
GATE_A = GATE_Z + SSD_INNER
GATE_S = GATE_A + D_MODEL
GATE_WIDTH = GATE_S + D_MODEL
GATE_SLABS = 2

VMEM_LIMIT_BYTES = 60 * 1024 * 1024

F32 = jnp.float32
BF16 = jnp.bfloat16


def _dot(a, b):
    return jnp.dot(a, b, preferred_element_type=F32)


def _dot_nt(a, b):
    return lax.dot_general(a, b, (((1,), (1,)), ((), ())), preferred_element_type=F32)


def _split3(x):
    hi = x.astype(BF16)
    r1 = x - hi.astype(F32)
    mid = r1.astype(BF16)
    lo = (r1 - mid.astype(F32)).astype(BF16)
    return hi, mid, lo


def _exact_dot_right(x, sel, passes=3):
    parts = _split3(x)[:passes]
    out = _dot(parts[0], sel)
    for part in parts[1:]:
        out = out + _dot(part, sel)
    return out


def _exact_dot_left(sel, x):
    hi, mid, lo = _split3(x)
    return _dot(sel, hi) + _dot(sel, mid) + _dot(sel, lo)


def _rmsnorm(x, g):
    return x * lax.rsqrt(jnp.mean(x * x, axis=-1, keepdims=True) + EPS) * g


def _sigmoid(x):
    return 1.0 / (1.0 + jnp.exp(-x))


def _silu(x):
    return x * _sigmoid(x)


def _softplus(x):
    return jnp.maximum(x, 0.0) + jnp.log1p(jnp.exp(-jnp.abs(x)))


def _bias_kernel(table_ref, bucket_ref, o_ref):
    bucket = bucket_ref[...]
    rows = lax.broadcasted_iota(jnp.int32, bucket.shape, 0)
    cols = lax.broadcasted_iota(jnp.int32, bucket.shape, 1)
    row_hi = rows >= CHUNK
    col_hi = (cols & HEAD_DIM) != 0
    for g in range(N_KV_HEADS):
        acc = jnp.zeros(bucket.shape, F32)
        for b in range(NUM_BUCKETS):
            t0 = table_ref[b, 4 * g + 0]
            t1 = table_ref[b, 4 * g + 1]
            t2 = table_ref[b, 4 * g + 2]
            t3 = table_ref[b, 4 * g + 3]
            tv = jnp.where(row_hi, jnp.where(col_hi, t3, t2), jnp.where(col_hi, t1, t0))
            acc = jnp.where(bucket == b, tv, acc)
        o_ref[g] = acc


def _t5_bucket(rel):
    nb = NUM_BUCKETS // 2
    max_exact = nb // 2
    ret = jnp.where(rel > 0, nb, 0)
    n = jnp.abs(rel)
    nf = jnp.maximum(n, 1).astype(jnp.float32)
    large = max_exact + (jnp.log(nf / max_exact) / math.log(MAX_DISTANCE / max_exact)
                         * (nb - max_exact)).astype(jnp.int32)
    large = jnp.minimum(large, nb - 1)
    return ret + jnp.where(n < max_exact, n, large)


def _blocked_bias(rel_table):
    qi = jnp.arange(CHUNK, dtype=jnp.int32)
    kj = jnp.arange(N_KEYS, dtype=jnp.int32)
    bucket = _t5_bucket(kj[None, :] - WINDOW - qi[:, None]).astype(jnp.int32)
    bucket = jnp.tile(bucket.reshape(CHUNK, N_KEYS // CHUNK, 1, CHUNK), (2, 1, 2, 1)).reshape(2 * CHUNK, 2 * N_KEYS)
    return pl.pallas_call(
        _bias_kernel,
        out_shape=jax.ShapeDtypeStruct((N_KV_HEADS, 2 * CHUNK, 2 * N_KEYS), F32),
        in_specs=[pl.BlockSpec(memory_space=pltpu.SMEM),
                  pl.BlockSpec(memory_space=pltpu.VMEM)],
        out_specs=pl.BlockSpec(memory_space=pltpu.VMEM),
        name="rel_bias",
    )(rel_table, bucket)


def _blocked_kv(win, g):
    lo_half = lax.broadcasted_iota(jnp.int32, (N_KEYS, LANES), 1) < HEAD_DIM
    if g == 0:
        a0 = jnp.where(lo_half, win, 0.0)
        a1 = pltpu.roll(a0, HEAD_DIM, axis=1)
    else:
        a1 = jnp.where(lo_half, 0.0, win)
        a0 = pltpu.roll(a1, HEAD_DIM, axis=1)
    pieces = []
    for r in range(0, N_KEYS, CHUNK):
        pieces += [a0[r:r + CHUNK], a1[r:r + CHUNK]]
    return jnp.concatenate(pieces, axis=0).astype(BF16)


def _attention_logits(q, kwin, bias_ref, g):
    qs = jnp.concatenate([q[:, 256 * g:256 * g + LANES],
                          q[:, 256 * g + LANES:256 * (g + 1)]], axis=0).astype(BF16)
    return _dot_nt(qs, _blocked_kv(kwin, g)) + bias_ref[g]


def _attention_finish(logits, vwin, sinks_ref, valid, g):
    row_lo = lax.broadcasted_iota(jnp.int32, (2 * CHUNK, 1), 0) < CHUNK
    even = lax.broadcasted_iota(jnp.int32, (2 * CHUNK, LANES), 1) < HEAD_DIM
    logits = jnp.where(valid, logits, NEG_INF)
    tiles = [logits[:, c:c + LANES] for c in range(0, 2 * N_KEYS, LANES)]
    s0 = jnp.where(row_lo, sinks_ref[4 * g + 0], sinks_ref[4 * g + 2])
    s1 = jnp.where(row_lo, sinks_ref[4 * g + 1], sinks_ref[4 * g + 3])
    tmax = functools.reduce(jnp.maximum, tiles)
    m0 = jnp.maximum(jnp.max(jnp.where(even, tmax, -jnp.inf), axis=1, keepdims=True), s0)
    m1 = jnp.maximum(jnp.max(jnp.where(even, -jnp.inf, tmax), axis=1, keepdims=True), s1)
    m = jnp.where(even, m0, m1)
    p = [jnp.exp(tile - m) for tile in tiles]
    psum = functools.reduce(jnp.add, p)
    d0 = jnp.sum(jnp.where(even, psum, 0.0), axis=1, keepdims=True) + jnp.exp(s0 - m0)
    d1 = jnp.sum(jnp.where(even, 0.0, psum), axis=1, keepdims=True) + jnp.exp(s1 - m1)
    o = _dot(jnp.concatenate(p, axis=1).astype(BF16), _blocked_kv(vwin, g)) / jnp.where(even, d0, d1)
    return jnp.concatenate([o[:CHUNK], o[CHUNK:]], axis=1)


def _ssd_constants(n_blocks):
    r = lax.broadcasted_iota(jnp.int32, (n_blocks * CHUNK, n_blocks * CHUNK), 0)
    c = lax.broadcasted_iota(jnp.int32, (n_blocks * CHUNK, n_blocks * CHUNK), 1)
    tri = jnp.where((c <= r) & ((c >> 6) == (r >> 6)), 1.0, 0.0).astype(BF16)
    hrow = lax.broadcasted_iota(jnp.int32, (LANES, SSD_INNER), 0)
    hcol = lax.broadcasted_iota(jnp.int32, (LANES, SSD_INNER), 1)
    expand = jnp.where(hcol >> 6 == hrow, 1.0, 0.0).astype(BF16)
    return tri, expand


def _ssd_prepare(chunks, dt_raws, dtb_ref, a_ref, nvalid, tri, expand):
    n = len(chunks)
    dt = _softplus(jnp.concatenate(dt_raws, axis=0) + dtb_ref[...])
    if nvalid < CHUNK:
        rows = lax.broadcasted_iota(jnp.int32, dt.shape, 0) & (CHUNK - 1)
        dt = jnp.where(rows < nvalid, dt, 0.0)
    acum3 = _dot(tri, jnp.concatenate(_split3(dt * a_ref[...]), axis=1))
    acum = acum3[:, :LANES] + acum3[:, LANES:2 * LANES] + acum3[:, 2 * LANES:]
    a_hi, a_mid, _ = _split3(acum)
    wide = _dot(jnp.concatenate([dt.astype(BF16), a_hi, a_mid], axis=0), expand)
    m = n * CHUNK
    for i, c in enumerate(chunks):
        sl = slice(i * CHUNK, (i + 1) * CHUNK)
        c["dt_exp"] = wide[sl]
        c["a_col"] = wide[m + i * CHUNK:m + (i + 1) * CHUNK] + wide[2 * m + i * CHUNK:2 * m + (i + 1) * CHUNK]


def _ssd_decay(c):
    a_col = c["a_col"]
    a_last = a_col[CHUNK - 1:CHUNK, :]
    c["xs"] = c["xbc_act"][:, :SSD_INNER]
    c["xdt"] = c["xs"] * c.pop("dt_exp")
    c["xw"] = c["xdt"] * jnp.exp(a_last - a_col)
    c["e_col"] = jnp.exp(a_col)
    c["e_last"] = jnp.exp(a_last)


def _ssd_group(c, gq, ht_ref, g_idx):
    l2 = lax.broadcasted_iota(jnp.int32, (CHUNK, LANES), 0)
    j2 = lax.broadcasted_iota(jnp.int32, (CHUNK, LANES), 1)
    s2 = j2 & (CHUNK - 1)
    diag_sel = jnp.where(s2 == l2, 1.0, 0.0)
    causal2 = s2 <= l2
    lane_lo = j2 < SSD_HEAD_DIM

    b0 = SSD_INNER + SSD_STATE * gq
    c0 = SSD_INNER + SSD_GROUPS * SSD_STATE + SSD_STATE * gq
    bg = c["xbc_act"][:, b0:b0 + SSD_STATE]
    cg = c["xbc_act"][:, c0:c0 + SSD_STATE].astype(BF16)
    bg_bf = bg.astype(BF16)
    cb2 = _dot_nt(cg, jnp.concatenate([bg_bf, bg_bf], axis=0))
    gsl = slice(GROUP_WIDTH * gq, GROUP_WIDTH * (gq + 1))
    h_prev = ht_ref[g_idx, :, gsl]
    y_off = _dot(cg, h_prev.astype(BF16)) * c["e_col"][:, gsl]
    yd = []
    for i in range(GROUP_WIDTH // LANES):
        psl = slice(GROUP_WIDTH * gq + LANES * i, GROUP_WIDTH * gq + LANES * (i + 1))
        ac = c["a_col"][:, psl]
        a_row = jnp.sum(ac * diag_sel, axis=0, keepdims=True)
        lmat = jnp.exp(jnp.where(causal2, ac - a_row, -jnp.inf))
        m2 = (cb2 * lmat).astype(BF16)
        xp = c["xdt"][:, psl]
        xblk = jnp.concatenate([jnp.where(lane_lo, xp, 0.0), jnp.where(lane_lo, 0.0, xp)],
                               axis=0).astype(BF16)
        yd.append(_dot(m2, xblk))
    c.setdefault("ys", []).append(jnp.concatenate(yd, axis=1) + y_off)
    st = _dot(bg.T.astype(BF16), c["xw"][:, gsl].astype(BF16))
    ht_ref[g_idx, :, gsl] = h_prev * c["e_last"][:, gsl] + st


class _LayerSinks:
    def __init__(self, ref, layer):
        self.ref, self.layer = ref, layer

    def __getitem__(self, head):
        return self.ref[self.layer, head]


def _mixer_kernel(*refs, layer, n_seq, n_chunk, n_tiles, nvalid, has_init):
    it = iter(refs)
    x_ref = next(it)
    bias_ref = next(it)
    sinks_ref = _LayerSinks(next(it), layer)
    gmix_ref = next(it)
    win_ref = next(it)
    wgate_ref = next(it)
    convw_ref = next(it)
    convb_ref = next(it)
    dtb_ref = next(it)
    a_ref = next(it)
    dskip_ref = next(it)
    gssd_ref = next(it)
    watt_ref = next(it)
    wssd_ref = next(it)
    wout_ref = next(it)
    if has_init:
        ki_ref = next(it)
        vi_ref = next(it)
        convi_ref = next(it)
        ssmi_ref = next(it)
    h_ref = next(it)
    ko_ref = next(it)
    vo_ref = next(it)
    convo_ref = next(it)
    ssmo_ref = next(it)
    hn_s = next(it)
    q_s = next(it)
    kv_s = next(it)
    xbc_s = next(it)
    xact_s = next(it)
    dt_s = next(it)
    oatt_s = next(it)
    y_s = next(it)
    ht_s = next(it)
    gate_s = next(it)

    t = pl.program_id(1)
    seq_rows = n_chunk * CHUNK

    @pl.when(t == 0)
    def _init():
        if has_init:
            kv_s[:, 0:WINDOW, :KV_WIDTH] = ki_ref[...]
            kv_s[:, 0:WINDOW, KV_WIDTH:] = vi_ref[...]
            xbc_s[:, :, 0:HIST_ROWS, :] = jnp.zeros((XBC_BLOCKS, n_seq, HIST_ROWS, MXU_WIDTH), F32)
            for cb in range(XBC_BLOCKS):
                xbc_s[cb, :, HIST_ROWS - (CONV_W - 1):HIST_ROWS, :] = convi_ref[
                    :, :, cb * MXU_WIDTH:(cb + 1) * MXU_WIDTH]
            for g in range(n_seq):
                ht_s[g] = ssmi_ref[g].T
        else:
            kv_s[:, 0:WINDOW, :] = jnp.zeros((n_seq, WINDOW, 2 * KV_WIDTH), F32)
            xbc_s[:, :, 0:HIST_ROWS, :] = jnp.zeros((XBC_BLOCKS, n_seq, HIST_ROWS, MXU_WIDTH), F32)
            ht_s[...] = jnp.zeros(ht_s.shape, F32)

    rows = n_seq * seq_rows
    hn = _rmsnorm(x_ref[...].reshape(rows, D_MODEL), gmix_ref[...]).astype(BF16)
    hn_s[...] = hn

    def project_block(cb):
        c0 = OFF_XBC + cb * MXU_WIDTH
        blk = _dot(hn, win_ref[:, c0:c0 + MXU_WIDTH])
        for g in range(n_seq):
            xbc_s[cb, g, HIST_ROWS:HIST_ROWS + seq_rows, :] = blk[g * seq_rows:(g + 1) * seq_rows]

    def conv_block(cb):
        csl = slice(cb * MXU_WIDTH, (cb + 1) * MXU_WIDTH)
        for g in range(n_seq):
            conv = convb_ref[:, csl] + convw_ref[CONV_W - 1:CONV_W, csl] * xbc_s[cb, g, HIST_ROWS:HIST_ROWS + seq_rows, :]
            for i in range(1, CONV_W):
                conv = conv + (convw_ref[CONV_W - 1 - i:CONV_W - i, csl]
                               * xbc_s[cb, g, HIST_ROWS - i:HIST_ROWS - i + seq_rows, :])
            xact_s[cb, g * seq_rows:(g + 1) * seq_rows, :] = _silu(conv)

    project_block(0)
    for cb in range(1, XBC_BLOCKS):
        project_block(cb)
        conv_block(cb - 1)
    q_s[...] = _dot(hn, win_ref[:, OFF_Q:OFF_KV]) * (HEAD_DIM ** -0.5)
    conv_block(XBC_BLOCKS - 1)
    kv = _dot(hn, win_ref[:, OFF_KV:OFF_XBC])
    for g in range(n_seq):
        kv_s[g, WINDOW:WINDOW + seq_rows, :] = kv[g * seq_rows:(g + 1) * seq_rows]
    dt_s[...] = _dot(hn, win_ref[:, OFF_DT:IN_PACKED])

    cols = lax.broadcasted_iota(jnp.int32, (2 * CHUNK, 2 * N_KEYS), 1)
    colmod = ((cols >> 7) << 6) + (cols & (CHUNK - 1))
    tri, expand = _ssd_constants(n_seq)

    slab = GATE_WIDTH // GATE_SLABS
    slabs_per_iter = GATE_SLABS // n_chunk

    def iter_body(j, carry):
        k0 = pl.multiple_of(j * CHUNK, CHUNK) if n_chunk > 1 else 0
        per_slab = slab // MXU_WIDTH
        n_pieces = slabs_per_iter * per_slab
        n_points = FILL_POINTS * n_seq
        calls = [0]

        first = FILL_FIRST_PHASE * n_seq
        span = n_points - first

        def fill():
            k = calls[0] - first
            calls[0] += 1
            if k < 0:
                return
            for p in range(-(-k * n_pieces // span), -(-(k + 1) * n_pieces // span)):
                s = j * slabs_per_iter + p // per_slab
                c0 = (p % per_slab) * MXU_WIDTH
                gate_s[s, :, c0:c0 + MXU_WIDTH] = _dot(hn_s[...], wgate_ref[s, :, c0:c0 + MXU_WIDTH])

        if has_init:
            valid = colmod < WINDOW + nvalid
        else:
            first_valid = jnp.maximum(0, (2 - (t * n_chunk + j)) * CHUNK)
            valid = colmod >= first_valid
        chunks = [{"g": g, "rows": pl.ds(pl.multiple_of(g * seq_rows + k0, CHUNK), CHUNK)}
                  for g in range(n_seq)]
        for c in chunks:
            kvwin = kv_s[c["g"], pl.ds(k0, N_KEYS), :]
            c["vwin"] = kvwin[:, KV_WIDTH:]
            q = q_s[c["rows"], :]
            c["logits"] = [_attention_logits(q, kvwin[:, :KV_WIDTH], bias_ref, hg) for hg in range(N_KV_HEADS)]
            fill()
        for c in chunks:
            c["xbc_act"] = jnp.concatenate([xact_s[cb, c["rows"], :] for cb in range(XBC_BLOCKS)], axis=1)
            fill()
        for c in chunks:
            outs = [_attention_finish(c["logits"][hg], c["vwin"], sinks_ref, valid, hg)
                    for hg in range(N_KV_HEADS)]
            oatt_s[c["rows"], :] = jnp.concatenate(outs, axis=1)
            del c["logits"], c["vwin"]
            fill()
        _ssd_prepare(chunks, [dt_s[c["rows"], :] for c in chunks], dtb_ref, a_ref, nvalid, tri, expand)
        for _ in chunks:
            fill()
        for c in chunks:
            _ssd_decay(c)
            fill()
        for gq in range(SSD_GROUPS):
            for c in chunks:
                _ssd_group(c, gq, ht_s, c["g"])
                fill()
        for c in chunks:
            y_s[c["rows"], :] = jnp.concatenate(c["ys"], axis=1) + c["xs"] * dskip_ref[...]
        assert calls[0] == n_points
        return carry

    for j in range(n_chunk):
        iter_body(j, 0)

    def gate_cols(lo, hi):
        pieces = []
        while lo < hi:
            s, off = divmod(lo, slab)
            take = min(hi - lo, slab - off)
            pieces.append(gate_s[s, :, off:off + take])
            lo += take
        return jnp.concatenate(pieces, axis=1)

    y_ssd = _rmsnorm(y_s[...] * _silu(gate_cols(GATE_Z, GATE_A)), gssd_ref[...]).astype(BF16)
    merged = (_sigmoid(gate_cols(GATE_A, GATE_S)) * _dot(oatt_s[...].astype(BF16), watt_ref[...])
              + _sigmoid(gate_cols(GATE_S, GATE_WIDTH)) * _dot(y_ssd, wssd_ref[...]))
    h = x_ref[...].reshape(rows, D_MODEL) + _dot(merged.astype(BF16), wout_ref[...])
    h_ref[...] = h.reshape(n_seq, seq_rows, D_MODEL)

    @pl.when(t == n_tiles - 1)
    def _emit_states():
        ko_ref[...] = kv_s[:, nvalid:nvalid + WINDOW, :KV_WIDTH]
        vo_ref[...] = kv_s[:, nvalid:nvalid + WINDOW, KV_WIDTH:]
        last = HIST_ROWS + nvalid
        convo_ref[...] = jnp.concatenate(
            [xbc_s[cb, :, last - (CONV_W - 1):last, :] for cb in range(XBC_BLOCKS)], axis=-1)
        for g in range(n_seq):
            ssmo_ref[g] = ht_s[g].T

    if n_tiles > 1:
        kv_s[:, 0:WINDOW, :] = kv_s[:, seq_rows:seq_rows + WINDOW, :]
        xbc_s[:, :, 0:HIST_ROWS, :] = xbc_s[:, :, seq_rows:seq_rows + HIST_ROWS, :]


def _layer_spec(shape, layer):
    nd = len(shape)
    return pl.BlockSpec((None,) + tuple(shape), lambda *_, _l=layer, _nd=nd: (_l,) + (0,) * _nd,
                        pipeline_mode=pl.Buffered(1))


def _mixer(x3d, bias, wts, layer, init, *, n_seq, n_chunk, nvalid):
    batch, seq_pad, _ = x3d.shape
    seq_rows = n_chunk * CHUNK
    n_tiles = seq_pad // seq_rows
    assert batch % n_seq == 0 and seq_pad % seq_rows == 0 and GATE_SLABS % n_chunk == 0
    rows = n_seq * seq_rows
    has_init = init is not None
    last_valid = nvalid - (n_tiles - 1) * seq_rows

    tile_map = lambda b, t: (b, t, 0)
    seq_map = lambda b, t: (b, 0, 0)
    cache_map = lambda b, t: (layer, b, 0, 0)
    spec = functools.partial(_layer_spec, layer=layer)
    in_specs = [
        pl.BlockSpec((n_seq, seq_rows, D_MODEL), tile_map),
        pl.BlockSpec(bias.shape, lambda b, t: (0, 0, 0), pipeline_mode=pl.Buffered(1)),
        pl.BlockSpec(memory_space=pltpu.SMEM),
        spec((1, D_MODEL)),
        spec((D_MODEL, IN_PACKED)),
        spec((GATE_SLABS, D_MODEL, GATE_WIDTH // GATE_SLABS)),
        spec((CONV_W, CONV_DIM)),
        spec((1, CONV_DIM)),
        spec((1, LANES)),
        spec((1, LANES)),
        spec((1, SSD_INNER)),
        spec((1, SSD_INNER)),
        spec((ATT_WIDTH, D_MODEL)),
        spec((SSD_INNER, D_MODEL)),
        spec((D_MODEL, D_MODEL)),
    ]
    args = [x3d, bias, wts["sinks"], wts["g_mix"], wts["w_in"], wts["w_gates"], wts["conv_w"], wts["conv_b"],
            wts["dt_bias"], wts["a"], wts["d_skip"], wts["g_ssd"], wts["w_att_out"], wts["w_ssd_out"],
            wts["w_out"]]
    if has_init:
        in_specs += [
            pl.BlockSpec((None, n_seq, WINDOW, KV_WIDTH), cache_map),
            pl.BlockSpec((None, n_seq, WINDOW, KV_WIDTH), cache_map),
            pl.BlockSpec((None, n_seq, CONV_W - 1, CONV_DIM), cache_map),
            pl.BlockSpec((None, n_seq, SSD_INNER, SSD_STATE), cache_map),
        ]
        args += list(init)
    out_shape = [
        jax.ShapeDtypeStruct((batch, seq_pad, D_MODEL), F32),
        jax.ShapeDtypeStruct((batch, WINDOW, KV_WIDTH), F32),
        jax.ShapeDtypeStruct((batch, WINDOW, KV_WIDTH), F32),
        jax.ShapeDtypeStruct((batch, CONV_W - 1, CONV_DIM), F32),
        jax.ShapeDtypeStruct((batch, SSD_INNER, SSD_STATE), F32),
    ]
    out_specs = [
        pl.BlockSpec((n_seq, seq_rows, D_MODEL), tile_map),
        pl.BlockSpec((n_seq, WINDOW, KV_WIDTH), seq_map),
        pl.BlockSpec((n_seq, WINDOW, KV_WIDTH), seq_map),
        pl.BlockSpec((n_seq, CONV_W - 1, CONV_DIM), seq_map),
        pl.BlockSpec((n_seq, SSD_INNER, SSD_STATE), seq_map),
    ]
    scratch = [
        pltpu.VMEM((rows, D_MODEL), BF16),
        pltpu.VMEM((rows, ATT_WIDTH), F32),
        pltpu.VMEM((n_seq, WINDOW + seq_rows, 2 * KV_WIDTH), F32),
        pltpu.VMEM((XBC_BLOCKS, n_seq, HIST_ROWS + seq_rows, MXU_WIDTH), F32),
        pltpu.VMEM((XBC_BLOCKS, rows, MXU_WIDTH), F32),
        pltpu.VMEM((rows, LANES), F32),
        pltpu.VMEM((rows, ATT_WIDTH), F32),
        pltpu.VMEM((rows, SSD_INNER), F32),
        pltpu.VMEM((n_seq, SSD_STATE, SSD_INNER), F32),
        pltpu.VMEM((GATE_SLABS, rows, GATE_WIDTH // GATE_SLABS), F32),
    ]
    kern = functools.partial(_mixer_kernel, layer=layer, n_seq=n_seq, n_chunk=n_chunk, n_tiles=n_tiles,
                             nvalid=last_valid, has_init=has_init)
    return pl.pallas_call(
        kern,
        grid=(batch // n_seq, n_tiles),
        in_specs=in_specs,
        out_specs=out_specs,
        out_shape=out_shape,
        scratch_shapes=scratch,
        compiler_params=pltpu.CompilerParams(
            dimension_semantics=("arbitrary", "arbitrary"), vmem_limit_bytes=VMEM_LIMIT_BYTES),
        name="mixer_init" if has_init else "mixer",
    )(*args)


def _ffn_kernel(h_ref, g_ref, wg_ref, wu_ref, wd_ref, gfin_ref, o_ref, *, final):
    h = h_ref[...]
    hf = _rmsnorm(h, g_ref[...]).astype(BF16)
    act = (_silu(_dot(hf, wg_ref[...])) * _dot(hf, wu_ref[...])).astype(BF16)
    out = h + _dot(act, wd_ref[...])
    if final:
        out = _rmsnorm(out, gfin_ref[...])
    o_ref[...] = out


def _ffn(h2d, wts, layer, g_final, *, block_rows, final):
    n_rows = h2d.shape[0]
    assert n_rows % block_rows == 0
    spec = functools.partial(_layer_spec, layer=layer)
    return pl.pallas_call(
        functools.partial(_ffn_kernel, final=final),
        grid=(n_rows // block_rows,),
        in_specs=[
            pl.BlockSpec((block_rows, D_MODEL), lambda i: (i, 0)),
            spec((1, D_MODEL)),
            spec((D_MODEL, D_FF)),
            spec((D_MODEL, D_FF)),
            spec((D_FF, D_MODEL)),
            pl.BlockSpec((1, D_MODEL), lambda i: (0, 0), pipeline_mode=pl.Buffered(1)),
        ],
        out_specs=pl.BlockSpec((block_rows, D_MODEL), lambda i: (i, 0)),
        out_shape=jax.ShapeDtypeStruct((n_rows, D_MODEL), F32),
        compiler_params=pltpu.CompilerParams(
            dimension_semantics=("arbitrary",), vmem_limit_bytes=VMEM_LIMIT_BYTES),
        name="ffn_final" if final else "ffn",
    )(h2d, wts["g_ffn"], wts["w_gate"], wts["w_up"], wts["w_down"], g_final)


def _pack_w_in_kernel(w_ref, packed_ref, gates_ref):
    o_z = ATT_WIDTH + 2 * KV_WIDTH
    o_xbc = o_z + SSD_INNER
    o_dt = o_xbc + CONV_DIM
    o_ga = o_dt + SSD_HEADS
    o_gs = o_ga + D_MODEL
    packed_ref[:, OFF_Q:OFF_XBC] = w_ref[:, 0:o_z].astype(BF16)
    packed_ref[:, OFF_XBC:OFF_DT] = w_ref[:, o_xbc:o_dt].astype(BF16)
    lane = lax.broadcasted_iota(jnp.int32, (w_ref.shape[0], LANES), 1)
    packed_ref[:, OFF_DT:IN_PACKED] = jnp.where(lane < SSD_HEADS, w_ref[:, o_dt:o_dt + LANES], 0.0).astype(BF16)
    slab = GATE_WIDTH // GATE_SLABS
    for dst, src, width in ((GATE_Z, o_z, SSD_INNER), (GATE_A, o_ga, D_MODEL), (GATE_S, o_gs, D_MODEL)):
        done = 0
        while done < width:
            s, off = divmod(dst + done, slab)
            take = min(width - done, slab - off)
            gates_ref[s, :, off:off + take] = w_ref[:, src + done:src + done + take].astype(BF16)
            done += take


def _pack_w_in(w_in):
    depth, d, width = w_in.shape
    slab = GATE_WIDTH // GATE_SLABS
    return pl.pallas_call(
        _pack_w_in_kernel,
        grid=(depth, d // PREP_ROWS),
        in_specs=[pl.BlockSpec((None, PREP_ROWS, width), lambda l, i: (l, i, 0))],
        out_specs=[pl.BlockSpec((None, PREP_ROWS, IN_PACKED), lambda l, i: (l, i, 0)),
                   pl.BlockSpec((None, GATE_SLABS, PREP_ROWS, slab), lambda l, i: (l, 0, i, 0))],
        out_shape=[jax.ShapeDtypeStruct((depth, d, IN_PACKED), BF16),
                   jax.ShapeDtypeStruct((depth, GATE_SLABS, d, slab), BF16)],
        compiler_params=pltpu.CompilerParams(dimension_semantics=("arbitrary", "arbitrary")),
        name="pack_w_in",
    )(w_in)


def _cast_kernel(w_ref, o_ref):
    o_ref[...] = w_ref[...].astype(BF16)


def _cast_bf16(w):
    depth, r, c = w.shape
    n_rows = depth * r
    block_rows = PREP_ROWS
    while (n_rows % (2 * block_rows) == 0 and n_rows // (2 * block_rows) >= CAST_MIN_STEPS
           and 2 * block_rows * c * 4 <= CAST_BLOCK_BYTES):
        block_rows *= 2
    assert n_rows % block_rows == 0
    out = pl.pallas_call(
        _cast_kernel,
        grid=(n_rows // block_rows,),
        in_specs=[pl.BlockSpec((block_rows, c), lambda i: (i, 0))],
        out_specs=pl.BlockSpec((block_rows, c), lambda i: (i, 0)),
        out_shape=jax.ShapeDtypeStruct((n_rows, c), BF16),
        compiler_params=pltpu.CompilerParams(dimension_semantics=("arbitrary",),
                                             vmem_limit_bytes=VMEM_LIMIT_BYTES),
        name="cast_bf16",
    )(w.reshape(n_rows, c))
    return out.reshape(depth, r, c)


def _prepare_weights(g_mix, w_in, conv_w, conv_b, dt_bias, a_log, d_skip, g_ssd, sinks, w_att_out, w_ssd_out,
                     w_out, g_ffn, w_gate, w_up, w_down):
    pad_h = ((0, 0), (0, LANES - SSD_HEADS))
    w_packed, w_gates = _pack_w_in(w_in)
    return {
        "sinks": sinks,
        "g_mix": g_mix[:, None, :],
        "w_in": w_packed,
        "w_gates": w_gates,
        "conv_w": conv_w,
        "conv_b": conv_b[:, None, :],
        "dt_bias": jnp.pad(dt_bias, pad_h)[:, None, :],
        "a": jnp.pad(-jnp.exp(a_log), pad_h)[:, None, :],
        "d_skip": jnp.repeat(d_skip, SSD_HEAD_DIM, axis=1)[:, None, :],
        "g_ssd": g_ssd[:, None, :],
        "w_att_out": _cast_bf16(w_att_out),
        "w_ssd_out": _cast_bf16(w_ssd_out),
        "w_out": _cast_bf16(w_out),
        "g_ffn": g_ffn[:, None, :],
        "w_gate": _cast_bf16(w_gate),
        "w_up": _cast_bf16(w_up),
        "w_down": _cast_bf16(w_down),
    }


PREP_ROWS = 256
CAST_BLOCK_BYTES = 8 * 1024 * 1024
CAST_MIN_STEPS = 4
PROMPT_SEQS_PER_TILE = 4
PROMPT_CHUNKS_PER_TILE = 2
SAMPLE_SEQS_PER_TILE = 4
FFN_BLOCK_ROWS = 512


def kernel(x_prompt, x_sample, cache_k, cache_v, state_conv, state_ssm, rel_table, g_mix, w_in, conv_w, conv_b, dt_bias, a_log, d_skip, g_ssd, sinks, w_att_out, w_ssd_out, w_out, g_ffn, w_gate, w_up, w_down, g_final):
    depth = w_in.shape[0]
    bp, sp, _ = x_prompt.shape
    bs, ts, _ = x_sample.shape
    kv_len = cache_k.shape[2]
    assert kv_len == WINDOW and ts <= CHUNK and ts % 8 == 0 and ts >= CONV_W - 1

    bias = _blocked_bias(rel_table)
    g_fin = g_final[None, :]

    wts = _prepare_weights(g_mix, w_in, conv_w, conv_b, dt_bias, a_log, d_skip, g_ssd, sinks, w_att_out,
                           w_ssd_out, w_out, g_ffn, w_gate, w_up, w_down)
    init = (cache_k.reshape(depth, bs, WINDOW, KV_WIDTH), cache_v.reshape(depth, bs, WINDOW, KV_WIDTH),
            state_conv, state_ssm.reshape(depth, bs, SSD_INNER, SSD_STATE))

    xp = x_prompt
    xs = x_sample.reshape(bs * ts, D_MODEL)
    st_p, st_s = [], []
    for l in range(depth):
        final = l == depth - 1
        hp, *state_p = _mixer(xp, bias, wts, l, None, n_seq=PROMPT_SEQS_PER_TILE,
                              n_chunk=PROMPT_CHUNKS_PER_TILE, nvalid=sp)
        xp = _ffn(hp.reshape(bp * sp, D_MODEL), wts, l, g_fin, block_rows=FFN_BLOCK_ROWS,
                  final=final).reshape(bp, sp, D_MODEL)
        st_p.append(state_p)
        xs_pad = jnp.pad(xs.reshape(bs, ts, D_MODEL), ((0, 0), (0, CHUNK - ts), (0, 0)))
        hs, *state_s = _mixer(xs_pad, bias, wts, l, init, n_seq=SAMPLE_SEQS_PER_TILE, n_chunk=1, nvalid=ts)
        xs = _ffn(hs[:, :ts].reshape(bs * ts, D_MODEL), wts, l, g_fin, block_rows=bs * ts, final=final)
        st_s.append(state_s)

    def states(sts, b):
        k, v, conv, ssm = (jnp.stack(leaves) for leaves in zip(*sts))
        return (k.reshape(depth, b, WINDOW, N_KV_HEADS, HEAD_DIM),
                v.reshape(depth, b, WINDOW, N_KV_HEADS, HEAD_DIM), conv,
                ssm.reshape(depth, b, SSD_HEADS, SSD_HEAD_DIM, SSD_STATE))

    return (xp, xs.reshape(x_sample.shape), *states(st_p, bp), *states(st_s, bs))
```

```python
import functools
import math

import jax
import jax.numpy as jnp
from jax import lax
from jax.experimental import pallas as pl
from jax.experimental.pallas import tpu as pltpu

D_MODEL = 1024
CHUNK = 64
EPS = 1e-6
NEG_INF = -1e30
N_HEADS = 8
N_KV_HEADS = 2
HEAD_DIM = 64
ATT_WIDTH = N_HEADS * HEAD_DIM
KV_WIDTH = N_KV_HEADS * HEAD_DIM
WINDOW = 128
N_KEYS = WINDOW + CHUNK
NUM_BUCKETS = 32
MAX_DISTANCE = 128
SSD_INNER = 1024
SSD_HEADS = 16
SSD_HEAD_DIM = 64
SSD_GROUPS = 2
SSD_STATE = 128
GROUP_WIDTH = SSD_INNER // SSD_GROUPS
CONV_W = 4
CONV_DIM = SSD_INNER + 2 * SSD_GROUPS * SSD_STATE
D_FF = 2816
LANES = 128
SUBLANES = 8
HIST_ROWS = 8
MXU_WIDTH = 256
XBC_BLOCKS = CONV_DIM // MXU_WIDTH
FILL_POINTS = 7
FILL_FIRST_PHASE = 3

OFF_Q = 0
OFF_KV = OFF_Q + ATT_WIDTH
OFF_XBC = OFF_KV + 2 * KV_WIDTH
OFF_DT = OFF_XBC + CONV_DIM
IN_PACKED = OFF_DT + LANES
GATE_Z = 0
GATE_A = GATE_Z + SSD_INNER
GATE_S = GATE_A + D_MODEL
GATE_WIDTH = GATE_S + D_MODEL
GATE_SLABS = 2

VMEM_LIMIT_BYTES = 60 * 1024 * 1024

F32 = jnp.float32
BF16 = jnp.bfloat16


def _dot(a, b):
    return jnp.dot(a, b, preferred_element_type=F32)


def _dot_nt(a, b):
    return lax.dot_general(a, b, (((1,), (1,)), ((), ())), preferred_element_type=F32)


def _split3(x):
    hi = x.astype(BF16)
    r1 = x - hi.astype(F32)
    mid = r1.astype(BF16)
    lo = (r1 - mid.astype(F32)).astype(BF16)
    return hi, mid, lo


def _exact_dot_right(x, sel, passes=3):
    parts = _split3(x)[:passes]
    out = _dot(parts[0], sel)
    for part in parts[1:]:
        out = out + _dot(part, sel)
    return out


def _exact_dot_left(sel, x):
    hi, mid, lo = _split3(x)
    return _dot(sel, hi) + _dot(sel, mid) + _dot(sel, lo)


def _rmsnorm(x, g):
    return x * lax.rsqrt(jnp.mean(x * x, axis=-1, keepdims=True) + EPS) * g


def _sigmoid(x):
    return 1.0 / (1.0 + jnp.exp(-x))


def _silu(x):
    return x * _sigmoid(x)


def _softplus(x):
    return jnp.maximum(x, 0.0) + jnp.log1p(jnp.exp(-jnp.abs(x)))


def _bias_kernel(table_ref, bucket_ref, o_ref):
    bucket = bucket_ref[...]
    rows = lax.broadcasted_iota(jnp.int32, bucket.shape, 0)
    cols = lax.broadcasted_iota(jnp.int32, bucket.shape, 1)
    row_hi = rows >= CHUNK
    col_hi = (cols & HEAD_DIM) != 0
    for g in range(N_KV_HEADS):
        acc = jnp.zeros(bucket.shape, F32)
        for b in range(NUM_BUCKETS):
            t0 = table_ref[b, 4 * g + 0]
            t1 = table_ref[b, 4 * g + 1]
            t2 = table_ref[b, 4 * g + 2]
            t3 = table_ref[b, 4 * g + 3]
            tv = jnp.where(row_hi, jnp.where(col_hi, t3, t2), jnp.where(col_hi, t1, t0))
            acc = jnp.where(bucket == b, tv, acc)
        o_ref[g] = acc


def _t5_bucket(rel):
    nb = NUM_BUCKETS // 2
    max_exact = nb // 2
    ret = jnp.where(rel > 0, nb, 0)
    n = jnp.abs(rel)
    nf = jnp.maximum(n, 1).astype(jnp.float32)
    large = max_exact + (jnp.log(nf / max_exact) / math.log(MAX_DISTANCE / max_exact)
                         * (nb - max_exact)).astype(jnp.int32)
    large = jnp.minimum(large, nb - 1)
    return ret + jnp.where(n < max_exact, n, large)


def _blocked_bias(rel_table):
    qi = jnp.arange(CHUNK, dtype=jnp.int32)
    kj = jnp.arange(N_KEYS, dtype=jnp.int32)
    bucket = _t5_bucket(kj[None, :] - WINDOW - qi[:, None]).astype(jnp.int32)
    bucket = jnp.tile(bucket.reshape(CHUNK, N_KEYS // CHUNK, 1, CHUNK), (2, 1, 2, 1)).reshape(2 * CHUNK, 2 * N_KEYS)
    return pl.pallas_call(
        _bias_kernel,
        out_shape=jax.ShapeDtypeStruct((N_KV_HEADS, 2 * CHUNK, 2 * N_KEYS), F32),
        in_specs=[pl.BlockSpec(memory_space=pltpu.SMEM),
                  pl.BlockSpec(memory_space=pltpu.VMEM)],
        out_specs=pl.BlockSpec(memory_space=pltpu.VMEM),
        name="rel_bias",
    )(rel_table, bucket)


def _blocked_kv(win, g):
    lo_half = lax.broadcasted_iota(jnp.int32, (N_KEYS, LANES), 1) < HEAD_DIM
    if g == 0:
        a0 = jnp.where(lo_half, win, 0.0)
        a1 = pltpu.roll(a0, HEAD_DIM, axis=1)
    else:
        a1 = jnp.where(lo_half, 0.0, win)
        a0 = pltpu.roll(a1, HEAD_DIM, axis=1)
    pieces = []
    for r in range(0, N_KEYS, CHUNK):
        pieces += [a0[r:r + CHUNK], a1[r:r + CHUNK]]
    return jnp.concatenate(pieces, axis=0).astype(BF16)


def _attention_logits(q, kwin, bias_ref, g):
    qs = jnp.concatenate([q[:, 256 * g:256 * g + LANES],
                          q[:, 256 * g + LANES:256 * (g + 1)]], axis=0).astype(BF16)
    return _dot_nt(qs, _blocked_kv(kwin, g)) + bias_ref[g]


def _attention_finish(logits, vwin, sinks_ref, valid, g):
    row_lo = lax.broadcasted_iota(jnp.int32, (2 * CHUNK, 1), 0) < CHUNK
    even = lax.broadcasted_iota(jnp.int32, (2 * CHUNK, LANES), 1) < HEAD_DIM
    logits = jnp.where(valid, logits, NEG_INF)
    tiles = [logits[:, c:c + LANES] for c in range(0, 2 * N_KEYS, LANES)]
    s0 = jnp.where(row_lo, sinks_ref[4 * g + 0], sinks_ref[4 * g + 2])
    s1 = jnp.where(row_lo, sinks_ref[4 * g + 1], sinks_ref[4 * g + 3])
    tmax = functools.reduce(jnp.maximum, tiles)
    m0 = jnp.maximum(jnp.max(jnp.where(even, tmax, -jnp.inf), axis=1, keepdims=True), s0)
    m1 = jnp.maximum(jnp.max(jnp.where(even, -jnp.inf, tmax), axis=1, keepdims=True), s1)
    m = jnp.where(even, m0, m1)
    p = [jnp.exp(tile - m) for tile in tiles]
    psum = functools.reduce(jnp.add, p)
    d0 = jnp.sum(jnp.where(even, psum, 0.0), axis=1, keepdims=True) + jnp.exp(s0 - m0)
    d1 = jnp.sum(jnp.where(even, 0.0, psum), axis=1, keepdims=True) + jnp.exp(s1 - m1)
    o = _dot(jnp.concatenate(p, axis=1).astype(BF16), _blocked_kv(vwin, g)) / jnp.where(even, d0, d1)
    return jnp.concatenate([o[:CHUNK], o[CHUNK:]], axis=1)


def _ssd_constants(n_blocks):
    r = lax.broadcasted_iota(jnp.int32, (n_blocks * CHUNK, n_blocks * CHUNK), 0)
    c = lax.broadcasted_iota(jnp.int32, (n_blocks * CHUNK, n_blocks * CHUNK), 1)
    tri = jnp.where((c <= r) & ((c >> 6) == (r >> 6)), 1.0, 0.0).astype(BF16)
    hrow = lax.broadcasted_iota(jnp.int32, (LANES, SSD_INNER), 0)
    hcol = lax.broadcasted_iota(jnp.int32, (LANES, SSD_INNER), 1)
    expand = jnp.where(hcol >> 6 == hrow, 1.0, 0.0).astype(BF16)
    return tri, expand


def _ssd_prepare(chunks, dt_raws, dtb_ref, a_ref, nvalid, tri, expand):
    n = len(chunks)
    dt = _softplus(jnp.concatenate(dt_raws, axis=0) + dtb_ref[...])
    if nvalid < CHUNK:
        rows = lax.broadcasted_iota(jnp.int32, dt.shape, 0) & (CHUNK - 1)
        dt = jnp.where(rows < nvalid, dt, 0.0)
    acum3 = _dot(tri, jnp.concatenate(_split3(dt * a_ref[...]), axis=1))
    acum = acum3[:, :LANES] + acum3[:, LANES:2 * LANES] + acum3[:, 2 * LANES:]
    a_hi, a_mid, _ = _split3(acum)
    wide = _dot(jnp.concatenate([dt.astype(BF16), a_hi, a_mid], axis=0), expand)
    m = n * CHUNK
    for i, c in enumerate(chunks):
        sl = slice(i * CHUNK, (i + 1) * CHUNK)
        c["dt_exp"] = wide[sl]
        c["a_col"] = wide[m + i * CHUNK:m + (i + 1) * CHUNK] + wide[2 * m + i * CHUNK:2 * m + (i + 1) * CHUNK]


def _ssd_decay(c):
    a_col = c["a_col"]
    a_last = a_col[CHUNK - 1:CHUNK, :]
    c["xs"] = c["xbc_act"][:, :SSD_INNER]
    c["xdt"] = c["xs"] * c.pop("dt_exp")
    c["xw"] = c["xdt"] * jnp.exp(a_last - a_col)
    c["e_col"] = jnp.exp(a_col)
    c["e_last"] = jnp.exp(a_last)


def _ssd_group(c, gq, ht_ref, g_idx):
    l2 = lax.broadcasted_iota(jnp.int32, (CHUNK, LANES), 0)
    j2 = lax.broadcasted_iota(jnp.int32, (CHUNK, LANES), 1)
    s2 = j2 & (CHUNK - 1)
    diag_sel = jnp.where(s2 == l2, 1.0, 0.0)
    causal2 = s2 <= l2
    lane_lo = j2 < SSD_HEAD_DIM

    b0 = SSD_INNER + SSD_STATE * gq
    c0 = SSD_INNER + SSD_GROUPS * SSD_STATE + SSD_STATE * gq
    bg = c["xbc_act"][:, b0:b0 + SSD_STATE]
    cg = c["xbc_act"][:, c0:c0 + SSD_STATE].astype(BF16)
    bg_bf = bg.astype(BF16)
    cb2 = _dot_nt(cg, jnp.concatenate([bg_bf, bg_bf], axis=0))
    gsl = slice(GROUP_WIDTH * gq, GROUP_WIDTH * (gq + 1))
    h_prev = ht_ref[g_idx, :, gsl]
    y_off = _dot(cg, h_prev.astype(BF16)) * c["e_col"][:, gsl]
    yd = []
    for i in range(GROUP_WIDTH // LANES):
        psl = slice(GROUP_WIDTH * gq + LANES * i, GROUP_WIDTH * gq + LANES * (i + 1))
        ac = c["a_col"][:, psl]
        a_row = jnp.sum(ac * diag_sel, axis=0, keepdims=True)
        lmat = jnp.exp(jnp.where(causal2, ac - a_row, -jnp.inf))
        m2 = (cb2 * lmat).astype(BF16)
        xp = c["xdt"][:, psl]
        xblk = jnp.concatenate([jnp.where(lane_lo, xp, 0.0), jnp.where(lane_lo, 0.0, xp)],
                               axis=0).astype(BF16)
        yd.append(_dot(m2, xblk))
    c.setdefault("ys", []).append(jnp.concatenate(yd, axis=1) + y_off)
    st = _dot(bg.T.astype(BF16), c["xw"][:, gsl].astype(BF16))
    ht_ref[g_idx, :, gsl] = h_prev * c["e_last"][:, gsl] + st


class _LayerSinks:
    def __init__(self, ref, layer):
        self.ref, self.layer = ref, layer

    def __getitem__(self, head):
        return self.ref[self.layer, head]


def _mixer_kernel(*refs, layer, n_seq, n_chunk, n_tiles, nvalid, has_init):
    it = iter(refs)
    x_ref = next(it)
    bias_ref = next(it)
    sinks_ref = _LayerSinks(next(it), layer)
    gmix_ref = next(it)
    win_ref = next(it)
    wgate_ref = next(it)
    convw_ref = next(it)
    convb_ref = next(it)
    dtb_ref = next(it)
    a_ref = next(it)
    dskip_ref = next(it)
    gssd_ref = next(it)
    watt_ref = next(it)
    wssd_ref = next(it)
    wout_ref = next(it)
    if has_init:
        ki_ref = next(it)
        vi_ref = next(it)
        convi_ref = next(it)
        ssmi_ref = next(it)
    h_ref = next(it)
    ko_ref = next(it)
    vo_ref = next(it)
    convo_ref = next(it)
    ssmo_ref = next(it)
    hn_s = next(it)
    q_s = next(it)
    kv_s = next(it)
    xbc_s = next(it)
    xact_s = next(it)
    dt_s = next(it)
    oatt_s = next(it)
    y_s = next(it)
    ht_s = next(it)
    gate_s = next(it)

    t = pl.program_id(1)
    seq_rows = n_chunk * CHUNK

    @pl.when(t == 0)
    def _init():
        if has_init:
            kv_s[:, 0:WINDOW, :KV_WIDTH] = ki_ref[...]
            kv_s[:, 0:WINDOW, KV_WIDTH:] = vi_ref[...]
            xbc_s[:, :, 0:HIST_ROWS, :] = jnp.zeros((XBC_BLOCKS, n_seq, HIST_ROWS, MXU_WIDTH), F32)
            for cb in range(XBC_BLOCKS):
                xbc_s[cb, :, HIST_ROWS - (CONV_W - 1):HIST_ROWS, :] = convi_ref[
                    :, :, cb * MXU_WIDTH:(cb + 1) * MXU_WIDTH]
            for g in range(n_seq):
                ht_s[g] = ssmi_ref[g].T
        else:
            kv_s[:, 0:WINDOW, :] = jnp.zeros((n_seq, WINDOW, 2 * KV_WIDTH), F32)
            xbc_s[:, :, 0:HIST_ROWS, :] = jnp.zeros((XBC_BLOCKS, n_seq, HIST_ROWS, MXU_WIDTH), F32)
            ht_s[...] = jnp.zeros(ht_s.shape, F32)

    rows = n_seq * seq_rows
    hn = _rmsnorm(x_ref[...].reshape(rows, D_MODEL), gmix_ref[...]).astype(BF16)
    hn_s[...] = hn

    def project_block(cb):
        c0 = OFF_XBC + cb * MXU_WIDTH
        blk = _dot(hn, win_ref[:, c0:c0 + MXU_WIDTH])
        for g in range(n_seq):
            xbc_s[cb, g, HIST_ROWS:HIST_ROWS + seq_rows, :] = blk[g * seq_rows:(g + 1) * seq_rows]

    def conv_block(cb):
        csl = slice(cb * MXU_WIDTH, (cb + 1) * MXU_WIDTH)
        for g in range(n_seq):
            conv = convb_ref[:, csl] + convw_ref[CONV_W - 1:CONV_W, csl] * xbc_s[cb, g, HIST_ROWS:HIST_ROWS + seq_rows, :]
            for i in range(1, CONV_W):
                conv = conv + (convw_ref[CONV_W - 1 - i:CONV_W - i, csl]
                               * xbc_s[cb, g, HIST_ROWS - i:HIST_ROWS - i + seq_rows, :])
            xact_s[cb, g * seq_rows:(g + 1) * seq_rows, :] = _silu(conv)

    project_block(0)
    for cb in range(1, XBC_BLOCKS):
        project_block(cb)
        conv_block(cb - 1)
    q_s[...] = _dot(hn, win_ref[:, OFF_Q:OFF_KV]) * (HEAD_DIM ** -0.5)
    conv_block(XBC_BLOCKS - 1)
    kv = _dot(hn, win_ref[:, OFF_KV:OFF_XBC])
    for g in range(n_seq):
        kv_s[g, WINDOW:WINDOW + seq_rows, :] = kv[g * seq_rows:(g + 1) * seq_rows]
    dt_s[...] = _dot(hn, win_ref[:, OFF_DT:IN_PACKED])

    cols = lax.broadcasted_iota(jnp.int32, (2 * CHUNK, 2 * N_KEYS), 1)
    colmod = ((cols >> 7) << 6) + (cols & (CHUNK - 1))
    tri, expand = _ssd_constants(n_seq)

    slab = GATE_WIDTH // GATE_SLABS
    slabs_per_iter = GATE_SLABS // n_chunk

    def iter_body(j, carry):
        k0 = pl.multiple_of(j * CHUNK, CHUNK) if n_chunk > 1 else 0
        per_slab = slab // MXU_WIDTH
        n_pieces = slabs_per_iter * per_slab
        n_points = FILL_POINTS * n_seq
        calls = [0]

        first = FILL_FIRST_PHASE * n_seq
        span = n_points - first

        def fill():
            k = calls[0] - first
            calls[0] += 1
            if k < 0:
                return
            for p in range(-(-k * n_pieces // span), -(-(k + 1) * n_pieces // span)):
                s = j * slabs_per_iter + p // per_slab
                c0 = (p % per_slab) * MXU_WIDTH
                gate_s[s, :, c0:c0 + MXU_WIDTH] = _dot(hn_s[...], wgate_ref[s, :, c0:c0 + MXU_WIDTH])

        if has_init:
            valid = colmod < WINDOW + nvalid
        else:
            first_valid = jnp.maximum(0, (2 - (t * n_chunk + j)) * CHUNK)
            valid = colmod >= first_valid
        chunks = [{"g": g, "rows": pl.ds(pl.multiple_of(g * seq_rows + k0, CHUNK), CHUNK)}
                  for g in range(n_seq)]
        for c in chunks:
            kvwin = kv_s[c["g"], pl.ds(k0, N_KEYS), :]
            c["vwin"] = kvwin[:, KV_WIDTH:]
            q = q_s[c["rows"], :]
            c["logits"] = [_attention_logits(q, kvwin[:, :KV_WIDTH], bias_ref, hg) for hg in range(N_KV_HEADS)]
            fill()
        for c in chunks:
            c["xbc_act"] = jnp.concatenate([xact_s[cb, c["rows"], :] for cb in range(XBC_BLOCKS)], axis=1)
            fill()
        for c in chunks:
            outs = [_attention_finish(c["logits"][hg], c["vwin"], sinks_ref, valid, hg)
                    for hg in range(N_KV_HEADS)]
            oatt_s[c["rows"], :] = jnp.concatenate(outs, axis=1)
            del c["logits"], c["vwin"]
            fill()
        _ssd_prepare(chunks, [dt_s[c["rows"], :] for c in chunks], dtb_ref, a_ref, nvalid, tri, expand)
        for _ in chunks:
            fill()
        for c in chunks:
            _ssd_decay(c)
            fill()
        for gq in range(SSD_GROUPS):
            for c in chunks:
                _ssd_group(c, gq, ht_s, c["g"])
                fill()
        for c in chunks:
            y_s[c["rows"], :] = jnp.concatenate(c["ys"], axis=1) + c["xs"] * dskip_ref[...]
        assert calls[0] == n_points
        return carry

    for j in range(n_chunk):
        iter_body(j, 0)

    def gate_cols(lo, hi):
        pieces = []
        while lo < hi:
            s, off = divmod(lo, slab)
            take = min(hi - lo, slab - off)
            pieces.append(gate_s[s, :, off:off + take])
            lo += take
        return jnp.concatenate(pieces, axis=1)

    y_ssd = _rmsnorm(y_s[...] * _silu(gate_cols(GATE_Z, GATE_A)), gssd_ref[...]).astype(BF16)
    merged = (_sigmoid(gate_cols(GATE_A, GATE_S)) * _dot(oatt_s[...].astype(BF16), watt_ref[...])
              + _sigmoid(gate_cols(GATE_S, GATE_WIDTH)) * _dot(y_ssd, wssd_ref[...]))
    h = x_ref[...].reshape(rows, D_MODEL) + _dot(merged.astype(BF16), wout_ref[...])
    h_ref[...] = h.reshape(n_seq, seq_rows, D_MODEL)

    @pl.when(t == n_tiles - 1)
    def _emit_states():
        ko_ref[...] = kv_s[:, nvalid:nvalid + WINDOW, :KV_WIDTH]
        vo_ref[...] = kv_s[:, nvalid:nvalid + WINDOW, KV_WIDTH:]
        last = HIST_ROWS + nvalid
        convo_ref[...] = jnp.concatenate(
            [xbc_s[cb, :, last - (CONV_W - 1):last, :] for cb in range(XBC_BLOCKS)], axis=-1)
        for g in range(n_seq):
            ssmo_ref[g] = ht_s[g].T

    if n_tiles > 1:
        kv_s[:, 0:WINDOW, :] = kv_s[:, seq_rows:seq_rows + WINDOW, :]
        xbc_s[:, :, 0:HIST_ROWS, :] = xbc_s[:, :, seq_rows:seq_rows + HIST_ROWS, :]


def _layer_spec(shape, layer):
    nd = len(shape)
    return pl.BlockSpec((None,) + tuple(shape), lambda *_, _l=layer, _nd=nd: (_l,) + (0,) * _nd,
                        pipeline_mode=pl.Buffered(1))


def _mixer(x3d, bias, wts, layer, init, *, n_seq, n_chunk, nvalid):
    batch, seq_pad, _ = x3d.shape
    seq_rows = n_chunk * CHUNK
    n_tiles = seq_pad // seq_rows
    assert batch % n_seq == 0 and seq_pad % seq_rows == 0 and GATE_SLABS % n_chunk == 0
    rows = n_seq * seq_rows
    has_init = init is not None
    last_valid = nvalid - (n_tiles - 1) * seq_rows

    tile_map = lambda b, t: (b, t, 0)
    seq_map = lambda b, t: (b, 0, 0)
    cache_map = lambda b, t: (layer, b, 0, 0)
    spec = functools.partial(_layer_spec, layer=layer)
    in_specs = [
        pl.BlockSpec((n_seq, seq_rows, D_MODEL), tile_map),
        pl.BlockSpec(bias.shape, lambda b, t: (0, 0, 0), pipeline_mode=pl.Buffered(1)),
        pl.BlockSpec(memory_space=pltpu.SMEM),
        spec((1, D_MODEL)),
        spec((D_MODEL, IN_PACKED)),
        spec((GATE_SLABS, D_MODEL, GATE_WIDTH // GATE_SLABS)),
        spec((CONV_W, CONV_DIM)),
        spec((1, CONV_DIM)),
        spec((1, LANES)),
        spec((1, LANES)),
        spec((1, SSD_INNER)),
        spec((1, SSD_INNER)),
        spec((ATT_WIDTH, D_MODEL)),
        spec((SSD_INNER, D_MODEL)),
        spec((D_MODEL, D_MODEL)),
    ]
    args = [x3d, bias, wts["sinks"], wts["g_mix"], wts["w_in"], wts["w_gates"], wts["conv_w"], wts["conv_b"],
            wts["dt_bias"], wts["a"], wts["d_skip"], wts["g_ssd"], wts["w_att_out"], wts["w_ssd_out"],
            wts["w_out"]]
    if has_init:
        in_specs += [
            pl.BlockSpec((None, n_seq, WINDOW, KV_WIDTH), cache_map),
            pl.BlockSpec((None, n_seq, WINDOW, KV_WIDTH), cache_map),
            pl.BlockSpec((None, n_seq, CONV_W - 1, CONV_DIM), cache_map),
            pl.BlockSpec((None, n_seq, SSD_INNER, SSD_STATE), cache_map),
        ]
        args += list(init)
    out_shape = [
        jax.ShapeDtypeStruct((batch, seq_pad, D_MODEL), F32),
        jax.ShapeDtypeStruct((batch, WINDOW, KV_WIDTH), F32),
        jax.ShapeDtypeStruct((batch, WINDOW, KV_WIDTH), F32),
        jax.ShapeDtypeStruct((batch, CONV_W - 1, CONV_DIM), F32),
        jax.ShapeDtypeStruct((batch, SSD_INNER, SSD_STATE), F32),
    ]
    out_specs = [
        pl.BlockSpec((n_seq, seq_rows, D_MODEL), tile_map),
        pl.BlockSpec((n_seq, WINDOW, KV_WIDTH), seq_map),
        pl.BlockSpec((n_seq, WINDOW, KV_WIDTH), seq_map),
        pl.BlockSpec((n_seq, CONV_W - 1, CONV_DIM), seq_map),
        pl.BlockSpec((n_seq, SSD_INNER, SSD_STATE), seq_map),
    ]
    scratch = [
        pltpu.VMEM((rows, D_MODEL), BF16),
        pltpu.VMEM((rows, ATT_WIDTH), F32),
        pltpu.VMEM((n_seq, WINDOW + seq_rows, 2 * KV_WIDTH), F32),
        pltpu.VMEM((XBC_BLOCKS, n_seq, HIST_ROWS + seq_rows, MXU_WIDTH), F32),
        pltpu.VMEM((XBC_BLOCKS, rows, MXU_WIDTH), F32),
        pltpu.VMEM((rows, LANES), F32),
        pltpu.VMEM((rows, ATT_WIDTH), F32),
        pltpu.VMEM((rows, SSD_INNER), F32),
        pltpu.VMEM((n_seq, SSD_STATE, SSD_INNER), F32),
        pltpu.VMEM((GATE_SLABS, rows, GATE_WIDTH // GATE_SLABS), F32),
    ]
    kern = functools.partial(_mixer_kernel, layer=layer, n_seq=n_seq, n_chunk=n_chunk, n_tiles=n_tiles,
                             nvalid=last_valid, has_init=has_init)
    return pl.pallas_call(
        kern,
        grid=(batch // n_seq, n_tiles),
        in_specs=in_specs,
        out_specs=out_specs,
        out_shape=out_shape,
        scratch_shapes=scratch,
        compiler_params=pltpu.CompilerParams(
            dimension_semantics=("arbitrary", "arbitrary"), vmem_limit_bytes=VMEM_LIMIT_BYTES),
        name="mixer_init" if has_init else "mixer",
    )(*args)


def _ffn_kernel(h_ref, g_ref, wg_ref, wu_ref, wd_ref, gfin_ref, o_ref, *, final):
    h = h_ref[...]
    hf = _rmsnorm(h, g_ref[...]).astype(BF16)
    act = (_silu(_dot(hf, wg_ref[...])) * _dot(hf, wu_ref[...])).astype(BF16)
    out = h + _dot(act, wd_ref[...])
    if final:
        out = _rmsnorm(out, gfin_ref[...])
    o_ref[...] = out


def _ffn(h2d, wts, layer, g_final, *, block_rows, final):
    n_rows = h2d.shape[0]
    assert n_rows % block_rows == 0
    spec = functools.partial(_layer_spec, layer=layer)
    return pl.pallas_call(
        functools.partial(_ffn_kernel, final=final),
        grid=(n_rows // block_rows,),
        in_specs=[
            pl.BlockSpec((block_rows, D_MODEL), lambda i: (i, 0)),
            spec((1, D_MODEL)),
            spec((D_MODEL, D_FF)),
            spec((D_MODEL, D_FF)),
            spec((D_FF, D_MODEL)),
            pl.BlockSpec((1, D_MODEL), lambda i: (0, 0), pipeline_mode=pl.Buffered(1)),
        ],
        out_specs=pl.BlockSpec((block_rows, D_MODEL), lambda i: (i, 0)),
        out_shape=jax.ShapeDtypeStruct((n_rows, D_MODEL), F32),
        compiler_params=pltpu.CompilerParams(
            dimension_semantics=("arbitrary",), vmem_limit_bytes=VMEM_LIMIT_BYTES),
        name="ffn_final" if final else "ffn",
    )(h2d, wts["g_ffn"], wts["w_gate"], wts["w_up"], wts["w_down"], g_final)


def _pack_plan():
    o_z = ATT_WIDTH + 2 * KV_WIDTH
    o_xbc = o_z + SSD_INNER
    o_dt = o_xbc + CONV_DIM
    o_ga = o_dt + SSD_HEADS
    o_gs = o_ga + D_MODEL
    w = MXU_WIDTH
    steps = [(c, c // w, -1) for c in range(0, o_z, w)]
    steps += [(o_xbc + c, (OFF_XBC + c) // w, -1) for c in range(0, CONV_DIM, w)]
    steps += [(o_dt, OFF_DT // w, -1)]
    for dst, src, width in ((GATE_Z, o_z, SSD_INNER), (GATE_A, o_ga, D_MODEL), (GATE_S, o_gs, D_MODEL)):
        steps += [(src + c, -1, (dst + c) // w) for c in range(0, width, w)]
    return steps


def _pack_w_in_kernel(src_ref, pblk_ref, gblk_ref, w_ref, packed_ref, gates_ref, *, n_packed):
    del src_ref, pblk_ref, gblk_ref
    i = pl.program_id(1)
    t = w_ref[0].T.astype(BF16)

    @pl.when(i < n_packed - 1)
    def _():
        packed_ref[...] = t

    @pl.when(i == n_packed - 1)
    def _():
        lane = lax.broadcasted_iota(jnp.int32, t.shape, 1)
        packed_ref[...] = jnp.where(lane < SSD_HEADS, t, jnp.zeros_like(t))

    @pl.when(i >= n_packed)
    def _():
        gates_ref[...] = t


def _pack_w_in(w_in):
    depth, d, _ = w_in.shape
    plan = _pack_plan()
    n_packed = sum(1 for _, p, _ in plan if p >= 0)
    per_slab = GATE_WIDTH // GATE_SLABS // MXU_WIDTH
    assert all(s % SUBLANES == 0 for s, _, _ in plan)
    src = jnp.asarray([s // SUBLANES for s, _, _ in plan], jnp.int32)
    pblk = jnp.asarray([p if p >= 0 else plan[n_packed - 1][1] for _, p, _ in plan], jnp.int32)
    gblk = jnp.asarray([g if g >= 0 else plan[n_packed][2] for _, _, g in plan], jnp.int32)
    grid_spec = pltpu.PrefetchScalarGridSpec(
        num_scalar_prefetch=3,
        grid=(depth, len(plan)),
        in_specs=[pl.BlockSpec((pl.Element(1), pl.Element(MXU_WIDTH), pl.Element(d)),
                               lambda l, i, s, p, g: (l, s[i] * SUBLANES, 0))],
        out_specs=[pl.BlockSpec((None, d, MXU_WIDTH), lambda l, i, s, p, g: (l, 0, p[i])),
                   pl.BlockSpec((None, None, d, MXU_WIDTH),
                                lambda l, i, s, p, g: (l, g[i] // per_slab, 0, g[i] % per_slab))],
    )
    return pl.pallas_call(
        functools.partial(_pack_w_in_kernel, n_packed=n_packed),
        grid_spec=grid_spec,
        out_shape=[jax.ShapeDtypeStruct((depth, d, IN_PACKED), BF16),
                   jax.ShapeDtypeStruct((depth, GATE_SLABS, d, GATE_WIDTH // GATE_SLABS), BF16)],
        compiler_params=pltpu.CompilerParams(dimension_semantics=("arbitrary", "arbitrary")),
        name="pack_w_in",
    )(src, pblk, gblk, jnp.swapaxes(w_in, 1, 2))


def _cast_kernel(w_ref, o_ref):
    o_ref[...] = w_ref[...].astype(BF16)


def _cast_bf16(w):
    depth, r, c = w.shape
    n_rows = depth * r
    block_rows = PREP_ROWS
    while (n_rows % (2 * block_rows) == 0 and n_rows // (2 * block_rows) >= CAST_MIN_STEPS
           and 2 * block_rows * c * 4 <= CAST_BLOCK_BYTES):
        block_rows *= 2
    assert n_rows % block_rows == 0
    out = pl.pallas_call(
        _cast_kernel,
        grid=(n_rows // block_rows,),
        in_specs=[pl.BlockSpec((block_rows, c), lambda i: (i, 0))],
        out_specs=pl.BlockSpec((block_rows, c), lambda i: (i, 0)),
        out_shape=jax.ShapeDtypeStruct((n_rows, c), BF16),
        compiler_params=pltpu.CompilerParams(dimension_semantics=("arbitrary",),
                                             vmem_limit_bytes=VMEM_LIMIT_BYTES),
        name="cast_bf16",
    )(w.reshape(n_rows, c))
    return out.reshape(depth, r, c)


def _prepare_weights(g_mix, w_in, conv_w, conv_b, dt_bias, a_log, d_skip, g_ssd, sinks, w_att_out, w_ssd_out,
                     w_out, g_ffn, w_gate, w_up, w_down):
    pad_h = ((0, 0), (0, LANES - SSD_HEADS))
    w_packed, w_gates = _pack_w_in(w_in)
    return {
        "sinks": sinks,
        "g_mix": g_mix[:, None, :],
        "w_in": w_packed,
        "w_gates": w_gates,
        "conv_w": conv_w,
        "conv_b": conv_b[:, None, :],
        "dt_bias": jnp.pad(dt_bias, pad_h)[:, None, :],
        "a": jnp.pad(-jnp.exp(a_log), pad_h)[:, None, :],
        "d_skip": jnp.repeat(d_skip, SSD_HEAD_DIM, axis=1)[:, None, :],
        "g_ssd": g_ssd[:, None, :],
        "w_att_out": _cast_bf16(w_att_out),
        "w_ssd_out": _cast_bf16(w_ssd_out),
        "w_out": _cast_bf16(w_out),
        "g_ffn": g_ffn[:, None, :],
        "w_gate": _cast_bf16(w_gate),
        "w_up": _cast_bf16(w_up),
        "w_down": _cast_bf16(w_down),
    }


PREP_ROWS = 256
CAST_BLOCK_BYTES = 8 * 1024 * 1024
CAST_MIN_STEPS = 4
PROMPT_SEQS_PER_TILE = 4
PROMPT_CHUNKS_PER_TILE = 2
SAMPLE_SEQS_PER_TILE = 4
FFN_BLOCK_ROWS = 512


def kernel(x_prompt, x_sample, cache_k, cache_v, state_conv, state_ssm, rel_table, g_mix, w_in, conv_w, conv_b, dt_bias, a_log, d_skip, g_ssd, sinks, w_att_out, w_ssd_out, w_out, g_ffn, w_gate, w_up, w_down, g_final):
    depth = w_in.shape[0]
    bp, sp, _ = x_prompt.shape
    bs, ts, _ = x_sample.shape
    kv_len = cache_k.shape[2]
    assert kv_len == WINDOW and ts <= CHUNK and ts % 8 == 0 and ts >= CONV_W - 1

    bias = _blocked_bias(rel_table)
    g_fin = g_final[None, :]

    wts = _prepare_weights(g_mix, w_in, conv_w, conv_b, dt_bias, a_log, d_skip, g_ssd, sinks, w_att_out,
                           w_ssd_out, w_out, g_ffn, w_gate, w_up, w_down)
    init = (cache_k.reshape(depth, bs, WINDOW, KV_WIDTH), cache_v.reshape(depth, bs, WINDOW, KV_WIDTH),
            state_conv, state_ssm.reshape(depth, bs, SSD_INNER, SSD_STATE))

    xp = x_prompt
    xs = x_sample.reshape(bs * ts, D_MODEL)
    st_p, st_s = [], []
    for l in range(depth):
        final = l == depth - 1
        hp, *state_p = _mixer(xp, bias, wts, l, None, n_seq=PROMPT_SEQS_PER_TILE,
                              n_chunk=PROMPT_CHUNKS_PER_TILE, nvalid=sp)
        xp = _ffn(hp.reshape(bp * sp, D_MODEL), wts, l, g_fin, block_rows=FFN_BLOCK_ROWS,
                  final=final).reshape(bp, sp, D_MODEL)
        st_p.append(state_p)
        xs_pad = jnp.pad(xs.reshape(bs, ts, D_MODEL), ((0, 0), (0, CHUNK - ts), (0, 0)))
        hs, *state_s = _mixer(xs_pad, bias, wts, l, init, n_seq=SAMPLE_SEQS_PER_TILE, n_chunk=1, nvalid=ts)
        xs = _ffn(hs[:, :ts].reshape(bs * ts, D_MODEL), wts, l, g_fin, block_rows=bs * ts, final=final)
        st_s.append(state_s)

    def states(sts, b):
        k, v, conv, ssm = (jnp.stack(leaves) for leaves in zip(*sts))
        return (k.reshape(depth, b, WINDOW, N_KV_HEADS, HEAD_DIM),
                v.reshape(depth, b, WINDOW, N_KV_HEADS, HEAD_DIM), conv,
                ssm.reshape(depth, b, SSD_HEADS, SSD_HEAD_DIM, SSD_STATE))

    return (xp, xs.reshape(x_sample.shape), *states(st_p, bp), *states(st_s, bs))
```

```python
import functools
import math

import jax
import jax.numpy as jnp
from jax import lax
from jax.experimental import pallas as pl
from jax.experimental.pallas import tpu as pltpu

D_MODEL = 1024
CHUNK = 64
EPS = 1e-6
NEG_INF = -1e30
N_HEADS = 8
N_KV_HEADS = 2
HEAD_DIM = 64
ATT_WIDTH = N_HEADS * HEAD_DIM
KV_WIDTH = N_KV_HEADS * HEAD_DIM
WINDOW = 128
N_KEYS = WINDOW + CHUNK
NUM_BUCKETS = 32
MAX_DISTANCE = 128
SSD_INNER = 1024
SSD_HEADS = 16
SSD_HEAD_DIM = 64
SSD_GROUPS = 2
SSD_STATE = 128
GROUP_WIDTH = SSD_INNER // SSD_GROUPS
CONV_W = 4
CONV_DIM = SSD_INNER + 2 * SSD_GROUPS * SSD_STATE
D_FF = 2816
LANES = 128
SUBLANES = 8
HIST_ROWS = 8
MXU_WIDTH = 256
XBC_BLOCKS = CONV_DIM // MXU_WIDTH
FILL_POINTS = 7
FILL_FIRST_PHASE = 3

OFF_Q = 0
OFF_KV = OFF_Q + ATT_WIDTH
OFF_XBC = OFF_KV + 2 * KV_WIDTH
OFF_DT = OFF_XBC + CONV_DIM
IN_PACKED = OFF_DT + LANES
GATE_Z = 0
GATE_A = GATE_Z + SSD_INNER
GATE_S = GATE_A + D_MODEL
GATE_WIDTH = GATE_S + D_MODEL
GATE_SLABS = 2
PACKED_BLOCKS = -(-IN_PACKED // MXU_WIDTH)
GATE_BLOCKS = GATE_WIDTH // MXU_WIDTH

VMEM_LIMIT_BYTES = 60 * 1024 * 1024

F32 = jnp.float32
BF16 = jnp.bfloat16


def _dot(a, b):
    return jnp.dot(a, b, preferred_element_type=F32)


def _dot_nt(a, b):
    return lax.dot_general(a, b, (((1,), (1,)), ((), ())), preferred_element_type=F32)


def _split3(x):
    hi = x.astype(BF16)
    r1 = x - hi.astype(F32)
    mid = r1.astype(BF16)
    lo = (r1 - mid.astype(F32)).astype(BF16)
    return hi, mid, lo


def _exact_dot_right(x, sel, passes=3):
    parts = _split3(x)[:passes]
    out = _dot(parts[0], sel)
    for part in parts[1:]:
        out = out + _dot(part, sel)
    return out


def _exact_dot_left(sel, x):
    hi, mid, lo = _split3(x)
    return _dot(sel, hi) + _dot(sel, mid) + _dot(sel, lo)


def _rmsnorm(x, g):
    return x * lax.rsqrt(jnp.mean(x * x, axis=-1, keepdims=True) + EPS) * g


def _sigmoid(x):
    return 1.0 / (1.0 + jnp.exp(-x))


def _silu(x):
    return x * _sigmoid(x)


def _softplus(x):
    return jnp.maximum(x, 0.0) + jnp.log1p(jnp.exp(-jnp.abs(x)))


def _bias_kernel(table_ref, bucket_ref, o_ref):
    bucket = bucket_ref[...]
    rows = lax.broadcasted_iota(jnp.int32, bucket.shape, 0)
    cols = lax.broadcasted_iota(jnp.int32, bucket.shape, 1)
    row_hi = rows >= CHUNK
    col_hi = (cols & HEAD_DIM) != 0
    for g in range(N_KV_HEADS):
        acc = jnp.zeros(bucket.shape, F32)
        for b in range(NUM_BUCKETS):
            t0 = table_ref[b, 4 * g + 0]
            t1 = table_ref[b, 4 * g + 1]
            t2 = table_ref[b, 4 * g + 2]
            t3 = table_ref[b, 4 * g + 3]
            tv = jnp.where(row_hi, jnp.where(col_hi, t3, t2), jnp.where(col_hi, t1, t0))
            acc = jnp.where(bucket == b, tv, acc)
        o_ref[g] = acc


def _t5_bucket(rel):
    nb = NUM_BUCKETS // 2
    max_exact = nb // 2
    ret = jnp.where(rel > 0, nb, 0)
    n = jnp.abs(rel)
    nf = jnp.maximum(n, 1).astype(jnp.float32)
    large = max_exact + (jnp.log(nf / max_exact) / math.log(MAX_DISTANCE / max_exact)
                         * (nb - max_exact)).astype(jnp.int32)
    large = jnp.minimum(large, nb - 1)
    return ret + jnp.where(n < max_exact, n, large)


def _blocked_bias(rel_table):
    qi = jnp.arange(CHUNK, dtype=jnp.int32)
    kj = jnp.arange(N_KEYS, dtype=jnp.int32)
    bucket = _t5_bucket(kj[None, :] - WINDOW - qi[:, None]).astype(jnp.int32)
    bucket = jnp.tile(bucket.reshape(CHUNK, N_KEYS // CHUNK, 1, CHUNK), (2, 1, 2, 1)).reshape(2 * CHUNK, 2 * N_KEYS)
    return pl.pallas_call(
        _bias_kernel,
        out_shape=jax.ShapeDtypeStruct((N_KV_HEADS, 2 * CHUNK, 2 * N_KEYS), F32),
        in_specs=[pl.BlockSpec(memory_space=pltpu.SMEM),
                  pl.BlockSpec(memory_space=pltpu.VMEM)],
        out_specs=pl.BlockSpec(memory_space=pltpu.VMEM),
        name="rel_bias",
    )(rel_table, bucket)


def _blocked_kv(win, g):
    lo_half = lax.broadcasted_iota(jnp.int32, (N_KEYS, LANES), 1) < HEAD_DIM
    if g == 0:
        a0 = jnp.where(lo_half, win, 0.0)
        a1 = pltpu.roll(a0, HEAD_DIM, axis=1)
    else:
        a1 = jnp.where(lo_half, 0.0, win)
        a0 = pltpu.roll(a1, HEAD_DIM, axis=1)
    pieces = []
    for r in range(0, N_KEYS, CHUNK):
        pieces += [a0[r:r + CHUNK], a1[r:r + CHUNK]]
    return jnp.concatenate(pieces, axis=0).astype(BF16)


def _attention_logits(q, kwin, bias_ref, g):
    qs = jnp.concatenate([q[:, 256 * g:256 * g + LANES],
                          q[:, 256 * g + LANES:256 * (g + 1)]], axis=0).astype(BF16)
    return _dot_nt(qs, _blocked_kv(kwin, g)) + bias_ref[g]


def _attention_finish(logits, vwin, sinks_ref, valid, g):
    row_lo = lax.broadcasted_iota(jnp.int32, (2 * CHUNK, 1), 0) < CHUNK
    even = lax.broadcasted_iota(jnp.int32, (2 * CHUNK, LANES), 1) < HEAD_DIM
    logits = jnp.where(valid, logits, NEG_INF)
    tiles = [logits[:, c:c + LANES] for c in range(0, 2 * N_KEYS, LANES)]
    s0 = jnp.where(row_lo, sinks_ref[4 * g + 0], sinks_ref[4 * g + 2])
    s1 = jnp.where(row_lo, sinks_ref[4 * g + 1], sinks_ref[4 * g + 3])
    tmax = functools.reduce(jnp.maximum, tiles)
    m0 = jnp.maximum(jnp.max(jnp.where(even, tmax, -jnp.inf), axis=1, keepdims=True), s0)
    m1 = jnp.maximum(jnp.max(jnp.where(even, -jnp.inf, tmax), axis=1, keepdims=True), s1)
    m = jnp.where(even, m0, m1)
    p = [jnp.exp(tile - m) for tile in tiles]
    psum = functools.reduce(jnp.add, p)
    d0 = jnp.sum(jnp.where(even, psum, 0.0), axis=1, keepdims=True) + jnp.exp(s0 - m0)
    d1 = jnp.sum(jnp.where(even, 0.0, psum), axis=1, keepdims=True) + jnp.exp(s1 - m1)
    o = _dot(jnp.concatenate(p, axis=1).astype(BF16), _blocked_kv(vwin, g)) / jnp.where(even, d0, d1)
    return jnp.concatenate([o[:CHUNK], o[CHUNK:]], axis=1)


def _ssd_constants(n_blocks):
    r = lax.broadcasted_iota(jnp.int32, (n_blocks * CHUNK, n_blocks * CHUNK), 0)
    c = lax.broadcasted_iota(jnp.int32, (n_blocks * CHUNK, n_blocks * CHUNK), 1)
    tri = jnp.where((c <= r) & ((c >> 6) == (r >> 6)), 1.0, 0.0).astype(BF16)
    hrow = lax.broadcasted_iota(jnp.int32, (LANES, SSD_INNER), 0)
    hcol = lax.broadcasted_iota(jnp.int32, (LANES, SSD_INNER), 1)
    expand = jnp.where(hcol >> 6 == hrow, 1.0, 0.0).astype(BF16)
    return tri, expand


def _ssd_prepare(chunks, dt_raws, dtb_ref, a_ref, nvalid, tri, expand):
    n = len(chunks)
    dt = _softplus(jnp.concatenate(dt_raws, axis=0) + dtb_ref[...])
    if nvalid < CHUNK:
        rows = lax.broadcasted_iota(jnp.int32, dt.shape, 0) & (CHUNK - 1)
        dt = jnp.where(rows < nvalid, dt, 0.0)
    acum3 = _dot(tri, jnp.concatenate(_split3(dt * a_ref[...]), axis=1))
    acum = acum3[:, :LANES] + acum3[:, LANES:2 * LANES] + acum3[:, 2 * LANES:]
    a_hi, a_mid, _ = _split3(acum)
    wide = _dot(jnp.concatenate([dt.astype(BF16), a_hi, a_mid], axis=0), expand)
    m = n * CHUNK
    for i, c in enumerate(chunks):
        sl = slice(i * CHUNK, (i + 1) * CHUNK)
        c["dt_exp"] = wide[sl]
        c["a_col"] = wide[m + i * CHUNK:m + (i + 1) * CHUNK] + wide[2 * m + i * CHUNK:2 * m + (i + 1) * CHUNK]


def _ssd_decay(c):
    a_col = c["a_col"]
    a_last = a_col[CHUNK - 1:CHUNK, :]
    c["xs"] = c["xbc_act"][:, :SSD_INNER]
    c["xdt"] = c["xs"] * c.pop("dt_exp")
    c["xw"] = c["xdt"] * jnp.exp(a_last - a_col)
    c["e_col"] = jnp.exp(a_col)
    c["e_last"] = jnp.exp(a_last)


def _ssd_group(c, gq, ht_ref, g_idx):
    l2 = lax.broadcasted_iota(jnp.int32, (CHUNK, LANES), 0)
    j2 = lax.broadcasted_iota(jnp.int32, (CHUNK, LANES), 1)
    s2 = j2 & (CHUNK - 1)
    diag_sel = jnp.where(s2 == l2, 1.0, 0.0)
    causal2 = s2 <= l2
    lane_lo = j2 < SSD_HEAD_DIM

    b0 = SSD_INNER + SSD_STATE * gq
    c0 = SSD_INNER + SSD_GROUPS * SSD_STATE + SSD_STATE * gq
    bg = c["xbc_act"][:, b0:b0 + SSD_STATE]
    cg = c["xbc_act"][:, c0:c0 + SSD_STATE].astype(BF16)
    bg_bf = bg.astype(BF16)
    cb2 = _dot_nt(cg, jnp.concatenate([bg_bf, bg_bf], axis=0))
    gsl = slice(GROUP_WIDTH * gq, GROUP_WIDTH * (gq + 1))
    h_prev = ht_ref[g_idx, :, gsl]
    y_off = _dot(cg, h_prev.astype(BF16)) * c["e_col"][:, gsl]
    yd = []
    for i in range(GROUP_WIDTH // LANES):
        psl = slice(GROUP_WIDTH * gq + LANES * i, GROUP_WIDTH * gq + LANES * (i + 1))
        ac = c["a_col"][:, psl]
        a_row = jnp.sum(ac * diag_sel, axis=0, keepdims=True)
        lmat = jnp.exp(jnp.where(causal2, ac - a_row, -jnp.inf))
        m2 = (cb2 * lmat).astype(BF16)
        xp = c["xdt"][:, psl]
        xblk = jnp.concatenate([jnp.where(lane_lo, xp, 0.0), jnp.where(lane_lo, 0.0, xp)],
                               axis=0).astype(BF16)
        yd.append(_dot(m2, xblk))
    c.setdefault("ys", []).append(jnp.concatenate(yd, axis=1) + y_off)
    st = _dot(bg.T.astype(BF16), c["xw"][:, gsl].astype(BF16))
    ht_ref[g_idx, :, gsl] = h_prev * c["e_last"][:, gsl] + st


class _LayerSinks:
    def __init__(self, ref, layer):
        self.ref, self.layer = ref, layer

    def __getitem__(self, head):
        return self.ref[self.layer, head]


def _mixer_kernel(*refs, layer, n_seq, n_chunk, n_tiles, nvalid, has_init):
    it = iter(refs)
    x_ref = next(it)
    bias_ref = next(it)
    sinks_ref = _LayerSinks(next(it), layer)
    gmix_ref = next(it)
    win_ref = next(it)
    wgate_ref = next(it)
    convw_ref = next(it)
    convb_ref = next(it)
    dtb_ref = next(it)
    a_ref = next(it)
    dskip_ref = next(it)
    gssd_ref = next(it)
    watt_ref = next(it)
    wssd_ref = next(it)
    wout_ref = next(it)
    if has_init:
        ki_ref = next(it)
        vi_ref = next(it)
        convi_ref = next(it)
        ssmi_ref = next(it)
    h_ref = next(it)
    ko_ref = next(it)
    vo_ref = next(it)
    convo_ref = next(it)
    ssmo_ref = next(it)
    hn_s = next(it)
    q_s = next(it)
    kv_s = next(it)
    xbc_s = next(it)
    xact_s = next(it)
    dt_s = next(it)
    oatt_s = next(it)
    y_s = next(it)
    ht_s = next(it)
    gate_s = next(it)

    t = pl.program_id(1)
    seq_rows = n_chunk * CHUNK

    @pl.when(t == 0)
    def _init():
        if has_init:
            kv_s[:, 0:WINDOW, :KV_WIDTH] = ki_ref[...]
            kv_s[:, 0:WINDOW, KV_WIDTH:] = vi_ref[...]
            xbc_s[:, :, 0:HIST_ROWS, :] = jnp.zeros((XBC_BLOCKS, n_seq, HIST_ROWS, MXU_WIDTH), F32)
            for cb in range(XBC_BLOCKS):
                xbc_s[cb, :, HIST_ROWS - (CONV_W - 1):HIST_ROWS, :] = convi_ref[
                    :, :, cb * MXU_WIDTH:(cb + 1) * MXU_WIDTH]
            for g in range(n_seq):
                ht_s[g] = ssmi_ref[g].T
        else:
            kv_s[:, 0:WINDOW, :] = jnp.zeros((n_seq, WINDOW, 2 * KV_WIDTH), F32)
            xbc_s[:, :, 0:HIST_ROWS, :] = jnp.zeros((XBC_BLOCKS, n_seq, HIST_ROWS, MXU_WIDTH), F32)
            ht_s[...] = jnp.zeros(ht_s.shape, F32)

    rows = n_seq * seq_rows
    hn = _rmsnorm(x_ref[...].reshape(rows, D_MODEL), gmix_ref[...]).astype(BF16)
    hn_s[...] = hn

    def project_block(cb):
        c0 = OFF_XBC + cb * MXU_WIDTH
        blk = _dot(hn, win_ref[c0 // MXU_WIDTH])
        for g in range(n_seq):
            xbc_s[cb, g, HIST_ROWS:HIST_ROWS + seq_rows, :] = blk[g * seq_rows:(g + 1) * seq_rows]

    def conv_block(cb):
        csl = slice(cb * MXU_WIDTH, (cb + 1) * MXU_WIDTH)
        for g in range(n_seq):
            conv = convb_ref[:, csl] + convw_ref[CONV_W - 1:CONV_W, csl] * xbc_s[cb, g, HIST_ROWS:HIST_ROWS + seq_rows, :]
            for i in range(1, CONV_W):
                conv = conv + (convw_ref[CONV_W - 1 - i:CONV_W - i, csl]
                               * xbc_s[cb, g, HIST_ROWS - i:HIST_ROWS - i + seq_rows, :])
            xact_s[cb, g * seq_rows:(g + 1) * seq_rows, :] = _silu(conv)

    project_block(0)
    for cb in range(1, XBC_BLOCKS):
        project_block(cb)
        conv_block(cb - 1)
    for c0 in range(OFF_Q, OFF_KV, MXU_WIDTH):
        q_s[:, c0:c0 + MXU_WIDTH] = _dot(hn, win_ref[c0 // MXU_WIDTH]) * (HEAD_DIM ** -0.5)
    conv_block(XBC_BLOCKS - 1)
    kv = _dot(hn, win_ref[OFF_KV // MXU_WIDTH])
    for g in range(n_seq):
        kv_s[g, WINDOW:WINDOW + seq_rows, :] = kv[g * seq_rows:(g + 1) * seq_rows]
    dt_s[...] = _dot(hn, win_ref[OFF_DT // MXU_WIDTH])[:, :LANES]

    cols = lax.broadcasted_iota(jnp.int32, (2 * CHUNK, 2 * N_KEYS), 1)
    colmod = ((cols >> 7) << 6) + (cols & (CHUNK - 1))
    tri, expand = _ssd_constants(n_seq)

    slab = GATE_WIDTH // GATE_SLABS
    slabs_per_iter = GATE_SLABS // n_chunk

    def iter_body(j, carry):
        k0 = pl.multiple_of(j * CHUNK, CHUNK) if n_chunk > 1 else 0
        per_slab = slab // MXU_WIDTH
        n_pieces = slabs_per_iter * per_slab
        n_points = FILL_POINTS * n_seq
        calls = [0]

        first = FILL_FIRST_PHASE * n_seq
        span = n_points - first

        def fill():
            k = calls[0] - first
            calls[0] += 1
            if k < 0:
                return
            for p in range(-(-k * n_pieces // span), -(-(k + 1) * n_pieces // span)):
                s = j * slabs_per_iter + p // per_slab
                c0 = (p % per_slab) * MXU_WIDTH
                gate_s[s, :, c0:c0 + MXU_WIDTH] = _dot(hn_s[...], wgate_ref[s * per_slab + c0 // MXU_WIDTH])

        if has_init:
            valid = colmod < WINDOW + nvalid
        else:
            first_valid = jnp.maximum(0, (2 - (t * n_chunk + j)) * CHUNK)
            valid = colmod >= first_valid
        chunks = [{"g": g, "rows": pl.ds(pl.multiple_of(g * seq_rows + k0, CHUNK), CHUNK)}
                  for g in range(n_seq)]
        for c in chunks:
            kvwin = kv_s[c["g"], pl.ds(k0, N_KEYS), :]
            c["vwin"] = kvwin[:, KV_WIDTH:]
            q = q_s[c["rows"], :]
            c["logits"] = [_attention_logits(q, kvwin[:, :KV_WIDTH], bias_ref, hg) for hg in range(N_KV_HEADS)]
            fill()
        for c in chunks:
            c["xbc_act"] = jnp.concatenate([xact_s[cb, c["rows"], :] for cb in range(XBC_BLOCKS)], axis=1)
            fill()
        for c in chunks:
            outs = [_attention_finish(c["logits"][hg], c["vwin"], sinks_ref, valid, hg)
                    for hg in range(N_KV_HEADS)]
            oatt_s[c["rows"], :] = jnp.concatenate(outs, axis=1)
            del c["logits"], c["vwin"]
            fill()
        _ssd_prepare(chunks, [dt_s[c["rows"], :] for c in chunks], dtb_ref, a_ref, nvalid, tri, expand)
        for _ in chunks:
            fill()
        for c in chunks:
            _ssd_decay(c)
            fill()
        for gq in range(SSD_GROUPS):
            for c in chunks:
                _ssd_group(c, gq, ht_s, c["g"])
                fill()
        for c in chunks:
            y_s[c["rows"], :] = jnp.concatenate(c["ys"], axis=1) + c["xs"] * dskip_ref[...]
        assert calls[0] == n_points
        return carry

    for j in range(n_chunk):
        iter_body(j, 0)

    def gate_cols(lo, hi):
        pieces = []
        while lo < hi:
            s, off = divmod(lo, slab)
            take = min(hi - lo, slab - off)
            pieces.append(gate_s[s, :, off:off + take])
            lo += take
        return jnp.concatenate(pieces, axis=1)

    y_ssd = _rmsnorm(y_s[...] * _silu(gate_cols(GATE_Z, GATE_A)), gssd_ref[...]).astype(BF16)
    merged = (_sigmoid(gate_cols(GATE_A, GATE_S)) * _dot(oatt_s[...].astype(BF16), watt_ref[...])
              + _sigmoid(gate_cols(GATE_S, GATE_WIDTH)) * _dot(y_ssd, wssd_ref[...]))
    h = x_ref[...].reshape(rows, D_MODEL) + _dot(merged.astype(BF16), wout_ref[...])
    h_ref[...] = h.reshape(n_seq, seq_rows, D_MODEL)

    @pl.when(t == n_tiles - 1)
    def _emit_states():
        ko_ref[...] = kv_s[:, nvalid:nvalid + WINDOW, :KV_WIDTH]
        vo_ref[...] = kv_s[:, nvalid:nvalid + WINDOW, KV_WIDTH:]
        last = HIST_ROWS + nvalid
        convo_ref[...] = jnp.concatenate(
            [xbc_s[cb, :, last - (CONV_W - 1):last, :] for cb in range(XBC_BLOCKS)], axis=-1)
        for g in range(n_seq):
            ssmo_ref[g] = ht_s[g].T

    if n_tiles > 1:
        kv_s[:, 0:WINDOW, :] = kv_s[:, seq_rows:seq_rows + WINDOW, :]
        xbc_s[:, :, 0:HIST_ROWS, :] = xbc_s[:, :, seq_rows:seq_rows + HIST_ROWS, :]


def _layer_spec(shape, layer):
    nd = len(shape)
    return pl.BlockSpec((None,) + tuple(shape), lambda *_, _l=layer, _nd=nd: (_l,) + (0,) * _nd,
                        pipeline_mode=pl.Buffered(1))


def _mixer(x3d, bias, wts, layer, init, *, n_seq, n_chunk, nvalid):
    batch, seq_pad, _ = x3d.shape
    seq_rows = n_chunk * CHUNK
    n_tiles = seq_pad // seq_rows
    assert batch % n_seq == 0 and seq_pad % seq_rows == 0 and GATE_SLABS % n_chunk == 0
    rows = n_seq * seq_rows
    has_init = init is not None
    last_valid = nvalid - (n_tiles - 1) * seq_rows

    tile_map = lambda b, t: (b, t, 0)
    seq_map = lambda b, t: (b, 0, 0)
    cache_map = lambda b, t: (layer, b, 0, 0)
    spec = functools.partial(_layer_spec, layer=layer)
    in_specs = [
        pl.BlockSpec((n_seq, seq_rows, D_MODEL), tile_map),
        pl.BlockSpec(bias.shape, lambda b, t: (0, 0, 0), pipeline_mode=pl.Buffered(1)),
        pl.BlockSpec(memory_space=pltpu.SMEM),
        spec((1, D_MODEL)),
        spec((PACKED_BLOCKS, D_MODEL, MXU_WIDTH)),
        spec((GATE_BLOCKS, D_MODEL, MXU_WIDTH)),
        spec((CONV_W, CONV_DIM)),
        spec((1, CONV_DIM)),
        spec((1, LANES)),
        spec((1, LANES)),
        spec((1, SSD_INNER)),
        spec((1, SSD_INNER)),
        spec((ATT_WIDTH, D_MODEL)),
        spec((SSD_INNER, D_MODEL)),
        spec((D_MODEL, D_MODEL)),
    ]
    args = [x3d, bias, wts["sinks"], wts["g_mix"], wts["w_in"], wts["w_gates"], wts["conv_w"], wts["conv_b"],
            wts["dt_bias"], wts["a"], wts["d_skip"], wts["g_ssd"], wts["w_att_out"], wts["w_ssd_out"],
            wts["w_out"]]
    if has_init:
        in_specs += [
            pl.BlockSpec((None, n_seq, WINDOW, KV_WIDTH), cache_map),
            pl.BlockSpec((None, n_seq, WINDOW, KV_WIDTH), cache_map),
            pl.BlockSpec((None, n_seq, CONV_W - 1, CONV_DIM), cache_map),
            pl.BlockSpec((None, n_seq, SSD_INNER, SSD_STATE), cache_map),
        ]
        args += list(init)
    out_shape = [
        jax.ShapeDtypeStruct((batch, seq_pad, D_MODEL), F32),
        jax.ShapeDtypeStruct((batch, WINDOW, KV_WIDTH), F32),
        jax.ShapeDtypeStruct((batch, WINDOW, KV_WIDTH), F32),
        jax.ShapeDtypeStruct((batch, CONV_W - 1, CONV_DIM), F32),
        jax.ShapeDtypeStruct((batch, SSD_INNER, SSD_STATE), F32),
    ]
    out_specs = [
        pl.BlockSpec((n_seq, seq_rows, D_MODEL), tile_map),
        pl.BlockSpec((n_seq, WINDOW, KV_WIDTH), seq_map),
        pl.BlockSpec((n_seq, WINDOW, KV_WIDTH), seq_map),
        pl.BlockSpec((n_seq, CONV_W - 1, CONV_DIM), seq_map),
        pl.BlockSpec((n_seq, SSD_INNER, SSD_STATE), seq_map),
    ]
    scratch = [
        pltpu.VMEM((rows, D_MODEL), BF16),
        pltpu.VMEM((rows, ATT_WIDTH), F32),
        pltpu.VMEM((n_seq, WINDOW + seq_rows, 2 * KV_WIDTH), F32),
        pltpu.VMEM((XBC_BLOCKS, n_seq, HIST_ROWS + seq_rows, MXU_WIDTH), F32),
        pltpu.VMEM((XBC_BLOCKS, rows, MXU_WIDTH), F32),
        pltpu.VMEM((rows, LANES), F32),
        pltpu.VMEM((rows, ATT_WIDTH), F32),
        pltpu.VMEM((rows, SSD_INNER), F32),
        pltpu.VMEM((n_seq, SSD_STATE, SSD_INNER), F32),
        pltpu.VMEM((GATE_SLABS, rows, GATE_WIDTH // GATE_SLABS), F32),
    ]
    kern = functools.partial(_mixer_kernel, layer=layer, n_seq=n_seq, n_chunk=n_chunk, n_tiles=n_tiles,
                             nvalid=last_valid, has_init=has_init)
    return pl.pallas_call(
        kern,
        grid=(batch // n_seq, n_tiles),
        in_specs=in_specs,
        out_specs=out_specs,
        out_shape=out_shape,
        scratch_shapes=scratch,
        compiler_params=pltpu.CompilerParams(
            dimension_semantics=("arbitrary", "arbitrary"), vmem_limit_bytes=VMEM_LIMIT_BYTES),
        name="mixer_init" if has_init else "mixer",
    )(*args)


def _ffn_kernel(h_ref, g_ref, wg_ref, wu_ref, wd_ref, gfin_ref, o_ref, *, final):
    h = h_ref[...]
    hf = _rmsnorm(h, g_ref[...]).astype(BF16)
    act = (_silu(_dot(hf, wg_ref[...])) * _dot(hf, wu_ref[...])).astype(BF16)
    out = h + _dot(act, wd_ref[...])
    if final:
        out = _rmsnorm(out, gfin_ref[...])
    o_ref[...] = out


def _ffn(h2d, wts, layer, g_final, *, block_rows, final):
    n_rows = h2d.shape[0]
    assert n_rows % block_rows == 0
    spec = functools.partial(_layer_spec, layer=layer)
    return pl.pallas_call(
        functools.partial(_ffn_kernel, final=final),
        grid=(n_rows // block_rows,),
        in_specs=[
            pl.BlockSpec((block_rows, D_MODEL), lambda i: (i, 0)),
            spec((1, D_MODEL)),
            spec((D_MODEL, D_FF)),
            spec((D_MODEL, D_FF)),
            spec((D_FF, D_MODEL)),
            pl.BlockSpec((1, D_MODEL), lambda i: (0, 0), pipeline_mode=pl.Buffered(1)),
        ],
        out_specs=pl.BlockSpec((block_rows, D_MODEL), lambda i: (i, 0)),
        out_shape=jax.ShapeDtypeStruct((n_rows, D_MODEL), F32),
        compiler_params=pltpu.CompilerParams(
            dimension_semantics=("arbitrary",), vmem_limit_bytes=VMEM_LIMIT_BYTES),
        name="ffn_final" if final else "ffn",
    )(h2d, wts["g_ffn"], wts["w_gate"], wts["w_up"], wts["w_down"], g_final)


def _pack_plan():
    o_z = ATT_WIDTH + 2 * KV_WIDTH
    o_xbc = o_z + SSD_INNER
    o_dt = o_xbc + CONV_DIM
    o_ga = o_dt + SSD_HEADS
    o_gs = o_ga + D_MODEL
    w = MXU_WIDTH
    steps = [(c, c // w, -1) for c in range(0, o_z, w)]
    steps += [(o_xbc + c, (OFF_XBC + c) // w, -1) for c in range(0, CONV_DIM, w)]
    steps += [(o_dt, OFF_DT // w, -1)]
    for dst, src, width in ((GATE_Z, o_z, SSD_INNER), (GATE_A, o_ga, D_MODEL), (GATE_S, o_gs, D_MODEL)):
        steps += [(src + c, -1, (dst + c) // w) for c in range(0, width, w)]
    return steps


def _pack_w_in_kernel(src_ref, pblk_ref, gblk_ref, w_ref, packed_ref, gates_ref, *, n_packed):
    del src_ref, pblk_ref, gblk_ref
    i = pl.program_id(1)
    t = w_ref[0].T.astype(BF16)

    @pl.when(i < n_packed - 1)
    def _():
        packed_ref[...] = t

    @pl.when(i == n_packed - 1)
    def _():
        lane = lax.broadcasted_iota(jnp.int32, t.shape, 1)
        packed_ref[...] = jnp.where(lane < SSD_HEADS, t, jnp.zeros_like(t))

    @pl.when(i >= n_packed)
    def _():
        gates_ref[...] = t


def _pack_w_in(w_in):
    depth, d, _ = w_in.shape
    plan = _pack_plan()
    n_packed = sum(1 for _, p, _ in plan if p >= 0)
    assert all(s % SUBLANES == 0 for s, _, _ in plan)
    src = jnp.asarray([s // SUBLANES for s, _, _ in plan], jnp.int32)
    pblk = jnp.asarray([p if p >= 0 else plan[n_packed - 1][1] for _, p, _ in plan], jnp.int32)
    gblk = jnp.asarray([g if g >= 0 else plan[n_packed][2] for _, _, g in plan], jnp.int32)
    grid_spec = pltpu.PrefetchScalarGridSpec(
        num_scalar_prefetch=3,
        grid=(depth, len(plan)),
        in_specs=[pl.BlockSpec((pl.Element(1), pl.Element(MXU_WIDTH), pl.Element(d)),
                               lambda l, i, s, p, g: (l, s[i] * SUBLANES, 0))],
        out_specs=[pl.BlockSpec((None, None, d, MXU_WIDTH), lambda l, i, s, p, g: (l, p[i], 0, 0)),
                   pl.BlockSpec((None, None, d, MXU_WIDTH), lambda l, i, s, p, g: (l, g[i], 0, 0))],
    )
    return pl.pallas_call(
        functools.partial(_pack_w_in_kernel, n_packed=n_packed),
        grid_spec=grid_spec,
        out_shape=[jax.ShapeDtypeStruct((depth, PACKED_BLOCKS, d, MXU_WIDTH), BF16),
                   jax.ShapeDtypeStruct((depth, GATE_BLOCKS, d, MXU_WIDTH), BF16)],
        compiler_params=pltpu.CompilerParams(dimension_semantics=("arbitrary", "arbitrary")),
        name="pack_w_in",
    )(src, pblk, gblk, jnp.swapaxes(w_in, 1, 2))


def _cast_kernel(w_ref, o_ref):
    o_ref[...] = w_ref[...].astype(BF16)


def _cast_bf16(w):
    depth, r, c = w.shape
    n_rows = depth * r
    block_rows = PREP_ROWS
    while (n_rows % (2 * block_rows) == 0 and n_rows // (2 * block_rows) >= CAST_MIN_STEPS
           and 2 * block_rows * c * 4 <= CAST_BLOCK_BYTES):
        block_rows *= 2
    assert n_rows % block_rows == 0
    out = pl.pallas_call(
        _cast_kernel,
        grid=(n_rows // block_rows,),
        in_specs=[pl.BlockSpec((block_rows, c), lambda i: (i, 0))],
        out_specs=pl.BlockSpec((block_rows, c), lambda i: (i, 0)),
        out_shape=jax.ShapeDtypeStruct((n_rows, c), BF16),
        compiler_params=pltpu.CompilerParams(dimension_semantics=("arbitrary",),
                                             vmem_limit_bytes=VMEM_LIMIT_BYTES),
        name="cast_bf16",
    )(w.reshape(n_rows, c))
    return out.reshape(depth, r, c)


def _prepare_weights(g_mix, w_in, conv_w, conv_b, dt_bias, a_log, d_skip, g_ssd, sinks, w_att_out, w_ssd_out,
                     w_out, g_ffn, w_gate, w_up, w_down):
    pad_h = ((0, 0), (0, LANES - SSD_HEADS))
    w_packed, w_gates = _pack_w_in(w_in)
    return {
        "sinks": sinks,
        "g_mix": g_mix[:, None, :],
        "w_in": w_packed,
        "w_gates": w_gates,
        "conv_w": conv_w,
        "conv_b": conv_b[:, None, :],
        "dt_bias": jnp.pad(dt_bias, pad_h)[:, None, :],
        "a": jnp.pad(-jnp.exp(a_log), pad_h)[:, None, :],
        "d_skip": jnp.repeat(d_skip, SSD_HEAD_DIM, axis=1)[:, None, :],
        "g_ssd": g_ssd[:, None, :],
        "w_att_out": _cast_bf16(w_att_out),
        "w_ssd_out": _cast_bf16(w_ssd_out),
        "w_out": _cast_bf16(w_out),
        "g_ffn": g_ffn[:, None, :],
        "w_gate": _cast_bf16(w_gate),
        "w_up": _cast_bf16(w_up),
        "w_down": _cast_bf16(w_down),
    }


PREP_ROWS = 256
CAST_BLOCK_BYTES = 8 * 1024 * 1024
CAST_MIN_STEPS = 4
PROMPT_SEQS_PER_TILE = 4
PROMPT_CHUNKS_PER_TILE = 2
SAMPLE_SEQS_PER_TILE = 4
FFN_BLOCK_ROWS = 512


def kernel(x_prompt, x_sample, cache_k, cache_v, state_conv, state_ssm, rel_table, g_mix, w_in, conv_w, conv_b, dt_bias, a_log, d_skip, g_ssd, sinks, w_att_out, w_ssd_out, w_out, g_ffn, w_gate, w_up, w_down, g_final):
    depth = w_in.shape[0]
    bp, sp, _ = x_prompt.shape
    bs, ts, _ = x_sample.shape
    kv_len = cache_k.shape[2]
    assert kv_len == WINDOW and ts <= CHUNK and ts % 8 == 0 and ts >= CONV_W - 1

    bias = _blocked_bias(rel_table)
    g_fin = g_final[None, :]

    wts = _prepare_weights(g_mix, w_in, conv_w, conv_b, dt_bias, a_log, d_skip, g_ssd, sinks, w_att_out,
                           w_ssd_out, w_out, g_ffn, w_gate, w_up, w_down)
    init = (cache_k.reshape(depth, bs, WINDOW, KV_WIDTH), cache_v.reshape(depth, bs, WINDOW, KV_WIDTH),
            state_conv, state_ssm.reshape(depth, bs, SSD_INNER, SSD_STATE))

    xp = x_prompt
    xs = x_sample.reshape(bs * ts, D_MODEL)
    st_p, st_s = [], []
    for l in range(depth):
        final = l == depth - 1
        hp, *state_p = _mixer(xp, bias, wts, l, None, n_seq=PROMPT_SEQS_PER_TILE,
                              n_chunk=PROMPT_CHUNKS_PER_TILE, nvalid=sp)
        xp = _ffn(hp.reshape(bp * sp, D_MODEL), wts, l, g_fin, block_rows=FFN_BLOCK_ROWS,
                  final=final).reshape(bp, sp, D_MODEL)
        st_p.append(state_p)
        xs_pad = jnp.pad(xs.reshape(bs, ts, D_MODEL), ((0, 0), (0, CHUNK - ts), (0, 0)))
        hs, *state_s = _mixer(xs_pad, bias, wts, l, init, n_seq=SAMPLE_SEQS_PER_TILE, n_chunk=1, nvalid=ts)
        xs = _ffn(hs[:, :ts].reshape(bs * ts, D_MODEL), wts, l, g_fin, block_rows=bs * ts, final=final)
        st_s.append(state_s)

    def states(sts, b):
        k, v, conv, ssm = (jnp.stack(leaves) for leaves in zip(*sts))
        return (k.reshape(depth, b, WINDOW, N_KV_HEADS, HEAD_DIM),
                v.reshape(depth, b, WINDOW, N_KV_HEADS, HEAD_DIM), conv,
                ssm.reshape(depth, b, SSD_HEADS, SSD_HEAD_DIM, SSD_STATE))

    return (xp, xs.reshape(x_sample.shape), *states(st_p, bp), *states(st_s, bs))
```

```python
import functools
import math

import jax
import jax.numpy as jnp
from jax import lax
from jax.experimental import pallas as pl
from jax.experimental.pallas import tpu as pltpu

D_MODEL = 1024
CHUNK = 64
EPS = 1e-6
NEG_INF = -1e30
N_HEADS = 8
N_KV_HEADS = 2
HEAD_DIM = 64
ATT_WIDTH = N_HEADS * HEAD_DIM
KV_WIDTH = N_KV_HEADS * HEAD_DIM
WINDOW = 128
N_KEYS = WINDOW + CHUNK
NUM_BUCKETS = 32
MAX_DISTANCE = 128
SSD_INNER = 1024
SSD_HEADS = 16
SSD_HEAD_DIM = 64
SSD_GROUPS = 2
SSD_STATE = 128
GROUP_WIDTH = SSD_INNER // SSD_GROUPS
CONV_W = 4
CONV_DIM = SSD_INNER + 2 * SSD_GROUPS * SSD_STATE
D_FF = 2816
LANES = 128
SUBLANES = 8
HIST_ROWS = 8
MXU_WIDTH = 256
XBC_BLOCKS = CONV_DIM // MXU_WIDTH
FILL_POINTS = 7
FILL_FIRST_PHASE = 3

OFF_Q = 0
OFF_KV = OFF_Q + ATT_WIDTH
OFF_XBC = OFF_KV + 2 * KV_WIDTH
OFF_DT = OFF_XBC + CONV_DIM
IN_PACKED = OFF_DT + LANES
GATE_Z = 0
GATE_A = GATE_Z + SSD_INNER
GATE_S = GATE_A + D_MODEL
GATE_WIDTH = GATE_S + D_MODEL
GATE_SLABS = 2
PACKED_BLOCKS = -(-IN_PACKED // MXU_WIDTH)
GATE_BLOCKS = GATE_WIDTH // MXU_WIDTH

VMEM_LIMIT_BYTES = 60 * 1024 * 1024

F32 = jnp.float32
BF16 = jnp.bfloat16


def _dot(a, b):
    return jnp.dot(a, b, preferred_element_type=F32)


def _dot_nt(a, b):
    return lax.dot_general(a, b, (((1,), (1,)), ((), ())), preferred_element_type=F32)


def _split3(x):
    hi = x.astype(BF16)
    r1 = x - hi.astype(F32)
    mid = r1.astype(BF16)
    lo = (r1 - mid.astype(F32)).astype(BF16)
    return hi, mid, lo


def _rmsnorm(x, g):
    return x * lax.rsqrt(jnp.mean(x * x, axis=-1, keepdims=True) + EPS) * g


def _sigmoid(x):
    return 1.0 / (1.0 + jnp.exp(-x))


def _silu(x):
    return x * _sigmoid(x)


def _softplus(x):
    return jnp.maximum(x, 0.0) + jnp.log1p(jnp.exp(-jnp.abs(x)))


def _bias_kernel(table_ref, bucket_ref, o_ref):
    bucket = bucket_ref[...]
    rows = lax.broadcasted_iota(jnp.int32, bucket.shape, 0)
    cols = lax.broadcasted_iota(jnp.int32, bucket.shape, 1)
    row_hi = rows >= CHUNK
    col_hi = (cols & HEAD_DIM) != 0
    for g in range(N_KV_HEADS):
        acc = jnp.zeros(bucket.shape, F32)
        for b in range(NUM_BUCKETS):
            t0 = table_ref[b, 4 * g + 0]
            t1 = table_ref[b, 4 * g + 1]
            t2 = table_ref[b, 4 * g + 2]
            t3 = table_ref[b, 4 * g + 3]
            tv = jnp.where(row_hi, jnp.where(col_hi, t3, t2), jnp.where(col_hi, t1, t0))
            acc = jnp.where(bucket == b, tv, acc)
        o_ref[g] = acc


def _t5_bucket(rel):
    nb = NUM_BUCKETS // 2
    max_exact = nb // 2
    ret = jnp.where(rel > 0, nb, 0)
    n = jnp.abs(rel)
    nf = jnp.maximum(n, 1).astype(jnp.float32)
    large = max_exact + (jnp.log(nf / max_exact) / math.log(MAX_DISTANCE / max_exact)
                         * (nb - max_exact)).astype(jnp.int32)
    large = jnp.minimum(large, nb - 1)
    return ret + jnp.where(n < max_exact, n, large)


def _blocked_bias(rel_table):
    qi = jnp.arange(CHUNK, dtype=jnp.int32)
    kj = jnp.arange(N_KEYS, dtype=jnp.int32)
    bucket = _t5_bucket(kj[None, :] - WINDOW - qi[:, None]).astype(jnp.int32)
    bucket = jnp.tile(bucket.reshape(CHUNK, N_KEYS // CHUNK, 1, CHUNK), (2, 1, 2, 1)).reshape(2 * CHUNK, 2 * N_KEYS)
    return pl.pallas_call(
        _bias_kernel,
        out_shape=jax.ShapeDtypeStruct((N_KV_HEADS, 2 * CHUNK, 2 * N_KEYS), F32),
        in_specs=[pl.BlockSpec(memory_space=pltpu.SMEM),
                  pl.BlockSpec(memory_space=pltpu.VMEM)],
        out_specs=pl.BlockSpec(memory_space=pltpu.VMEM),
        name="rel_bias",
    )(rel_table, bucket)


def _blocked_kv(win, g):
    lo_half = lax.broadcasted_iota(jnp.int32, (N_KEYS, LANES), 1) < HEAD_DIM
    if g == 0:
        a0 = jnp.where(lo_half, win, 0.0)
        a1 = pltpu.roll(a0, HEAD_DIM, axis=1)
    else:
        a1 = jnp.where(lo_half, 0.0, win)
        a0 = pltpu.roll(a1, HEAD_DIM, axis=1)
    pieces = []
    for r in range(0, N_KEYS, CHUNK):
        pieces += [a0[r:r + CHUNK], a1[r:r + CHUNK]]
    return jnp.concatenate(pieces, axis=0).astype(BF16)


def _attention_logits(q, kwin, bias_ref, g):
    qs = jnp.concatenate([q[:, 256 * g:256 * g + LANES],
                          q[:, 256 * g + LANES:256 * (g + 1)]], axis=0).astype(BF16)
    return _dot_nt(qs, _blocked_kv(kwin, g)) + bias_ref[g]


def _attention_finish(logits, vwin, sinks_ref, valid, g):
    row_lo = lax.broadcasted_iota(jnp.int32, (2 * CHUNK, 1), 0) < CHUNK
    even = lax.broadcasted_iota(jnp.int32, (2 * CHUNK, LANES), 1) < HEAD_DIM
    logits = jnp.where(valid, logits, NEG_INF)
    tiles = [logits[:, c:c + LANES] for c in range(0, 2 * N_KEYS, LANES)]
    s0 = jnp.where(row_lo, sinks_ref[4 * g + 0], sinks_ref[4 * g + 2])
    s1 = jnp.where(row_lo, sinks_ref[4 * g + 1], sinks_ref[4 * g + 3])
    tmax = functools.reduce(jnp.maximum, tiles)
    m0 = jnp.maximum(jnp.max(jnp.where(even, tmax, -jnp.inf), axis=1, keepdims=True), s0)
    m1 = jnp.maximum(jnp.max(jnp.where(even, -jnp.inf, tmax), axis=1, keepdims=True), s1)
    m = jnp.where(even, m0, m1)
    p = [jnp.exp(tile - m) for tile in tiles]
    psum = functools.reduce(jnp.add, p)
    d0 = jnp.sum(jnp.where(even, psum, 0.0), axis=1, keepdims=True) + jnp.exp(s0 - m0)
    d1 = jnp.sum(jnp.where(even, 0.0, psum), axis=1, keepdims=True) + jnp.exp(s1 - m1)
    o = _dot(jnp.concatenate(p, axis=1).astype(BF16), _blocked_kv(vwin, g)) / jnp.where(even, d0, d1)
    return jnp.concatenate([o[:CHUNK], o[CHUNK:]], axis=1)


def _ssd_constants(n_blocks):
    r = lax.broadcasted_iota(jnp.int32, (n_blocks * CHUNK, n_blocks * CHUNK), 0)
    c = lax.broadcasted_iota(jnp.int32, (n_blocks * CHUNK, n_blocks * CHUNK), 1)
    tri = jnp.where((c <= r) & ((c >> 6) == (r >> 6)), 1.0, 0.0).astype(BF16)
    hrow = lax.broadcasted_iota(jnp.int32, (LANES, SSD_INNER), 0)
    hcol = lax.broadcasted_iota(jnp.int32, (LANES, SSD_INNER), 1)
    expand = jnp.where(hcol >> 6 == hrow, 1.0, 0.0).astype(BF16)
    return tri, expand


def _ssd_prepare(chunks, dt_raws, dtb_ref, a_ref, nvalid, tri, expand):
    n = len(chunks)
    dt = _softplus(jnp.concatenate(dt_raws, axis=0) + dtb_ref[...])
    if nvalid < CHUNK:
        rows = lax.broadcasted_iota(jnp.int32, dt.shape, 0) & (CHUNK - 1)
        dt = jnp.where(rows < nvalid, dt, 0.0)
    acum3 = _dot(tri, jnp.concatenate(_split3(dt * a_ref[...]), axis=1))
    acum = acum3[:, :LANES] + acum3[:, LANES:2 * LANES] + acum3[:, 2 * LANES:]
    a_hi, a_mid, _ = _split3(acum)
    wide = _dot(jnp.concatenate([dt.astype(BF16), a_hi, a_mid], axis=0), expand)
    m = n * CHUNK
    for i, c in enumerate(chunks):
        sl = slice(i * CHUNK, (i + 1) * CHUNK)
        c["dt_exp"] = wide[sl]
        c["a_col"] = wide[m + i * CHUNK:m + (i + 1) * CHUNK] + wide[2 * m + i * CHUNK:2 * m + (i + 1) * CHUNK]


def _ssd_decay(c):
    a_col = c["a_col"]
    a_last = a_col[CHUNK - 1:CHUNK, :]
    c["xs"] = c["xbc_act"][:, :SSD_INNER]
    c["xdt"] = c["xs"] * c.pop("dt_exp")
    c["xw"] = c["xdt"] * jnp.exp(a_last - a_col)
    c["e_col"] = jnp.exp(a_col)
    c["e_last"] = jnp.exp(a_last)


def _ssd_group(c, gq, ht_ref, g_idx):
    l2 = lax.broadcasted_iota(jnp.int32, (CHUNK, LANES), 0)
    j2 = lax.broadcasted_iota(jnp.int32, (CHUNK, LANES), 1)
    s2 = j2 & (CHUNK - 1)
    diag_sel = jnp.where(s2 == l2, 1.0, 0.0)
    causal2 = s2 <= l2
    lane_lo = j2 < SSD_HEAD_DIM

    b0 = SSD_INNER + SSD_STATE * gq
    c0 = SSD_INNER + SSD_GROUPS * SSD_STATE + SSD_STATE * gq
    bg = c["xbc_act"][:, b0:b0 + SSD_STATE]
    cg = c["xbc_act"][:, c0:c0 + SSD_STATE].astype(BF16)
    bg_bf = bg.astype(BF16)
    cb2 = _dot_nt(cg, jnp.concatenate([bg_bf, bg_bf], axis=0))
    gsl = slice(GROUP_WIDTH * gq, GROUP_WIDTH * (gq + 1))
    h_prev = ht_ref[g_idx, :, gsl]
    y_off = _dot(cg, h_prev.astype(BF16)) * c["e_col"][:, gsl]
    yd = []
    for i in range(GROUP_WIDTH // LANES):
        psl = slice(GROUP_WIDTH * gq + LANES * i, GROUP_WIDTH * gq + LANES * (i + 1))
        ac = c["a_col"][:, psl]
        a_row = jnp.sum(ac * diag_sel, axis=0, keepdims=True)
        lmat = jnp.exp(jnp.where(causal2, ac - a_row, -jnp.inf))
        m2 = (cb2 * lmat).astype(BF16)
        xp = c["xdt"][:, psl]
        xblk = jnp.concatenate([jnp.where(lane_lo, xp, 0.0), jnp.where(lane_lo, 0.0, xp)],
                               axis=0).astype(BF16)
        yd.append(_dot(m2, xblk))
    c.setdefault("ys", []).append(jnp.concatenate(yd, axis=1) + y_off)
    st = _dot(bg.T.astype(BF16), c["xw"][:, gsl].astype(BF16))
    ht_ref[g_idx, :, gsl] = h_prev * c["e_last"][:, gsl] + st


class _LayerSinks:
    def __init__(self, ref, layer):
        self.ref, self.layer = ref, layer

    def __getitem__(self, head):
        return self.ref[self.layer, head]


def _mixer_kernel(*refs, layer, n_seq, n_chunk, n_tiles, nvalid, has_init):
    it = iter(refs)
    x_ref = next(it)
    bias_ref = next(it)
    sinks_ref = _LayerSinks(next(it), layer)
    gmix_ref = next(it)
    win_ref = next(it)
    wgate_ref = next(it)
    convw_ref = next(it)
    convb_ref = next(it)
    dtb_ref = next(it)
    a_ref = next(it)
    dskip_ref = next(it)
    gssd_ref = next(it)
    watt_ref = next(it)
    wssd_ref = next(it)
    wout_ref = next(it)
    if has_init:
        ki_ref = next(it)
        vi_ref = next(it)
        convi_ref = next(it)
        ssmi_ref = next(it)
    h_ref = next(it)
    ko_ref = next(it)
    vo_ref = next(it)
    convo_ref = next(it)
    ssmo_ref = next(it)
    hn_s = next(it)
    q_s = next(it)
    kv_s = next(it)
    xbc_s = next(it)
    xact_s = next(it)
    dt_s = next(it)
    oatt_s = next(it)
    y_s = next(it)
    ht_s = next(it)
    gate_s = next(it)

    t = pl.program_id(1)
    seq_rows = n_chunk * CHUNK

    @pl.when(t == 0)
    def _init():
        if has_init:
            kv_s[:, 0:WINDOW, :KV_WIDTH] = ki_ref[...]
            kv_s[:, 0:WINDOW, KV_WIDTH:] = vi_ref[...]
            xbc_s[:, :, 0:HIST_ROWS, :] = jnp.zeros((XBC_BLOCKS, n_seq, HIST_ROWS, MXU_WIDTH), F32)
            for cb in range(XBC_BLOCKS):
                xbc_s[cb, :, HIST_ROWS - (CONV_W - 1):HIST_ROWS, :] = convi_ref[
                    :, :, cb * MXU_WIDTH:(cb + 1) * MXU_WIDTH]
            for g in range(n_seq):
                ht_s[g] = ssmi_ref[g].T
        else:
            kv_s[:, 0:WINDOW, :] = jnp.zeros((n_seq, WINDOW, 2 * KV_WIDTH), F32)
            xbc_s[:, :, 0:HIST_ROWS, :] = jnp.zeros((XBC_BLOCKS, n_seq, HIST_ROWS, MXU_WIDTH), F32)
            ht_s[...] = jnp.zeros(ht_s.shape, F32)

    rows = n_seq * seq_rows
    hn = _rmsnorm(x_ref[...].reshape(rows, D_MODEL), gmix_ref[...]).astype(BF16)
    hn_s[...] = hn

    def project_block(cb):
        c0 = OFF_XBC + cb * MXU_WIDTH
        blk = _dot(hn, win_ref[c0 // MXU_WIDTH])
        for g in range(n_seq):
            xbc_s[cb, g, HIST_ROWS:HIST_ROWS + seq_rows, :] = blk[g * seq_rows:(g + 1) * seq_rows]

    def conv_block(cb):
        csl = slice(cb * MXU_WIDTH, (cb + 1) * MXU_WIDTH)
        for g in range(n_seq):
            conv = convb_ref[:, csl] + convw_ref[CONV_W - 1:CONV_W, csl] * xbc_s[cb, g, HIST_ROWS:HIST_ROWS + seq_rows, :]
            for i in range(1, CONV_W):
                conv = conv + (convw_ref[CONV_W - 1 - i:CONV_W - i, csl]
                               * xbc_s[cb, g, HIST_ROWS - i:HIST_ROWS - i + seq_rows, :])
            xact_s[cb, g * seq_rows:(g + 1) * seq_rows, :] = _silu(conv)

    project_block(0)
    for cb in range(1, XBC_BLOCKS):
        project_block(cb)
        conv_block(cb - 1)
    for c0 in range(OFF_Q, OFF_KV, MXU_WIDTH):
        q_s[:, c0:c0 + MXU_WIDTH] = _dot(hn, win_ref[c0 // MXU_WIDTH]) * (HEAD_DIM ** -0.5)
    conv_block(XBC_BLOCKS - 1)
    kv = _dot(hn, win_ref[OFF_KV // MXU_WIDTH])
    for g in range(n_seq):
        kv_s[g, WINDOW:WINDOW + seq_rows, :] = kv[g * seq_rows:(g + 1) * seq_rows]
    dt_s[...] = _dot(hn, win_ref[OFF_DT // MXU_WIDTH])[:, :LANES]

    cols = lax.broadcasted_iota(jnp.int32, (2 * CHUNK, 2 * N_KEYS), 1)
    colmod = ((cols >> 7) << 6) + (cols & (CHUNK - 1))
    tri, expand = _ssd_constants(n_seq)

    slab = GATE_WIDTH // GATE_SLABS
    slabs_per_iter = GATE_SLABS // n_chunk

    def iter_body(j, carry):
        k0 = pl.multiple_of(j * CHUNK, CHUNK) if n_chunk > 1 else 0
        per_slab = slab // MXU_WIDTH
        n_pieces = slabs_per_iter * per_slab
        n_points = FILL_POINTS * n_seq
        calls = [0]

        first = FILL_FIRST_PHASE * n_seq
        span = n_points - first

        def fill():
            k = calls[0] - first
            calls[0] += 1
            if k < 0:
                return
            for p in range(-(-k * n_pieces // span), -(-(k + 1) * n_pieces // span)):
                s = j * slabs_per_iter + p // per_slab
                c0 = (p % per_slab) * MXU_WIDTH
                gate_s[s, :, c0:c0 + MXU_WIDTH] = _dot(hn_s[...], wgate_ref[s * per_slab + c0 // MXU_WIDTH])

        if has_init:
            valid = colmod < WINDOW + nvalid
        else:
            first_valid = jnp.maximum(0, (2 - (t * n_chunk + j)) * CHUNK)
            valid = colmod >= first_valid
        chunks = [{"g": g, "rows": pl.ds(pl.multiple_of(g * seq_rows + k0, CHUNK), CHUNK)}
                  for g in range(n_seq)]
        for c in chunks:
            kvwin = kv_s[c["g"], pl.ds(k0, N_KEYS), :]
            c["vwin"] = kvwin[:, KV_WIDTH:]
            q = q_s[c["rows"], :]
            c["logits"] = [_attention_logits(q, kvwin[:, :KV_WIDTH], bias_ref, hg) for hg in range(N_KV_HEADS)]
            fill()
        for c in chunks:
            c["xbc_act"] = jnp.concatenate([xact_s[cb, c["rows"], :] for cb in range(XBC_BLOCKS)], axis=1)
            fill()
        for c in chunks:
            outs = [_attention_finish(c["logits"][hg], c["vwin"], sinks_ref, valid, hg)
                    for hg in range(N_KV_HEADS)]
            oatt_s[c["rows"], :] = jnp.concatenate(outs, axis=1)
            del c["logits"], c["vwin"]
            fill()
        _ssd_prepare(chunks, [dt_s[c["rows"], :] for c in chunks], dtb_ref, a_ref, nvalid, tri, expand)
        for _ in chunks:
            fill()
        for c in chunks:
            _ssd_decay(c)
            fill()
        for gq in range(SSD_GROUPS):
            for c in chunks:
                _ssd_group(c, gq, ht_s, c["g"])
                fill()
        for c in chunks:
            y_s[c["rows"], :] = jnp.concatenate(c["ys"], axis=1) + c["xs"] * dskip_ref[...]
        assert calls[0] == n_points
        return carry

    for j in range(n_chunk):
        iter_body(j, 0)

    def gate_cols(lo, hi):
        pieces = []
        while lo < hi:
            s, off = divmod(lo, slab)
            take = min(hi - lo, slab - off)
            pieces.append(gate_s[s, :, off:off + take])
            lo += take
        return jnp.concatenate(pieces, axis=1)

    y_ssd = _rmsnorm(y_s[...] * _silu(gate_cols(GATE_Z, GATE_A)), gssd_ref[...]).astype(BF16)
    merged = (_sigmoid(gate_cols(GATE_A, GATE_S)) * _dot(oatt_s[...].astype(BF16), watt_ref[...])
              + _sigmoid(gate_cols(GATE_S, GATE_WIDTH)) * _dot(y_ssd, wssd_ref[...]))
    h = x_ref[...].reshape(rows, D_MODEL) + _dot(merged.astype(BF16), wout_ref[...])
    h_ref[...] = h.reshape(n_seq, seq_rows, D_MODEL)

    @pl.when(t == n_tiles - 1)
    def _emit_states():
        ko_ref[...] = kv_s[:, nvalid:nvalid + WINDOW, :KV_WIDTH]
        vo_ref[...] = kv_s[:, nvalid:nvalid + WINDOW, KV_WIDTH:]
        last = HIST_ROWS + nvalid
        convo_ref[...] = jnp.concatenate(
            [xbc_s[cb, :, last - (CONV_W - 1):last, :] for cb in range(XBC_BLOCKS)], axis=-1)
        for g in range(n_seq):
            ssmo_ref[g] = ht_s[g].T

    if n_tiles > 1:
        kv_s[:, 0:WINDOW, :] = kv_s[:, seq_rows:seq_rows + WINDOW, :]
        xbc_s[:, :, 0:HIST_ROWS, :] = xbc_s[:, :, seq_rows:seq_rows + HIST_ROWS, :]


def _layer_spec(shape, layer):
    nd = len(shape)
    return pl.BlockSpec((None,) + tuple(shape), lambda *_, _l=layer, _nd=nd: (_l,) + (0,) * _nd,
                        pipeline_mode=pl.Buffered(1))


def _mixer(x3d, bias, wts, layer, init, *, n_seq, n_chunk, nvalid):
    batch, seq_pad, _ = x3d.shape
    seq_rows = n_chunk * CHUNK
    n_tiles = seq_pad // seq_rows
    assert batch % n_seq == 0 and seq_pad % seq_rows == 0 and GATE_SLABS % n_chunk == 0
    rows = n_seq * seq_rows
    has_init = init is not None
    last_valid = nvalid - (n_tiles - 1) * seq_rows

    tile_map = lambda b, t: (b, t, 0)
    seq_map = lambda b, t: (b, 0, 0)
    cache_map = lambda b, t: (layer, b, 0, 0)
    spec = functools.partial(_layer_spec, layer=layer)
    in_specs = [
        pl.BlockSpec((n_seq, seq_rows, D_MODEL), tile_map),
        pl.BlockSpec(bias.shape, lambda b, t: (0, 0, 0), pipeline_mode=pl.Buffered(1)),
        pl.BlockSpec(memory_space=pltpu.SMEM),
        spec((1, D_MODEL)),
        spec((PACKED_BLOCKS, D_MODEL, MXU_WIDTH)),
        spec((GATE_BLOCKS, D_MODEL, MXU_WIDTH)),
        spec((CONV_W, CONV_DIM)),
        spec((1, CONV_DIM)),
        spec((1, LANES)),
        spec((1, LANES)),
        spec((1, SSD_INNER)),
        spec((1, SSD_INNER)),
        spec((ATT_WIDTH, D_MODEL)),
        spec((SSD_INNER, D_MODEL)),
        spec((D_MODEL, D_MODEL)),
    ]
    args = [x3d, bias, wts["sinks"], wts["g_mix"], wts["w_in"], wts["w_gates"], wts["conv_w"], wts["conv_b"],
            wts["dt_bias"], wts["a"], wts["d_skip"], wts["g_ssd"], wts["w_att_out"], wts["w_ssd_out"],
            wts["w_out"]]
    if has_init:
        in_specs += [
            pl.BlockSpec((None, n_seq, WINDOW, KV_WIDTH), cache_map),
            pl.BlockSpec((None, n_seq, WINDOW, KV_WIDTH), cache_map),
            pl.BlockSpec((None, n_seq, CONV_W - 1, CONV_DIM), cache_map),
            pl.BlockSpec((None, n_seq, SSD_INNER, SSD_STATE), cache_map),
        ]
        args += list(init)
    out_shape = [
        jax.ShapeDtypeStruct((batch, seq_pad, D_MODEL), F32),
        jax.ShapeDtypeStruct((batch, WINDOW, KV_WIDTH), F32),
        jax.ShapeDtypeStruct((batch, WINDOW, KV_WIDTH), F32),
        jax.ShapeDtypeStruct((batch, CONV_W - 1, CONV_DIM), F32),
        jax.ShapeDtypeStruct((batch, SSD_INNER, SSD_STATE), F32),
    ]
    out_specs = [
        pl.BlockSpec((n_seq, seq_rows, D_MODEL), tile_map),
        pl.BlockSpec((n_seq, WINDOW, KV_WIDTH), seq_map),
        pl.BlockSpec((n_seq, WINDOW, KV_WIDTH), seq_map),
        pl.BlockSpec((n_seq, CONV_W - 1, CONV_DIM), seq_map),
        pl.BlockSpec((n_seq, SSD_INNER, SSD_STATE), seq_map),
    ]
    scratch = [
        pltpu.VMEM((rows, D_MODEL), BF16),
        pltpu.VMEM((rows, ATT_WIDTH), F32),
        pltpu.VMEM((n_seq, WINDOW + seq_rows, 2 * KV_WIDTH), F32),
        pltpu.VMEM((XBC_BLOCKS, n_seq, HIST_ROWS + seq_rows, MXU_WIDTH), F32),
        pltpu.VMEM((XBC_BLOCKS, rows, MXU_WIDTH), F32),
        pltpu.VMEM((rows, LANES), F32),
        pltpu.VMEM((rows, ATT_WIDTH), F32),
        pltpu.VMEM((rows, SSD_INNER), F32),
        pltpu.VMEM((n_seq, SSD_STATE, SSD_INNER), F32),
        pltpu.VMEM((GATE_SLABS, rows, GATE_WIDTH // GATE_SLABS), F32),
    ]
    kern = functools.partial(_mixer_kernel, layer=layer, n_seq=n_seq, n_chunk=n_chunk, n_tiles=n_tiles,
                             nvalid=last_valid, has_init=has_init)
    return pl.pallas_call(
        kern,
        grid=(batch // n_seq, n_tiles),
        in_specs=in_specs,
        out_specs=out_specs,
        out_shape=out_shape,
        scratch_shapes=scratch,
        compiler_params=pltpu.CompilerParams(
            dimension_semantics=("arbitrary", "arbitrary"), vmem_limit_bytes=VMEM_LIMIT_BYTES),
        name="mixer_init" if has_init else "mixer",
    )(*args)


def _ffn_kernel(h_ref, g_ref, wg_ref, wu_ref, wd_ref, gfin_ref, o_ref, *, final):
    h = h_ref[...]
    hf = _rmsnorm(h, g_ref[...]).astype(BF16)
    act = (_silu(_dot(hf, wg_ref[...])) * _dot(hf, wu_ref[...])).astype(BF16)
    out = h + _dot(act, wd_ref[...])
    if final:
        out = _rmsnorm(out, gfin_ref[...])
    o_ref[...] = out


def _ffn(h2d, wts, layer, g_final, *, block_rows, final):
    n_rows = h2d.shape[0]
    assert n_rows % block_rows == 0
    spec = functools.partial(_layer_spec, layer=layer)
    return pl.pallas_call(
        functools.partial(_ffn_kernel, final=final),
        grid=(n_rows // block_rows,),
        in_specs=[
            pl.BlockSpec((block_rows, D_MODEL), lambda i: (i, 0)),
            spec((1, D_MODEL)),
            spec((D_MODEL, D_FF)),
            spec((D_MODEL, D_FF)),
            spec((D_FF, D_MODEL)),
            pl.BlockSpec((1, D_MODEL), lambda i: (0, 0), pipeline_mode=pl.Buffered(1)),
        ],
        out_specs=pl.BlockSpec((block_rows, D_MODEL), lambda i: (i, 0)),
        out_shape=jax.ShapeDtypeStruct((n_rows, D_MODEL), F32),
        compiler_params=pltpu.CompilerParams(
            dimension_semantics=("arbitrary",), vmem_limit_bytes=VMEM_LIMIT_BYTES),
        name="ffn_final" if final else "ffn",
    )(h2d, wts["g_ffn"], wts["w_gate"], wts["w_up"], wts["w_down"], g_final)


def _pack_plan():
    o_z = ATT_WIDTH + 2 * KV_WIDTH
    o_xbc = o_z + SSD_INNER
    o_dt = o_xbc + CONV_DIM
    o_ga = o_dt + SSD_HEADS
    o_gs = o_ga + D_MODEL
    w = MXU_WIDTH
    steps = [(c, c // w, -1) for c in range(0, o_z, w)]
    steps += [(o_xbc + c, (OFF_XBC + c) // w, -1) for c in range(0, CONV_DIM, w)]
    steps += [(o_dt, OFF_DT // w, -1)]
    for dst, src, width in ((GATE_Z, o_z, SSD_INNER), (GATE_A, o_ga, D_MODEL), (GATE_S, o_gs, D_MODEL)):
        steps += [(src + c, -1, (dst + c) // w) for c in range(0, width, w)]
    return steps


def _pack_w_in_kernel(src_ref, pblk_ref, gblk_ref, w_ref, packed_ref, gates_ref, *, n_packed):
    del src_ref, pblk_ref, gblk_ref
    i = pl.program_id(1)
    t = w_ref[0].T.astype(BF16)

    @pl.when(i < n_packed - 1)
    def _():
        packed_ref[...] = t

    @pl.when(i == n_packed - 1)
    def _():
        lane = lax.broadcasted_iota(jnp.int32, t.shape, 1)
        packed_ref[...] = jnp.where(lane < SSD_HEADS, t, jnp.zeros_like(t))

    @pl.when(i >= n_packed)
    def _():
        gates_ref[...] = t


def _pack_w_in(w_in):
    depth, d, _ = w_in.shape
    plan = _pack_plan()
    n_packed = sum(1 for _, p, _ in plan if p >= 0)
    assert all(s % SUBLANES == 0 for s, _, _ in plan)
    src = jnp.asarray([s // SUBLANES for s, _, _ in plan], jnp.int32)
    pblk = jnp.asarray([p if p >= 0 else plan[n_packed - 1][1] for _, p, _ in plan], jnp.int32)
    gblk = jnp.asarray([g if g >= 0 else plan[n_packed][2] for _, _, g in plan], jnp.int32)
    grid_spec = pltpu.PrefetchScalarGridSpec(
        num_scalar_prefetch=3,
        grid=(depth, len(plan)),
        in_specs=[pl.BlockSpec((pl.Element(1), pl.Element(MXU_WIDTH), pl.Element(d)),
                               lambda l, i, s, p, g: (l, s[i] * SUBLANES, 0))],
        out_specs=[pl.BlockSpec((None, None, d, MXU_WIDTH), lambda l, i, s, p, g: (l, p[i], 0, 0)),
                   pl.BlockSpec((None, None, d, MXU_WIDTH), lambda l, i, s, p, g: (l, g[i], 0, 0))],
    )
    return pl.pallas_call(
        functools.partial(_pack_w_in_kernel, n_packed=n_packed),
        grid_spec=grid_spec,
        out_shape=[jax.ShapeDtypeStruct((depth, PACKED_BLOCKS, d, MXU_WIDTH), BF16),
                   jax.ShapeDtypeStruct((depth, GATE_BLOCKS, d, MXU_WIDTH), BF16)],
        compiler_params=pltpu.CompilerParams(dimension_semantics=("arbitrary", "arbitrary")),
        name="pack_w_in",
    )(src, pblk, gblk, jnp.swapaxes(w_in, 1, 2))


def _cast_kernel(w_ref, o_ref):
    o_ref[...] = w_ref[...].astype(BF16)


def _cast_bf16(w):
    depth, r, c = w.shape
    n_rows = depth * r
    block_rows = PREP_ROWS
    while (n_rows % (2 * block_rows) == 0 and n_rows // (2 * block_rows) >= CAST_MIN_STEPS
           and 2 * block_rows * c * 4 <= CAST_BLOCK_BYTES):
        block_rows *= 2
    assert n_rows % block_rows == 0
    out = pl.pallas_call(
        _cast_kernel,
        grid=(n_rows // block_rows,),
        in_specs=[pl.BlockSpec((block_rows, c), lambda i: (i, 0))],
        out_specs=pl.BlockSpec((block_rows, c), lambda i: (i, 0)),
        out_shape=jax.ShapeDtypeStruct((n_rows, c), BF16),
        compiler_params=pltpu.CompilerParams(dimension_semantics=("arbitrary",),
                                             vmem_limit_bytes=VMEM_LIMIT_BYTES),
        name="cast_bf16",
    )(w.reshape(n_rows, c))
    return out.reshape(depth, r, c)


def _prepare_weights(g_mix, w_in, conv_w, conv_b, dt_bias, a_log, d_skip, g_ssd, sinks, w_att_out, w_ssd_out,
                     w_out, g_ffn, w_gate, w_up, w_down):
    pad_h = ((0, 0), (0, LANES - SSD_HEADS))
    w_packed, w_gates = _pack_w_in(w_in)
    return {
        "sinks": sinks,
        "g_mix": g_mix[:, None, :],
        "w_in": w_packed,
        "w_gates": w_gates,
        "conv_w": conv_w,
        "conv_b": conv_b[:, None, :],
        "dt_bias": jnp.pad(dt_bias, pad_h)[:, None, :],
        "a": jnp.pad(-jnp.exp(a_log), pad_h)[:, None, :],
        "d_skip": jnp.repeat(d_skip, SSD_HEAD_DIM, axis=1)[:, None, :],
        "g_ssd": g_ssd[:, None, :],
        "w_att_out": _cast_bf16(w_att_out),
        "w_ssd_out": _cast_bf16(w_ssd_out),
        "w_out": _cast_bf16(w_out),
        "g_ffn": g_ffn[:, None, :],
        "w_gate": _cast_bf16(w_gate),
        "w_up": _cast_bf16(w_up),
        "w_down": _cast_bf16(w_down),
    }


PREP_ROWS = 256
CAST_BLOCK_BYTES = 8 * 1024 * 1024
CAST_MIN_STEPS = 4
PROMPT_SEQS_PER_TILE = 4
PROMPT_CHUNKS_PER_TILE = 2
SAMPLE_SEQS_PER_TILE = 4
FFN_BLOCK_ROWS = 512


def kernel(x_prompt, x_sample, cache_k, cache_v, state_conv, state_ssm, rel_table, g_mix, w_in, conv_w, conv_b, dt_bias, a_log, d_skip, g_ssd, sinks, w_att_out, w_ssd_out, w_out, g_ffn, w_gate, w_up, w_down, g_final):
    depth = w_in.shape[0]
    bp, sp, _ = x_prompt.shape
    bs, ts, _ = x_sample.shape
    kv_len = cache_k.shape[2]
    assert kv_len == WINDOW and ts <= CHUNK and ts % 8 == 0 and ts >= CONV_W - 1

    bias = _blocked_bias(rel_table)
    g_fin = g_final[None, :]

    wts = _prepare_weights(g_mix, w_in, conv_w, conv_b, dt_bias, a_log, d_skip, g_ssd, sinks, w_att_out,
                           w_ssd_out, w_out, g_ffn, w_gate, w_up, w_down)
    init = (cache_k.reshape(depth, bs, WINDOW, KV_WIDTH), cache_v.reshape(depth, bs, WINDOW, KV_WIDTH),
            state_conv, state_ssm.reshape(depth, bs, SSD_INNER, SSD_STATE))

    xp = x_prompt
    xs = x_sample.reshape(bs * ts, D_MODEL)
    st_p, st_s = [], []
    for l in range(depth):
        final = l == depth - 1
        hp, *state_p = _mixer(xp, bias, wts, l, None, n_seq=PROMPT_SEQS_PER_TILE,
                              n_chunk=PROMPT_CHUNKS_PER_TILE, nvalid=sp)
        xp = _ffn(hp.reshape(bp * sp, D_MODEL), wts, l, g_fin, block_rows=FFN_BLOCK_ROWS,
                  final=final).reshape(bp, sp, D_MODEL)
        st_p.append(state_p)
        xs_pad = jnp.pad(xs.reshape(bs, ts, D_MODEL), ((0, 0), (0, CHUNK - ts), (0, 0)))
        hs, *state_s = _mixer(xs_pad, bias, wts, l, init, n_seq=SAMPLE_SEQS_PER_TILE, n_chunk=1, nvalid=ts)
        xs = _ffn(hs[:, :ts].reshape(bs * ts, D_MODEL), wts, l, g_fin, block_rows=bs * ts, final=final)
        st_s.append(state_s)

    def states(sts, b):
        k, v, conv, ssm = (jnp.stack(leaves) for leaves in zip(*sts))
        return (k.reshape(depth, b, WINDOW, N_KV_HEADS, HEAD_DIM),
                v.reshape(depth, b, WINDOW, N_KV_HEADS, HEAD_DIM), conv,
                ssm.reshape(depth, b, SSD_HEADS, SSD_HEAD_DIM, SSD_STATE))

    return (xp, xs.reshape(x_sample.shape), *states(st_p, bp), *states(st_s, bs))
```

```python
import functools
import math

import jax
import jax.numpy as jnp
from jax import lax
from jax.experimental import pallas as pl
from jax.experimental.pallas import tpu as pltpu

D_MODEL = 1024
CHUNK = 64
EPS = 1e-6
NEG_INF = -1e30
N_HEADS = 8
N_KV_HEADS = 2
HEAD_DIM = 64
ATT_WIDTH = N_HEADS * HEAD_DIM
KV_WIDTH = N_KV_HEADS * HEAD_DIM
WINDOW = 128
N_KEYS = WINDOW + CHUNK
NUM_BUCKETS = 32
MAX_DISTANCE = 128
SSD_INNER = 1024
SSD_HEADS = 16
SSD_HEAD_DIM = 64
SSD_GROUPS = 2
SSD_STATE = 128
GROUP_WIDTH = SSD_INNER // SSD_GROUPS
CONV_W = 4
CONV_DIM = SSD_INNER + 2 * SSD_GROUPS * SSD_STATE
D_FF = 2816
LANES = 128
SUBLANES = 8
HIST_ROWS = 8
MXU_WIDTH = 256
XBC_BLOCKS = CONV_DIM // MXU_WIDTH
FILL_POINTS = 7
FILL_FIRST_PHASE = 3

OFF_Q = 0
OFF_KV = OFF_Q + ATT_WIDTH
OFF_XBC = OFF_KV + 2 * KV_WIDTH
OFF_DT = OFF_XBC + CONV_DIM
IN_PACKED = OFF_DT + LANES
GATE_Z = 0
GATE_A = GATE_Z + SSD_INNER
GATE_S = GATE_A + D_MODEL
GATE_WIDTH = GATE_S + D_MODEL
GATE_SLABS = 2
PACKED_BLOCKS = -(-IN_PACKED // MXU_WIDTH)
GATE_BLOCKS = GATE_WIDTH // MXU_WIDTH

VMEM_LIMIT_BYTES = 60 * 1024 * 1024

F32 = jnp.float32
BF16 = jnp.bfloat16


def _dot(a, b):
    return jnp.dot(a, b, preferred_element_type=F32)


def _dot_nt(a, b):
    return lax.dot_general(a, b, (((1,), (1,)), ((), ())), preferred_element_type=F32)


def _split3(x):
    hi = x.astype(BF16)
    r1 = x - hi.astype(F32)
    mid = r1.astype(BF16)
    lo = (r1 - mid.astype(F32)).astype(BF16)
    return hi, mid, lo


def _rmsnorm(x, g):
    return x * lax.rsqrt(jnp.mean(x * x, axis=-1, keepdims=True) + EPS) * g


def _sigmoid(x):
    return 1.0 / (1.0 + jnp.exp(-x))


def _silu(x):
    return x * _sigmoid(x)


def _softplus(x):
    return jnp.maximum(x, 0.0) + jnp.log1p(jnp.exp(-jnp.abs(x)))


def _bias_kernel(table_ref, bucket_ref, o_ref):
    bucket = bucket_ref[...]
    rows = lax.broadcasted_iota(jnp.int32, bucket.shape, 0)
    cols = lax.broadcasted_iota(jnp.int32, bucket.shape, 1)
    row_hi = rows >= CHUNK
    col_hi = (cols & HEAD_DIM) != 0
    for g in range(N_KV_HEADS):
        acc = jnp.zeros(bucket.shape, F32)
        for b in range(NUM_BUCKETS):
            t0 = table_ref[b, 4 * g + 0]
            t1 = table_ref[b, 4 * g + 1]
            t2 = table_ref[b, 4 * g + 2]
            t3 = table_ref[b, 4 * g + 3]
            tv = jnp.where(row_hi, jnp.where(col_hi, t3, t2), jnp.where(col_hi, t1, t0))
            acc = jnp.where(bucket == b, tv, acc)
        o_ref[g] = acc


def _t5_bucket(rel):
    nb = NUM_BUCKETS // 2
    max_exact = nb // 2
    ret = jnp.where(rel > 0, nb, 0)
    n = jnp.abs(rel)
    nf = jnp.maximum(n, 1).astype(jnp.float32)
    large = max_exact + (jnp.log(nf / max_exact) / math.log(MAX_DISTANCE / max_exact)
                         * (nb - max_exact)).astype(jnp.int32)
    large = jnp.minimum(large, nb - 1)
    return ret + jnp.where(n < max_exact, n, large)


def _blocked_bias(rel_table):
    qi = jnp.arange(CHUNK, dtype=jnp.int32)
    kj = jnp.arange(N_KEYS, dtype=jnp.int32)
    bucket = _t5_bucket(kj[None, :] - WINDOW - qi[:, None]).astype(jnp.int32)
    bucket = jnp.tile(bucket.reshape(CHUNK, N_KEYS // CHUNK, 1, CHUNK), (2, 1, 2, 1)).reshape(2 * CHUNK, 2 * N_KEYS)
    return pl.pallas_call(
        _bias_kernel,
        out_shape=jax.ShapeDtypeStruct((N_KV_HEADS, 2 * CHUNK, 2 * N_KEYS), F32),
        in_specs=[pl.BlockSpec(memory_space=pltpu.SMEM),
                  pl.BlockSpec(memory_space=pltpu.VMEM)],
        out_specs=pl.BlockSpec(memory_space=pltpu.VMEM),
        name="rel_bias",
    )(rel_table, bucket)


def _blocked_kv(win, g):
    lo_half = lax.broadcasted_iota(jnp.int32, (N_KEYS, LANES), 1) < HEAD_DIM
    if g == 0:
        a0 = jnp.where(lo_half, win, 0.0)
        a1 = pltpu.roll(a0, HEAD_DIM, axis=1)
    else:
        a1 = jnp.where(lo_half, 0.0, win)
        a0 = pltpu.roll(a1, HEAD_DIM, axis=1)
    pieces = []
    for r in range(0, N_KEYS, CHUNK):
        pieces += [a0[r:r + CHUNK], a1[r:r + CHUNK]]
    return jnp.concatenate(pieces, axis=0).astype(BF16)


def _attention_logits(q, kwin, bias_ref, g):
    qs = jnp.concatenate([q[:, 256 * g:256 * g + LANES],
                          q[:, 256 * g + LANES:256 * (g + 1)]], axis=0).astype(BF16)
    return _dot_nt(qs, _blocked_kv(kwin, g)) + bias_ref[g]


def _attention_finish(logits, vwin, sinks_ref, valid, g):
    row_lo = lax.broadcasted_iota(jnp.int32, (2 * CHUNK, 1), 0) < CHUNK
    even = lax.broadcasted_iota(jnp.int32, (2 * CHUNK, LANES), 1) < HEAD_DIM
    logits = jnp.where(valid, logits, NEG_INF)
    tiles = [logits[:, c:c + LANES] for c in range(0, 2 * N_KEYS, LANES)]
    s0 = jnp.where(row_lo, sinks_ref[4 * g + 0], sinks_ref[4 * g + 2])
    s1 = jnp.where(row_lo, sinks_ref[4 * g + 1], sinks_ref[4 * g + 3])
    tmax = functools.reduce(jnp.maximum, tiles)
    m0 = jnp.maximum(jnp.max(jnp.where(even, tmax, -jnp.inf), axis=1, keepdims=True), s0)
    m1 = jnp.maximum(jnp.max(jnp.where(even, -jnp.inf, tmax), axis=1, keepdims=True), s1)
    m = jnp.where(even, m0, m1)
    p = [jnp.exp(tile - m) for tile in tiles]
    psum = functools.reduce(jnp.add, p)
    d0 = jnp.sum(jnp.where(even, psum, 0.0), axis=1, keepdims=True) + jnp.exp(s0 - m0)
    d1 = jnp.sum(jnp.where(even, 0.0, psum), axis=1, keepdims=True) + jnp.exp(s1 - m1)
    o = _dot(jnp.concatenate(p, axis=1).astype(BF16), _blocked_kv(vwin, g)) / jnp.where(even, d0, d1)
    return jnp.concatenate([o[:CHUNK], o[CHUNK:]], axis=1)


def _ssd_constants(n_blocks):
    r = lax.broadcasted_iota(jnp.int32, (n_blocks * CHUNK, n_blocks * CHUNK), 0)
    c = lax.broadcasted_iota(jnp.int32, (n_blocks * CHUNK, n_blocks * CHUNK), 1)
    tri = jnp.where((c <= r) & ((c >> 6) == (r >> 6)), 1.0, 0.0).astype(BF16)
    hrow = lax.broadcasted_iota(jnp.int32, (LANES, SSD_INNER), 0)
    hcol = lax.broadcasted_iota(jnp.int32, (LANES, SSD_INNER), 1)
    expand = jnp.where(hcol >> 6 == hrow, 1.0, 0.0).astype(BF16)
    return tri, expand


def _ssd_prepare(chunks, dt_raws, dtb_ref, a_ref, nvalid, tri, expand):
    n = len(chunks)
    dt = _softplus(jnp.concatenate(dt_raws, axis=0) + dtb_ref[...])
    if nvalid < CHUNK:
        rows = lax.broadcasted_iota(jnp.int32, dt.shape, 0) & (CHUNK - 1)
        dt = jnp.where(rows < nvalid, dt, 0.0)
    acum3 = _dot(tri, jnp.concatenate(_split3(dt * a_ref[...]), axis=1))
    acum = acum3[:, :LANES] + acum3[:, LANES:2 * LANES] + acum3[:, 2 * LANES:]
    a_hi, a_mid, _ = _split3(acum)
    wide = _dot(jnp.concatenate([dt.astype(BF16), a_hi, a_mid], axis=0), expand)
    m = n * CHUNK
    for i, c in enumerate(chunks):
        sl = slice(i * CHUNK, (i + 1) * CHUNK)
        c["dt_exp"] = wide[sl]
        c["a_col"] = wide[m + i * CHUNK:m + (i + 1) * CHUNK] + wide[2 * m + i * CHUNK:2 * m + (i + 1) * CHUNK]


def _ssd_decay(c):
    a_col = c["a_col"]
    a_last = a_col[CHUNK - 1:CHUNK, :]
    c["xs"] = c["xbc_act"][:, :SSD_INNER]
    c["xdt"] = c["xs"] * c.pop("dt_exp")
    c["xw"] = c["xdt"] * jnp.exp(a_last - a_col)
    c["e_col"] = jnp.exp(a_col)
    c["e_last"] = jnp.exp(a_last)


def _ssd_group(c, gq, ht_ref, g_idx):
    l2 = lax.broadcasted_iota(jnp.int32, (CHUNK, LANES), 0)
    j2 = lax.broadcasted_iota(jnp.int32, (CHUNK, LANES), 1)
    s2 = j2 & (CHUNK - 1)
    diag_sel = jnp.where(s2 == l2, 1.0, 0.0)
    causal2 = s2 <= l2
    lane_lo = j2 < SSD_HEAD_DIM

    b0 = SSD_INNER + SSD_STATE * gq
    c0 = SSD_INNER + SSD_GROUPS * SSD_STATE + SSD_STATE * gq
    bg = c["xbc_act"][:, b0:b0 + SSD_STATE]
    cg = c["xbc_act"][:, c0:c0 + SSD_STATE].astype(BF16)
    bg_bf = bg.astype(BF16)
    cb2 = _dot_nt(cg, jnp.concatenate([bg_bf, bg_bf], axis=0))
    gsl = slice(GROUP_WIDTH * gq, GROUP_WIDTH * (gq + 1))
    h_prev = ht_ref[g_idx, :, gsl]
    y_off = _dot(cg, h_prev.astype(BF16)) * c["e_col"][:, gsl]
    yd = []
    for i in range(GROUP_WIDTH // LANES):
        psl = slice(GROUP_WIDTH * gq + LANES * i, GROUP_WIDTH * gq + LANES * (i + 1))
        ac = c["a_col"][:, psl]
        a_row = jnp.sum(ac * diag_sel, axis=0, keepdims=True)
        lmat = jnp.exp(jnp.where(causal2, ac - a_row, -jnp.inf))
        m2 = (cb2 * lmat).astype(BF16)
        xp = c["xdt"][:, psl]
        xblk = jnp.concatenate([jnp.where(lane_lo, xp, 0.0), jnp.where(lane_lo, 0.0, xp)],
                               axis=0).astype(BF16)
        yd.append(_dot(m2, xblk))
    c.setdefault("ys", []).append(jnp.concatenate(yd, axis=1) + y_off)
    st = _dot(bg.T.astype(BF16), c["xw"][:, gsl].astype(BF16))
    ht_ref[g_idx, :, gsl] = h_prev * c["e_last"][:, gsl] + st


class _LayerSinks:
    def __init__(self, ref, layer):
        self.ref, self.layer = ref, layer

    def __getitem__(self, head):
        return self.ref[self.layer, head]


def _mixer_kernel(*refs, layer, n_seq, n_chunk, n_tiles, nvalid, has_init):
    it = iter(refs)
    x_ref = next(it)
    bias_ref = next(it)
    sinks_ref = _LayerSinks(next(it), layer)
    gmix_ref = next(it)
    win_ref = next(it)
    wgate_ref = next(it)
    convw_ref = next(it)
    convb_ref = next(it)
    dtb_ref = next(it)
    a_ref = next(it)
    dskip_ref = next(it)
    gssd_ref = next(it)
    watt_ref = next(it)
    wssd_ref = next(it)
    wout_ref = next(it)
    if has_init:
        ki_ref = next(it)
        vi_ref = next(it)
        convi_ref = next(it)
        ssmi_ref = next(it)
    h_ref = next(it)
    ko_ref = next(it)
    vo_ref = next(it)
    convo_ref = next(it)
    ssmo_ref = next(it)
    hn_s = next(it)
    q_s = next(it)
    kv_s = next(it)
    xbc_s = next(it)
    xact_s = next(it)
    dt_s = next(it)
    oatt_s = next(it)
    y_s = next(it)
    ht_s = next(it)
    gate_s = next(it)

    t = pl.program_id(1)
    seq_rows = n_chunk * CHUNK

    @pl.when(t == 0)
    def _init():
        if has_init:
            kv_s[:, 0:WINDOW, :KV_WIDTH] = ki_ref[...]
            kv_s[:, 0:WINDOW, KV_WIDTH:] = vi_ref[...]
            xbc_s[:, :, 0:HIST_ROWS, :] = jnp.zeros((XBC_BLOCKS, n_seq, HIST_ROWS, MXU_WIDTH), F32)
            for cb in range(XBC_BLOCKS):
                xbc_s[cb, :, HIST_ROWS - (CONV_W - 1):HIST_ROWS, :] = convi_ref[
                    :, :, cb * MXU_WIDTH:(cb + 1) * MXU_WIDTH]
            for g in range(n_seq):
                ht_s[g] = ssmi_ref[g].T
        else:
            kv_s[:, 0:WINDOW, :] = jnp.zeros((n_seq, WINDOW, 2 * KV_WIDTH), F32)
            xbc_s[:, :, 0:HIST_ROWS, :] = jnp.zeros((XBC_BLOCKS, n_seq, HIST_ROWS, MXU_WIDTH), F32)
            ht_s[...] = jnp.zeros(ht_s.shape, F32)

    rows = n_seq * seq_rows
    hn = _rmsnorm(x_ref[...].reshape(rows, D_MODEL), gmix_ref[...]).astype(BF16)
    hn_s[...] = hn

    def project_block(cb):
        c0 = OFF_XBC + cb * MXU_WIDTH
        blk = _dot(hn, win_ref[c0 // MXU_WIDTH])
        for g in range(n_seq):
            xbc_s[cb, g, HIST_ROWS:HIST_ROWS + seq_rows, :] = blk[g * seq_rows:(g + 1) * seq_rows]

    def conv_block(cb):
        csl = slice(cb * MXU_WIDTH, (cb + 1) * MXU_WIDTH)
        for g in range(n_seq):
            conv = convb_ref[:, csl] + convw_ref[CONV_W - 1:CONV_W, csl] * xbc_s[cb, g, HIST_ROWS:HIST_ROWS + seq_rows, :]
            for i in range(1, CONV_W):
                conv = conv + (convw_ref[CONV_W - 1 - i:CONV_W - i, csl]
                               * xbc_s[cb, g, HIST_ROWS - i:HIST_ROWS - i + seq_rows, :])
            xact_s[cb, g * seq_rows:(g + 1) * seq_rows, :] = _silu(conv)

    project_block(0)
    for cb in range(1, XBC_BLOCKS):
        project_block(cb)
        conv_block(cb - 1)
    for c0 in range(OFF_Q, OFF_KV, MXU_WIDTH):
        q_s[:, c0:c0 + MXU_WIDTH] = _dot(hn, win_ref[c0 // MXU_WIDTH]) * (HEAD_DIM ** -0.5)
    conv_block(XBC_BLOCKS - 1)
    kv = _dot(hn, win_ref[OFF_KV // MXU_WIDTH])
    for g in range(n_seq):
        kv_s[g, WINDOW:WINDOW + seq_rows, :] = kv[g * seq_rows:(g + 1) * seq_rows]
    dt_s[...] = _dot(hn, win_ref[OFF_DT // MXU_WIDTH])[:, :LANES]

    cols = lax.broadcasted_iota(jnp.int32, (2 * CHUNK, 2 * N_KEYS), 1)
    colmod = ((cols >> 7) << 6) + (cols & (CHUNK - 1))
    tri, expand = _ssd_constants(n_seq)

    slab = GATE_WIDTH // GATE_SLABS
    slabs_per_iter = GATE_SLABS // n_chunk

    def iter_body(j, carry):
        k0 = pl.multiple_of(j * CHUNK, CHUNK) if n_chunk > 1 else 0
        per_slab = slab // MXU_WIDTH
        n_pieces = slabs_per_iter * per_slab
        n_points = FILL_POINTS * n_seq
        calls = [0]

        first = FILL_FIRST_PHASE * n_seq
        span = n_points - first

        def fill():
            k = calls[0] - first
            calls[0] += 1
            if k < 0:
                return
            for p in range(-(-k * n_pieces // span), -(-(k + 1) * n_pieces // span)):
                s = j * slabs_per_iter + p // per_slab
                c0 = (p % per_slab) * MXU_WIDTH
                gate_s[s, :, c0:c0 + MXU_WIDTH] = _dot(hn_s[...], wgate_ref[s * per_slab + c0 // MXU_WIDTH])

        if has_init:
            valid = colmod < WINDOW + nvalid
        else:
            first_valid = jnp.maximum(0, (2 - (t * n_chunk + j)) * CHUNK)
            valid = colmod >= first_valid
        chunks = [{"g": g, "rows": pl.ds(pl.multiple_of(g * seq_rows + k0, CHUNK), CHUNK)}
                  for g in range(n_seq)]
        for c in chunks:
            kvwin = kv_s[c["g"], pl.ds(k0, N_KEYS), :]
            c["vwin"] = kvwin[:, KV_WIDTH:]
            q = q_s[c["rows"], :]
            c["logits"] = [_attention_logits(q, kvwin[:, :KV_WIDTH], bias_ref, hg) for hg in range(N_KV_HEADS)]
            fill()
        for c in chunks:
            c["xbc_act"] = jnp.concatenate([xact_s[cb, c["rows"], :] for cb in range(XBC_BLOCKS)], axis=1)
            fill()
        for c in chunks:
            outs = [_attention_finish(c["logits"][hg], c["vwin"], sinks_ref, valid, hg)
                    for hg in range(N_KV_HEADS)]
            oatt_s[c["rows"], :] = jnp.concatenate(outs, axis=1)
            del c["logits"], c["vwin"]
            fill()
        _ssd_prepare(chunks, [dt_s[c["rows"], :] for c in chunks], dtb_ref, a_ref, nvalid, tri, expand)
        for _ in chunks:
            fill()
        for c in chunks:
            _ssd_decay(c)
            fill()
        for gq in range(SSD_GROUPS):
            for c in chunks:
                _ssd_group(c, gq, ht_s, c["g"])
                fill()
        for c in chunks:
            y_s[c["rows"], :] = jnp.concatenate(c["ys"], axis=1) + c["xs"] * dskip_ref[...]
        assert calls[0] == n_points
        return carry

    for j in range(n_chunk):
        iter_body(j, 0)

    def gate_cols(lo, hi):
        pieces = []
        while lo < hi:
            s, off = divmod(lo, slab)
            take = min(hi - lo, slab - off)
            pieces.append(gate_s[s, :, off:off + take])
            lo += take
        return jnp.concatenate(pieces, axis=1)

    y_ssd = _rmsnorm(y_s[...] * _silu(gate_cols(GATE_Z, GATE_A)), gssd_ref[...]).astype(BF16)
    merged = (_sigmoid(gate_cols(GATE_A, GATE_S)) * _dot(oatt_s[...].astype(BF16), watt_ref[...])
              + _sigmoid(gate_cols(GATE_S, GATE_WIDTH)) * _dot(y_ssd, wssd_ref[...]))
    h = x_ref[...].reshape(rows, D_MODEL) + _dot(merged.astype(BF16), wout_ref[...])
    h_ref[...] = h.reshape(n_seq, seq_rows, D_MODEL)

    @pl.when(t == n_tiles - 1)
    def _emit_states():
        ko_ref[...] = kv_s[:, nvalid:nvalid + WINDOW, :KV_WIDTH]
        vo_ref[...] = kv_s[:, nvalid:nvalid + WINDOW, KV_WIDTH:]
        last = HIST_ROWS + nvalid
        convo_ref[...] = jnp.concatenate(
            [xbc_s[cb, :, last - (CONV_W - 1):last, :] for cb in range(XBC_BLOCKS)], axis=-1)
        for g in range(n_seq):
            ssmo_ref[g] = ht_s[g].T

    if n_tiles > 1:
        kv_s[:, 0:WINDOW, :] = kv_s[:, seq_rows:seq_rows + WINDOW, :]
        xbc_s[:, :, 0:HIST_ROWS, :] = xbc_s[:, :, seq_rows:seq_rows + HIST_ROWS, :]


def _layer_spec(shape, layer):
    nd = len(shape)
    return pl.BlockSpec((None,) + tuple(shape), lambda *_, _l=layer, _nd=nd: (_l,) + (0,) * _nd,
                        pipeline_mode=pl.Buffered(1))


def _mixer(x3d, bias, wts, layer, init, *, n_seq, n_chunk, nvalid):
    batch, seq_pad, _ = x3d.shape
    seq_rows = n_chunk * CHUNK
    n_tiles = seq_pad // seq_rows
    assert batch % n_seq == 0 and seq_pad % seq_rows == 0 and GATE_SLABS % n_chunk == 0
    rows = n_seq * seq_rows
    has_init = init is not None
    last_valid = nvalid - (n_tiles - 1) * seq_rows

    tile_map = lambda b, t: (b, t, 0)
    seq_map = lambda b, t: (b, 0, 0)
    cache_map = lambda b, t: (layer, b, 0, 0)
    spec = functools.partial(_layer_spec, layer=layer)
    in_specs = [
        pl.BlockSpec((n_seq, seq_rows, D_MODEL), tile_map),
        pl.BlockSpec(bias.shape, lambda b, t: (0, 0, 0), pipeline_mode=pl.Buffered(1)),
        pl.BlockSpec(memory_space=pltpu.SMEM),
        spec((1, D_MODEL)),
        spec((PACKED_BLOCKS, D_MODEL, MXU_WIDTH)),
        spec((GATE_BLOCKS, D_MODEL, MXU_WIDTH)),
        spec((CONV_W, CONV_DIM)),
        spec((1, CONV_DIM)),
        spec((1, LANES)),
        spec((1, LANES)),
        spec((1, SSD_INNER)),
        spec((1, SSD_INNER)),
        spec((ATT_WIDTH, D_MODEL)),
        spec((SSD_INNER, D_MODEL)),
        spec((D_MODEL, D_MODEL)),
    ]
    args = [x3d, bias, wts["sinks"], wts["g_mix"], wts["w_in"], wts["w_gates"], wts["conv_w"], wts["conv_b"],
            wts["dt_bias"], wts["a"], wts["d_skip"], wts["g_ssd"], wts["w_att_out"], wts["w_ssd_out"],
            wts["w_out"]]
    if has_init:
        in_specs += [
            pl.BlockSpec((None, n_seq, WINDOW, KV_WIDTH), cache_map),
            pl.BlockSpec((None, n_seq, WINDOW, KV_WIDTH), cache_map),
            pl.BlockSpec((None, n_seq, CONV_W - 1, CONV_DIM), cache_map),
            pl.BlockSpec((None, n_seq, SSD_INNER, SSD_STATE), cache_map),
        ]
        args += list(init)
    out_shape = [
        jax.ShapeDtypeStruct((batch, seq_pad, D_MODEL), F32),
        jax.ShapeDtypeStruct((batch, WINDOW, KV_WIDTH), F32),
        jax.ShapeDtypeStruct((batch, WINDOW, KV_WIDTH), F32),
        jax.ShapeDtypeStruct((batch, CONV_W - 1, CONV_DIM), F32),
        jax.ShapeDtypeStruct((batch, SSD_INNER, SSD_STATE), F32),
    ]
    out_specs = [
        pl.BlockSpec((n_seq, seq_rows, D_MODEL), tile_map),
        pl.BlockSpec((n_seq, WINDOW, KV_WIDTH), seq_map),
        pl.BlockSpec((n_seq, WINDOW, KV_WIDTH), seq_map),
        pl.BlockSpec((n_seq, CONV_W - 1, CONV_DIM), seq_map),
        pl.BlockSpec((n_seq, SSD_INNER, SSD_STATE), seq_map),
    ]
    scratch = [
        pltpu.VMEM((rows, D_MODEL), BF16),
        pltpu.VMEM((rows, ATT_WIDTH), F32),
        pltpu.VMEM((n_seq, WINDOW + seq_rows, 2 * KV_WIDTH), F32),
        pltpu.VMEM((XBC_BLOCKS, n_seq, HIST_ROWS + seq_rows, MXU_WIDTH), F32),
        pltpu.VMEM((XBC_BLOCKS, rows, MXU_WIDTH), F32),
        pltpu.VMEM((rows, LANES), F32),
        pltpu.VMEM((rows, ATT_WIDTH), F32),
        pltpu.VMEM((rows, SSD_INNER), F32),
        pltpu.VMEM((n_seq, SSD_STATE, SSD_INNER), F32),
        pltpu.VMEM((GATE_SLABS, rows, GATE_WIDTH // GATE_SLABS), F32),
    ]
    kern = functools.partial(_mixer_kernel, layer=layer, n_seq=n_seq, n_chunk=n_chunk, n_tiles=n_tiles,
                             nvalid=last_valid, has_init=has_init)
    return pl.pallas_call(
        kern,
        grid=(batch // n_seq, n_tiles),
        in_specs=in_specs,
        out_specs=out_specs,
        out_shape=out_shape,
        scratch_shapes=scratch,
        compiler_params=pltpu.CompilerParams(
            dimension_semantics=("arbitrary", "arbitrary"), vmem_limit_bytes=VMEM_LIMIT_BYTES),
        name="mixer_init" if has_init else "mixer",
    )(*args)


def _ffn_kernel(h_ref, g_ref, wg_ref, wu_ref, wd_ref, gfin_ref, o_ref, *, final):
    h = h_ref[...]
    hf = _rmsnorm(h, g_ref[...]).astype(BF16)
    act = (_silu(_dot(hf, wg_ref[...])) * _dot(hf, wu_ref[...])).astype(BF16)
    out = h + _dot(act, wd_ref[...])
    if final:
        out = _rmsnorm(out, gfin_ref[...])
    o_ref[...] = out


def _ffn(h2d, wts, layer, g_final, *, block_rows, final):
    n_rows = h2d.shape[0]
    assert n_rows % block_rows == 0
    spec = functools.partial(_layer_spec, layer=layer)
    return pl.pallas_call(
        functools.partial(_ffn_kernel, final=final),
        grid=(n_rows // block_rows,),
        in_specs=[
            pl.BlockSpec((block_rows, D_MODEL), lambda i: (i, 0)),
            spec((1, D_MODEL)),
            spec((D_MODEL, D_FF)),
            spec((D_MODEL, D_FF)),
            spec((D_FF, D_MODEL)),
            pl.BlockSpec((1, D_MODEL), lambda i: (0, 0), pipeline_mode=pl.Buffered(1)),
        ],
        out_specs=pl.BlockSpec((block_rows, D_MODEL), lambda i: (i, 0)),
        out_shape=jax.ShapeDtypeStruct((n_rows, D_MODEL), F32),
        compiler_params=pltpu.CompilerParams(
            dimension_semantics=("arbitrary",), vmem_limit_bytes=VMEM_LIMIT_BYTES),
        name="ffn_final" if final else "ffn",
    )(h2d, wts["g_ffn"], wts["w_gate"], wts["w_up"], wts["w_down"], g_final)


def _pack_plan():
    o_z = ATT_WIDTH + 2 * KV_WIDTH
    o_xbc = o_z + SSD_INNER
    o_dt = o_xbc + CONV_DIM
    o_ga = o_dt + SSD_HEADS
    o_gs = o_ga + D_MODEL
    w = MXU_WIDTH
    steps = [(c, c // w, -1) for c in range(0, o_z, w)]
    steps += [(o_xbc + c, (OFF_XBC + c) // w, -1) for c in range(0, CONV_DIM, w)]
    steps += [(o_dt, OFF_DT // w, -1)]
    for dst, src, width in ((GATE_Z, o_z, SSD_INNER), (GATE_A, o_ga, D_MODEL), (GATE_S, o_gs, D_MODEL)):
        steps += [(src + c, -1, (dst + c) // w) for c in range(0, width, w)]
    return steps


def _pack_w_in_kernel(src_ref, pblk_ref, gblk_ref, w_ref, packed_ref, gates_ref, *, n_packed):
    del src_ref, pblk_ref, gblk_ref
    i = pl.program_id(1)
    t = w_ref[0].T.astype(BF16)

    @pl.when(i < n_packed - 1)
    def _():
        packed_ref[...] = t

    @pl.when(i == n_packed - 1)
    def _():
        lane = lax.broadcasted_iota(jnp.int32, t.shape, 1)
        packed_ref[...] = jnp.where(lane < SSD_HEADS, t, jnp.zeros_like(t))

    @pl.when(i >= n_packed)
    def _():
        gates_ref[...] = t


def _pack_w_in(w_in):
    depth, d, _ = w_in.shape
    plan = _pack_plan()
    n_packed = sum(1 for _, p, _ in plan if p >= 0)
    assert all(s % SUBLANES == 0 for s, _, _ in plan)
    src = jnp.asarray([s // SUBLANES for s, _, _ in plan], jnp.int32)
    pblk = jnp.asarray([p if p >= 0 else plan[n_packed - 1][1] for _, p, _ in plan], jnp.int32)
    gblk = jnp.asarray([g if g >= 0 else plan[n_packed][2] for _, _, g in plan], jnp.int32)
    grid_spec = pltpu.PrefetchScalarGridSpec(
        num_scalar_prefetch=3,
        grid=(depth, len(plan)),
        in_specs=[pl.BlockSpec((pl.Element(1), pl.Element(MXU_WIDTH), pl.Element(d)),
                               lambda l, i, s, p, g: (l, s[i] * SUBLANES, 0))],
        out_specs=[pl.BlockSpec((None, None, d, MXU_WIDTH), lambda l, i, s, p, g: (l, p[i], 0, 0)),
                   pl.BlockSpec((None, None, d, MXU_WIDTH), lambda l, i, s, p, g: (l, g[i], 0, 0))],
    )
    return pl.pallas_call(
        functools.partial(_pack_w_in_kernel, n_packed=n_packed),
        grid_spec=grid_spec,
        out_shape=[jax.ShapeDtypeStruct((depth, PACKED_BLOCKS, d, MXU_WIDTH), BF16),
                   jax.ShapeDtypeStruct((depth, GATE_BLOCKS, d, MXU_WIDTH), BF16)],
        compiler_params=pltpu.CompilerParams(dimension_semantics=("arbitrary", "arbitrary")),
        name="pack_w_in",
    )(src, pblk, gblk, jnp.swapaxes(w_in, 1, 2))


def _cast_kernel(w_ref, o_ref):
    o_ref[...] = w_ref[...].astype(BF16)


def _cast_bf16(w):
    depth, r, c = w.shape
    n_rows = depth * r
    block_rows = PREP_ROWS
    while (n_rows % (2 * block_rows) == 0 and n_rows // (2 * block_rows) >= CAST_MIN_STEPS
           and 2 * block_rows * c * 4 <= CAST_BLOCK_BYTES):
        block_rows *= 2
    assert n_rows % block_rows == 0
    out = pl.pallas_call(
        _cast_kernel,
        grid=(n_rows // block_rows,),
        in_specs=[pl.BlockSpec((block_rows, c), lambda i: (i, 0))],
        out_specs=pl.BlockSpec((block_rows, c), lambda i: (i, 0)),
        out_shape=jax.ShapeDtypeStruct((n_rows, c), BF16),
        compiler_params=pltpu.CompilerParams(dimension_semantics=("arbitrary",),
                                             vmem_limit_bytes=VMEM_LIMIT_BYTES),
        name="cast_bf16",
    )(w.reshape(n_rows, c))
    return out.reshape(depth, r, c)


def _prepare_weights(g_mix, w_in, conv_w, conv_b, dt_bias, a_log, d_skip, g_ssd, sinks, w_att_out, w_ssd_out,
                     w_out, g_ffn, w_gate, w_up, w_down):
    pad_h = ((0, 0), (0, LANES - SSD_HEADS))
    w_packed, w_gates = _pack_w_in(w_in)
    return {
        "sinks": sinks,
        "g_mix": g_mix[:, None, :],
        "w_in": w_packed,
        "w_gates": w_gates,
        "conv_w": conv_w,
        "conv_b": conv_b[:, None, :],
        "dt_bias": jnp.pad(dt_bias, pad_h)[:, None, :],
        "a": jnp.pad(-jnp.exp(a_log), pad_h)[:, None, :],
        "d_skip": jnp.repeat(d_skip, SSD_HEAD_DIM, axis=1)[:, None, :],
        "g_ssd": g_ssd[:, None, :],
        "w_att_out": _cast_bf16(w_att_out),
        "w_ssd_out": _cast_bf16(w_ssd_out),
        "w_out": _cast_bf16(w_out),
        "g_ffn": g_ffn[:, None, :],
        "w_gate": _cast_bf16(w_gate),
        "w_up": _cast_bf16(w_up),
        "w_down": _cast_bf16(w_down),
    }


PREP_ROWS = 256
CAST_BLOCK_BYTES = 8 * 1024 * 1024
CAST_MIN_STEPS = 4
PROMPT_SEQS_PER_TILE = 4
PROMPT_CHUNKS_PER_TILE = 2
SAMPLE_SEQS_PER_TILE = 4
FFN_BLOCK_ROWS = 1024


def kernel(x_prompt, x_sample, cache_k, cache_v, state_conv, state_ssm, rel_table, g_mix, w_in, conv_w, conv_b, dt_bias, a_log, d_skip, g_ssd, sinks, w_att_out, w_ssd_out, w_out, g_ffn, w_gate, w_up, w_down, g_final):
    depth = w_in.shape[0]
    bp, sp, _ = x_prompt.shape
    bs, ts, _ = x_sample.shape
    kv_len = cache_k.shape[2]
    assert kv_len == WINDOW and ts <= CHUNK and ts % 8 == 0 and ts >= CONV_W - 1

    bias = _blocked_bias(rel_table)
    g_fin = g_final[None, :]

    wts = _prepare_weights(g_mix, w_in, conv_w, conv_b, dt_bias, a_log, d_skip, g_ssd, sinks, w_att_out,
                           w_ssd_out, w_out, g_ffn, w_gate, w_up, w_down)
    init = (cache_k.reshape(depth, bs, WINDOW, KV_WIDTH), cache_v.reshape(depth, bs, WINDOW, KV_WIDTH),
            state_conv, state_ssm.reshape(depth, bs, SSD_INNER, SSD_STATE))

    xp = x_prompt
    xs = x_sample.reshape(bs * ts, D_MODEL)
    st_p, st_s = [], []
    for l in range(depth):
        final = l == depth - 1
        hp, *state_p = _mixer(xp, bias, wts, l, None, n_seq=PROMPT_SEQS_PER_TILE,
                              n_chunk=PROMPT_CHUNKS_PER_TILE, nvalid=sp)
        xp = _ffn(hp.reshape(bp * sp, D_MODEL), wts, l, g_fin, block_rows=FFN_BLOCK_ROWS,
                  final=final).reshape(bp, sp, D_MODEL)
        st_p.append(state_p)
        xs_pad = jnp.pad(xs.reshape(bs, ts, D_MODEL), ((0, 0), (0, CHUNK - ts), (0, 0)))
        hs, *state_s = _mixer(xs_pad, bias, wts, l, init, n_seq=SAMPLE_SEQS_PER_TILE, n_chunk=1, nvalid=ts)
        xs = _ffn(hs[:, :ts].reshape(bs * ts, D_MODEL), wts, l, g_fin, block_rows=bs * ts, final=final)
        st_s.append(state_s)

    def states(sts, b):
        k, v, conv, ssm = (jnp.stack(leaves) for leaves in zip(*sts))
        return (k.reshape(depth, b, WINDOW, N_KV_HEADS, HEAD_DIM),
                v.reshape(depth, b, WINDOW, N_KV_HEADS, HEAD_DIM), conv,
                ssm.reshape(depth, b, SSD_HEADS, SSD_HEAD_DIM, SSD_STATE))

    return (xp, xs.reshape(x_sample.shape), *states(st_p, bp), *states(st_s, bs))
```

```python
import functools
import math

import jax
import jax.numpy as jnp
from jax import lax
from jax.experimental import pallas as pl
from jax.experimental.pallas import tpu as pltpu

D_MODEL = 1024
CHUNK = 64
EPS = 1e-6
NEG_INF = -1e30
N_HEADS = 8
N_KV_HEADS = 2
HEAD_DIM = 64
ATT_WIDTH = N_HEADS * HEAD_DIM
KV_WIDTH = N_KV_HEADS * HEAD_DIM
WINDOW = 128
N_KEYS = WINDOW + CHUNK
NUM_BUCKETS = 32
MAX_DISTANCE = 128
SSD_INNER = 1024
SSD_HEADS = 16
SSD_HEAD_DIM = 64
SSD_GROUPS = 2
SSD_STATE = 128
GROUP_WIDTH = SSD_INNER // SSD_GROUPS
CONV_W = 4
CONV_DIM = SSD_INNER + 2 * SSD_GROUPS * SSD_STATE
D_FF = 2816
LANES = 128
SUBLANES = 8
HIST_ROWS = 8
MXU_WIDTH = 256
XBC_BLOCKS = CONV_DIM // MXU_WIDTH
FILL_POINTS = 7
FILL_FIRST_PHASE = 3

OFF_Q = 0
OFF_KV = OFF_Q + ATT_WIDTH
OFF_XBC = OFF_KV + 2 * KV_WIDTH
OFF_DT = OFF_XBC + CONV_DIM
IN_PACKED = OFF_DT + LANES
GATE_Z = 0
GATE_A = GATE_Z + SSD_INNER
GATE_S = GATE_A + D_MODEL
GATE_WIDTH = GATE_S + D_MODEL
GATE_SLABS = 2
PACKED_BLOCKS = -(-IN_PACKED // MXU_WIDTH)
GATE_BLOCKS = GATE_WIDTH // MXU_WIDTH

VMEM_LIMIT_BYTES = 60 * 1024 * 1024

F32 = jnp.float32
BF16 = jnp.bfloat16


def _dot(a, b):
    return jnp.dot(a, b, preferred_element_type=F32)


def _dot_nt(a, b):
    return lax.dot_general(a, b, (((1,), (1,)), ((), ())), preferred_element_type=F32)


def _split3(x):
    hi = x.astype(BF16)
    r1 = x - hi.astype(F32)
    mid = r1.astype(BF16)
    lo = (r1 - mid.astype(F32)).astype(BF16)
    return hi, mid, lo


def _rmsnorm(x, g):
    return x * lax.rsqrt(jnp.mean(x * x, axis=-1, keepdims=True) + EPS) * g


def _sigmoid(x):
    return 1.0 / (1.0 + jnp.exp(-x))


def _silu(x):
    return x * _sigmoid(x)


def _softplus(x):
    return jnp.maximum(x, 0.0) + jnp.log1p(jnp.exp(-jnp.abs(x)))


def _bias_kernel(table_ref, bucket_ref, o_ref):
    bucket = bucket_ref[...]
    rows = lax.broadcasted_iota(jnp.int32, bucket.shape, 0)
    cols = lax.broadcasted_iota(jnp.int32, bucket.shape, 1)
    row_hi = rows >= CHUNK
    col_hi = (cols & HEAD_DIM) != 0
    for g in range(N_KV_HEADS):
        acc = jnp.zeros(bucket.shape, F32)
        for b in range(NUM_BUCKETS):
            t0 = table_ref[b, 4 * g + 0]
            t1 = table_ref[b, 4 * g + 1]
            t2 = table_ref[b, 4 * g + 2]
            t3 = table_ref[b, 4 * g + 3]
            tv = jnp.where(row_hi, jnp.where(col_hi, t3, t2), jnp.where(col_hi, t1, t0))
            acc = jnp.where(bucket == b, tv, acc)
        o_ref[g] = acc


def _t5_bucket(rel):
    nb = NUM_BUCKETS // 2
    max_exact = nb // 2
    ret = jnp.where(rel > 0, nb, 0)
    n = jnp.abs(rel)
    nf = jnp.maximum(n, 1).astype(jnp.float32)
    large = max_exact + (jnp.log(nf / max_exact) / math.log(MAX_DISTANCE / max_exact)
                         * (nb - max_exact)).astype(jnp.int32)
    large = jnp.minimum(large, nb - 1)
    return ret + jnp.where(n < max_exact, n, large)


def _blocked_bias(rel_table):
    qi = jnp.arange(CHUNK, dtype=jnp.int32)
    kj = jnp.arange(N_KEYS, dtype=jnp.int32)
    bucket = _t5_bucket(kj[None, :] - WINDOW - qi[:, None]).astype(jnp.int32)
    bucket = jnp.tile(bucket.reshape(CHUNK, N_KEYS // CHUNK, 1, CHUNK), (2, 1, 2, 1)).reshape(2 * CHUNK, 2 * N_KEYS)
    return pl.pallas_call(
        _bias_kernel,
        out_shape=jax.ShapeDtypeStruct((N_KV_HEADS, 2 * CHUNK, 2 * N_KEYS), F32),
        in_specs=[pl.BlockSpec(memory_space=pltpu.SMEM),
                  pl.BlockSpec(memory_space=pltpu.VMEM)],
        out_specs=pl.BlockSpec(memory_space=pltpu.VMEM),
        name="rel_bias",
    )(rel_table, bucket)


def _blocked_kv(win, g):
    lo_half = lax.broadcasted_iota(jnp.int32, (N_KEYS, LANES), 1) < HEAD_DIM
    if g == 0:
        a0 = jnp.where(lo_half, win, 0.0)
        a1 = pltpu.roll(a0, HEAD_DIM, axis=1)
    else:
        a1 = jnp.where(lo_half, 0.0, win)
        a0 = pltpu.roll(a1, HEAD_DIM, axis=1)
    pieces = []
    for r in range(0, N_KEYS, CHUNK):
        pieces += [a0[r:r + CHUNK], a1[r:r + CHUNK]]
    return jnp.concatenate(pieces, axis=0).astype(BF16)


def _attention_logits(q, kwin, bias_ref, g):
    qs = jnp.concatenate([q[:, 256 * g:256 * g + LANES],
                          q[:, 256 * g + LANES:256 * (g + 1)]], axis=0).astype(BF16)
    return _dot_nt(qs, _blocked_kv(kwin, g)) + bias_ref[g]


def _attention_finish(logits, vwin, sinks_ref, valid, g):
    row_lo = lax.broadcasted_iota(jnp.int32, (2 * CHUNK, 1), 0) < CHUNK
    even = lax.broadcasted_iota(jnp.int32, (2 * CHUNK, LANES), 1) < HEAD_DIM
    logits = jnp.where(valid, logits, NEG_INF)
    tiles = [logits[:, c:c + LANES] for c in range(0, 2 * N_KEYS, LANES)]
    s0 = jnp.where(row_lo, sinks_ref[4 * g + 0], sinks_ref[4 * g + 2])
    s1 = jnp.where(row_lo, sinks_ref[4 * g + 1], sinks_ref[4 * g + 3])
    tmax = functools.reduce(jnp.maximum, tiles)
    m0 = jnp.maximum(jnp.max(jnp.where(even, tmax, -jnp.inf), axis=1, keepdims=True), s0)
    m1 = jnp.maximum(jnp.max(jnp.where(even, -jnp.inf, tmax), axis=1, keepdims=True), s1)
    m = jnp.where(even, m0, m1)
    p = [jnp.exp(tile - m) for tile in tiles]
    psum = functools.reduce(jnp.add, p)
    d0 = jnp.sum(jnp.where(even, psum, 0.0), axis=1, keepdims=True) + jnp.exp(s0 - m0)
    d1 = jnp.sum(jnp.where(even, 0.0, psum), axis=1, keepdims=True) + jnp.exp(s1 - m1)
    o = _dot(jnp.concatenate(p, axis=1).astype(BF16), _blocked_kv(vwin, g)) / jnp.where(even, d0, d1)
    return jnp.concatenate([o[:CHUNK], o[CHUNK:]], axis=1)


def _ssd_constants(n_blocks):
    r = lax.broadcasted_iota(jnp.int32, (n_blocks * CHUNK, n_blocks * CHUNK), 0)
    c = lax.broadcasted_iota(jnp.int32, (n_blocks * CHUNK, n_blocks * CHUNK), 1)
    tri = jnp.where((c <= r) & ((c >> 6) == (r >> 6)), 1.0, 0.0).astype(BF16)
    hrow = lax.broadcasted_iota(jnp.int32, (LANES, SSD_INNER), 0)
    hcol = lax.broadcasted_iota(jnp.int32, (LANES, SSD_INNER), 1)
    expand = jnp.where(hcol >> 6 == hrow, 1.0, 0.0).astype(BF16)
    return tri, expand


def _ssd_prepare(chunks, dt_raws, dtb_ref, a_ref, nvalid, tri, expand):
    n = len(chunks)
    dt = _softplus(jnp.concatenate(dt_raws, axis=0) + dtb_ref[...])
    if nvalid < CHUNK:
        rows = lax.broadcasted_iota(jnp.int32, dt.shape, 0) & (CHUNK - 1)
        dt = jnp.where(rows < nvalid, dt, 0.0)
    acum3 = _dot(tri, jnp.concatenate(_split3(dt * a_ref[...]), axis=1))
    acum = acum3[:, :LANES] + acum3[:, LANES:2 * LANES] + acum3[:, 2 * LANES:]
    a_hi, a_mid, _ = _split3(acum)
    wide = _dot(jnp.concatenate([dt.astype(BF16), a_hi, a_mid], axis=0), expand)
    m = n * CHUNK
    for i, c in enumerate(chunks):
        sl = slice(i * CHUNK, (i + 1) * CHUNK)
        c["dt_exp"] = wide[sl]
        c["a_col"] = wide[m + i * CHUNK:m + (i + 1) * CHUNK] + wide[2 * m + i * CHUNK:2 * m + (i + 1) * CHUNK]


def _ssd_decay(c):
    a_col = c["a_col"]
    a_last = a_col[CHUNK - 1:CHUNK, :]
    c["xs"] = c["xbc_act"][:, :SSD_INNER]
    c["xdt"] = c["xs"] * c.pop("dt_exp")
    c["xw"] = c["xdt"] * jnp.exp(a_last - a_col)
    c["e_col"] = jnp.exp(a_col)
    c["e_last"] = jnp.exp(a_last)


def _ssd_group(c, gq, ht_ref, g_idx):
    l2 = lax.broadcasted_iota(jnp.int32, (CHUNK, LANES), 0)
    j2 = lax.broadcasted_iota(jnp.int32, (CHUNK, LANES), 1)
    s2 = j2 & (CHUNK - 1)
    diag_sel = jnp.where(s2 == l2, 1.0, 0.0)
    causal2 = s2 <= l2
    lane_lo = j2 < SSD_HEAD_DIM

    b0 = SSD_INNER + SSD_STATE * gq
    c0 = SSD_INNER + SSD_GROUPS * SSD_STATE + SSD_STATE * gq
    bg = c["xbc_act"][:, b0:b0 + SSD_STATE]
    cg = c["xbc_act"][:, c0:c0 + SSD_STATE].astype(BF16)
    bg_bf = bg.astype(BF16)
    cb2 = _dot_nt(cg, jnp.concatenate([bg_bf, bg_bf], axis=0))
    gsl = slice(GROUP_WIDTH * gq, GROUP_WIDTH * (gq + 1))
    h_prev = ht_ref[g_idx, :, gsl]
    y_off = _dot(cg, h_prev.astype(BF16)) * c["e_col"][:, gsl]
    yd = []
    for i in range(GROUP_WIDTH // LANES):
        psl = slice(GROUP_WIDTH * gq + LANES * i, GROUP_WIDTH * gq + LANES * (i + 1))
        ac = c["a_col"][:, psl]
        a_row = jnp.sum(ac * diag_sel, axis=0, keepdims=True)
        lmat = jnp.exp(jnp.where(causal2, ac - a_row, -jnp.inf))
        m2 = (cb2 * lmat).astype(BF16)
        xp = c["xdt"][:, psl]
        xblk = jnp.concatenate([jnp.where(lane_lo, xp, 0.0), jnp.where(lane_lo, 0.0, xp)],
                               axis=0).astype(BF16)
        yd.append(_dot(m2, xblk))
    c.setdefault("ys", []).append(jnp.concatenate(yd, axis=1) + y_off)
    st = _dot(bg.T.astype(BF16), c["xw"][:, gsl].astype(BF16))
    ht_ref[g_idx, :, gsl] = h_prev * c["e_last"][:, gsl] + st


class _LayerSinks:
    def __init__(self, ref, layer):
        self.ref, self.layer = ref, layer

    def __getitem__(self, head):
        return self.ref[self.layer, head]


def _mixer_kernel(*refs, layer, n_seq, n_chunk, n_tiles, nvalid, has_init):
    it = iter(refs)
    x_ref = next(it)
    bias_ref = next(it)
    sinks_ref = _LayerSinks(next(it), layer)
    gmix_ref = next(it)
    win_ref = next(it)
    wgate_ref = next(it)
    convw_ref = next(it)
    convb_ref = next(it)
    dtb_ref = next(it)
    a_ref = next(it)
    dskip_ref = next(it)
    gssd_ref = next(it)
    watt_ref = next(it)
    wssd_ref = next(it)
    wout_ref = next(it)
    if has_init:
        ki_ref = next(it)
        vi_ref = next(it)
        convi_ref = next(it)
        ssmi_ref = next(it)
    h_ref = next(it)
    ko_ref = next(it)
    vo_ref = next(it)
    convo_ref = next(it)
    ssmo_ref = next(it)
    hn_s = next(it)
    q_s = next(it)
    kv_s = next(it)
    xbc_s = next(it)
    xact_s = next(it)
    dt_s = next(it)
    oatt_s = next(it)
    y_s = next(it)
    ht_s = next(it)
    gate_s = next(it)

    t = pl.program_id(1)
    seq_rows = n_chunk * CHUNK

    @pl.when(t == 0)
    def _init():
        if has_init:
            kv_s[:, 0:WINDOW, :KV_WIDTH] = ki_ref[...]
            kv_s[:, 0:WINDOW, KV_WIDTH:] = vi_ref[...]
            xbc_s[:, :, 0:HIST_ROWS, :] = jnp.zeros((XBC_BLOCKS, n_seq, HIST_ROWS, MXU_WIDTH), F32)
            for cb in range(XBC_BLOCKS):
                xbc_s[cb, :, HIST_ROWS - (CONV_W - 1):HIST_ROWS, :] = convi_ref[
                    :, :, cb * MXU_WIDTH:(cb + 1) * MXU_WIDTH]
            for g in range(n_seq):
                ht_s[g] = ssmi_ref[g].T
        else:
            kv_s[:, 0:WINDOW, :] = jnp.zeros((n_seq, WINDOW, 2 * KV_WIDTH), F32)
            xbc_s[:, :, 0:HIST_ROWS, :] = jnp.zeros((XBC_BLOCKS, n_seq, HIST_ROWS, MXU_WIDTH), F32)
            ht_s[...] = jnp.zeros(ht_s.shape, F32)

    rows = n_seq * seq_rows
    hn = _rmsnorm(x_ref[...].reshape(rows, D_MODEL), gmix_ref[...]).astype(BF16)
    hn_s[...] = hn

    def project_block(cb):
        c0 = OFF_XBC + cb * MXU_WIDTH
        blk = _dot(hn, win_ref[c0 // MXU_WIDTH])
        for g in range(n_seq):
            xbc_s[cb, g, HIST_ROWS:HIST_ROWS + seq_rows, :] = blk[g * seq_rows:(g + 1) * seq_rows]

    def conv_block(cb):
        csl = slice(cb * MXU_WIDTH, (cb + 1) * MXU_WIDTH)
        for g in range(n_seq):
            conv = convb_ref[:, csl] + convw_ref[CONV_W - 1:CONV_W, csl] * xbc_s[cb, g, HIST_ROWS:HIST_ROWS + seq_rows, :]
            for i in range(1, CONV_W):
                conv = conv + (convw_ref[CONV_W - 1 - i:CONV_W - i, csl]
                               * xbc_s[cb, g, HIST_ROWS - i:HIST_ROWS - i + seq_rows, :])
            xact_s[cb, g * seq_rows:(g + 1) * seq_rows, :] = _silu(conv)

    project_block(0)
    for cb in range(1, XBC_BLOCKS):
        project_block(cb)
        conv_block(cb - 1)
    for c0 in range(OFF_Q, OFF_KV, MXU_WIDTH):
        q_s[:, c0:c0 + MXU_WIDTH] = _dot(hn, win_ref[c0 // MXU_WIDTH]) * (HEAD_DIM ** -0.5)
    conv_block(XBC_BLOCKS - 1)
    kv = _dot(hn, win_ref[OFF_KV // MXU_WIDTH])
    for g in range(n_seq):
        kv_s[g, WINDOW:WINDOW + seq_rows, :] = kv[g * seq_rows:(g + 1) * seq_rows]
    dt_s[...] = _dot(hn, win_ref[OFF_DT // MXU_WIDTH])[:, :LANES]

    cols = lax.broadcasted_iota(jnp.int32, (2 * CHUNK, 2 * N_KEYS), 1)
    colmod = ((cols >> 7) << 6) + (cols & (CHUNK - 1))
    tri, expand = _ssd_constants(n_seq)

    slab = GATE_WIDTH // GATE_SLABS
    slabs_per_iter = GATE_SLABS // n_chunk

    def iter_body(j, carry):
        k0 = pl.multiple_of(j * CHUNK, CHUNK) if n_chunk > 1 else 0
        per_slab = slab // MXU_WIDTH
        n_pieces = slabs_per_iter * per_slab
        n_points = FILL_POINTS * n_seq
        calls = [0]

        first = FILL_FIRST_PHASE * n_seq
        span = n_points - first

        def fill():
            k = calls[0] - first
            calls[0] += 1
            if k < 0:
                return
            for p in range(-(-k * n_pieces // span), -(-(k + 1) * n_pieces // span)):
                s = j * slabs_per_iter + p // per_slab
                c0 = (p % per_slab) * MXU_WIDTH
                gate_s[s, :, c0:c0 + MXU_WIDTH] = _dot(hn_s[...], wgate_ref[s * per_slab + c0 // MXU_WIDTH])

        if has_init:
            valid = colmod < WINDOW + nvalid
        else:
            first_valid = jnp.maximum(0, (2 - (t * n_chunk + j)) * CHUNK)
            valid = colmod >= first_valid
        chunks = [{"g": g, "rows": pl.ds(pl.multiple_of(g * seq_rows + k0, CHUNK), CHUNK)}
                  for g in range(n_seq)]
        for c in chunks:
            kvwin = kv_s[c["g"], pl.ds(k0, N_KEYS), :]
            c["vwin"] = kvwin[:, KV_WIDTH:]
            q = q_s[c["rows"], :]
            c["logits"] = [_attention_logits(q, kvwin[:, :KV_WIDTH], bias_ref, hg) for hg in range(N_KV_HEADS)]
            fill()
        for c in chunks:
            c["xbc_act"] = jnp.concatenate([xact_s[cb, c["rows"], :] for cb in range(XBC_BLOCKS)], axis=1)
            fill()
        for c in chunks:
            outs = [_attention_finish(c["logits"][hg], c["vwin"], sinks_ref, valid, hg)
                    for hg in range(N_KV_HEADS)]
            oatt_s[c["rows"], :] = jnp.concatenate(outs, axis=1)
            del c["logits"], c["vwin"]
            fill()
        _ssd_prepare(chunks, [dt_s[c["rows"], :] for c in chunks], dtb_ref, a_ref, nvalid, tri, expand)
        for _ in chunks:
            fill()
        for c in chunks:
            _ssd_decay(c)
            fill()
        for gq in range(SSD_GROUPS):
            for c in chunks:
                _ssd_group(c, gq, ht_s, c["g"])
                fill()
        for c in chunks:
            y_s[c["rows"], :] = jnp.concatenate(c["ys"], axis=1) + c["xs"] * dskip_ref[...]
        assert calls[0] == n_points
        return carry

    for j in range(n_chunk):
        iter_body(j, 0)

    def gate_cols(lo, hi):
        pieces = []
        while lo < hi:
            s, off = divmod(lo, slab)
            take = min(hi - lo, slab - off)
            pieces.append(gate_s[s, :, off:off + take])
            lo += take
        return jnp.concatenate(pieces, axis=1)

    y_ssd = _rmsnorm(y_s[...] * _silu(gate_cols(GATE_Z, GATE_A)), gssd_ref[...]).astype(BF16)
    merged = (_sigmoid(gate_cols(GATE_A, GATE_S)) * _dot(oatt_s[...].astype(BF16), watt_ref[...])
              + _sigmoid(gate_cols(GATE_S, GATE_WIDTH)) * _dot(y_ssd, wssd_ref[...]))
    h = x_ref[...].reshape(rows, D_MODEL) + _dot(merged.astype(BF16), wout_ref[...])
    h_ref[...] = h.reshape(n_seq, seq_rows, D_MODEL)

    @pl.when(t == n_tiles - 1)
    def _emit_states():
        ko_ref[...] = kv_s[:, nvalid:nvalid + WINDOW, :KV_WIDTH]
        vo_ref[...] = kv_s[:, nvalid:nvalid + WINDOW, KV_WIDTH:]
        last = HIST_ROWS + nvalid
        convo_ref[...] = jnp.concatenate(
            [xbc_s[cb, :, last - (CONV_W - 1):last, :] for cb in range(XBC_BLOCKS)], axis=-1)
        for g in range(n_seq):
            ssmo_ref[g] = ht_s[g].T

    if n_tiles > 1:
        kv_s[:, 0:WINDOW, :] = kv_s[:, seq_rows:seq_rows + WINDOW, :]
        xbc_s[:, :, 0:HIST_ROWS, :] = xbc_s[:, :, seq_rows:seq_rows + HIST_ROWS, :]


def _layer_spec(shape, layer):
    nd = len(shape)
    return pl.BlockSpec((None,) + tuple(shape), lambda *_, _l=layer, _nd=nd: (_l,) + (0,) * _nd,
                        pipeline_mode=pl.Buffered(1))


def _mixer(x3d, bias, wts, layer, init, *, n_seq, n_chunk, nvalid):
    batch, seq_pad, _ = x3d.shape
    seq_rows = n_chunk * CHUNK
    n_tiles = seq_pad // seq_rows
    assert batch % n_seq == 0 and seq_pad % seq_rows == 0 and GATE_SLABS % n_chunk == 0
    rows = n_seq * seq_rows
    has_init = init is not None
    last_valid = nvalid - (n_tiles - 1) * seq_rows

    tile_map = lambda b, t: (b, t, 0)
    seq_map = lambda b, t: (b, 0, 0)
    cache_map = lambda b, t: (layer, b, 0, 0)
    spec = functools.partial(_layer_spec, layer=layer)
    in_specs = [
        pl.BlockSpec((n_seq, seq_rows, D_MODEL), tile_map),
        pl.BlockSpec(bias.shape, lambda b, t: (0, 0, 0), pipeline_mode=pl.Buffered(1)),
        pl.BlockSpec(memory_space=pltpu.SMEM),
        spec((1, D_MODEL)),
        spec((PACKED_BLOCKS, D_MODEL, MXU_WIDTH)),
        spec((GATE_BLOCKS, D_MODEL, MXU_WIDTH)),
        spec((CONV_W, CONV_DIM)),
        spec((1, CONV_DIM)),
        spec((1, LANES)),
        spec((1, LANES)),
        spec((1, SSD_INNER)),
        spec((1, SSD_INNER)),
        spec((ATT_WIDTH, D_MODEL)),
        spec((SSD_INNER, D_MODEL)),
        spec((D_MODEL, D_MODEL)),
    ]
    args = [x3d, bias, wts["sinks"], wts["g_mix"], wts["w_in"], wts["w_gates"], wts["conv_w"], wts["conv_b"],
            wts["dt_bias"], wts["a"], wts["d_skip"], wts["g_ssd"], wts["w_att_out"], wts["w_ssd_out"],
            wts["w_out"]]
    if has_init:
        in_specs += [
            pl.BlockSpec((None, n_seq, WINDOW, KV_WIDTH), cache_map),
            pl.BlockSpec((None, n_seq, WINDOW, KV_WIDTH), cache_map),
            pl.BlockSpec((None, n_seq, CONV_W - 1, CONV_DIM), cache_map),
            pl.BlockSpec((None, n_seq, SSD_INNER, SSD_STATE), cache_map),
        ]
        args += list(init)
    out_shape = [
        jax.ShapeDtypeStruct((batch, seq_pad, D_MODEL), F32),
        jax.ShapeDtypeStruct((batch, WINDOW, KV_WIDTH), F32),
        jax.ShapeDtypeStruct((batch, WINDOW, KV_WIDTH), F32),
        jax.ShapeDtypeStruct((batch, CONV_W - 1, CONV_DIM), F32),
        jax.ShapeDtypeStruct((batch, SSD_INNER, SSD_STATE), F32),
    ]
    out_specs = [
        pl.BlockSpec((n_seq, seq_rows, D_MODEL), tile_map),
        pl.BlockSpec((n_seq, WINDOW, KV_WIDTH), seq_map),
        pl.BlockSpec((n_seq, WINDOW, KV_WIDTH), seq_map),
        pl.BlockSpec((n_seq, CONV_W - 1, CONV_DIM), seq_map),
        pl.BlockSpec((n_seq, SSD_INNER, SSD_STATE), seq_map),
    ]
    scratch = [
        pltpu.VMEM((rows, D_MODEL), BF16),
        pltpu.VMEM((rows, ATT_WIDTH), F32),
        pltpu.VMEM((n_seq, WINDOW + seq_rows, 2 * KV_WIDTH), F32),
        pltpu.VMEM((XBC_BLOCKS, n_seq, HIST_ROWS + seq_rows, MXU_WIDTH), F32),
        pltpu.VMEM((XBC_BLOCKS, rows, MXU_WIDTH), F32),
        pltpu.VMEM((rows, LANES), F32),
        pltpu.VMEM((rows, ATT_WIDTH), F32),
        pltpu.VMEM((rows, SSD_INNER), F32),
        pltpu.VMEM((n_seq, SSD_STATE, SSD_INNER), F32),
        pltpu.VMEM((GATE_SLABS, rows, GATE_WIDTH // GATE_SLABS), F32),
    ]
    kern = functools.partial(_mixer_kernel, layer=layer, n_seq=n_seq, n_chunk=n_chunk, n_tiles=n_tiles,
                             nvalid=last_valid, has_init=has_init)
    return pl.pallas_call(
        kern,
        grid=(batch // n_seq, n_tiles),
        in_specs=in_specs,
        out_specs=out_specs,
        out_shape=out_shape,
        scratch_shapes=scratch,
        compiler_params=pltpu.CompilerParams(
            dimension_semantics=("arbitrary", "arbitrary"), vmem_limit_bytes=VMEM_LIMIT_BYTES),
        name="mixer_init" if has_init else "mixer",
    )(*args)


def _ffn_kernel(h_ref, g_ref, wg_ref, wu_ref, wd_ref, gfin_ref, o_ref, *, final):
    h = h_ref[...]
    hf = _rmsnorm(h, g_ref[...]).astype(BF16)
    act = (_silu(_dot(hf, wg_ref[...])) * _dot(hf, wu_ref[...])).astype(BF16)
    out = h + _dot(act, wd_ref[...])
    if final:
        out = _rmsnorm(out, gfin_ref[...])
    o_ref[...] = out


def _ffn(h2d, wts, layer, g_final, *, block_rows, final):
    n_rows = h2d.shape[0]
    assert n_rows % block_rows == 0
    spec = functools.partial(_layer_spec, layer=layer)
    return pl.pallas_call(
        functools.partial(_ffn_kernel, final=final),
        grid=(n_rows // block_rows,),
        in_specs=[
            pl.BlockSpec((block_rows, D_MODEL), lambda i: (i, 0)),
            spec((1, D_MODEL)),
            spec((D_MODEL, D_FF)),
            spec((D_MODEL, D_FF)),
            spec((D_FF, D_MODEL)),
            pl.BlockSpec((1, D_MODEL), lambda i: (0, 0), pipeline_mode=pl.Buffered(1)),
        ],
        out_specs=pl.BlockSpec((block_rows, D_MODEL), lambda i: (i, 0)),
        out_shape=jax.ShapeDtypeStruct((n_rows, D_MODEL), F32),
        compiler_params=pltpu.CompilerParams(
            dimension_semantics=("arbitrary",), vmem_limit_bytes=VMEM_LIMIT_BYTES),
        name="ffn_final" if final else "ffn",
    )(h2d, wts["g_ffn"], wts["w_gate"], wts["w_up"], wts["w_down"], g_final)


def _transpose_cast_kernel(src_ref, w_ref, o_ref, *, n_sub, keep_last):
    del src_ref
    t = w_ref[0].T
    if keep_last is not None:
        last = pl.program_id(1) == pl.num_programs(1) - 1
        lane = lax.broadcasted_iota(jnp.int32, t.shape, 1)
        t = jnp.where(lane < jnp.where(last, keep_last, t.shape[1]), t, 0.0)
    t = t.astype(BF16)
    for k in range(n_sub):
        o_ref[k] = t[:, k * MXU_WIDTH:(k + 1) * MXU_WIDTH]


def _transpose_cast(w_in_t, sources, n_sub, keep_last, name):
    depth, _, d = w_in_t.shape
    assert all(s % SUBLANES == 0 for s in sources)
    src = jnp.asarray([s // SUBLANES for s in sources], jnp.int32)
    grid_spec = pltpu.PrefetchScalarGridSpec(
        num_scalar_prefetch=1,
        grid=(depth, len(sources)),
        in_specs=[pl.BlockSpec((pl.Element(1), pl.Element(n_sub * MXU_WIDTH), pl.Element(d)),
                               lambda l, i, s: (l, s[i] * SUBLANES, 0))],
        out_specs=pl.BlockSpec((None, n_sub, d, MXU_WIDTH), lambda l, i, s: (l, i, 0, 0)),
    )
    return pl.pallas_call(
        functools.partial(_transpose_cast_kernel, n_sub=n_sub, keep_last=keep_last),
        grid_spec=grid_spec,
        out_shape=jax.ShapeDtypeStruct((depth, len(sources) * n_sub, d, MXU_WIDTH), BF16),
        compiler_params=pltpu.CompilerParams(dimension_semantics=("arbitrary", "arbitrary")),
        name=name,
    )(src, w_in_t)


def _pack_w_in(w_in):
    o_z = ATT_WIDTH + 2 * KV_WIDTH
    o_xbc = o_z + SSD_INNER
    o_dt = o_xbc + CONV_DIM
    o_ga = o_dt + SSD_HEADS
    o_gs = o_ga + D_MODEL
    w_in_t = jnp.swapaxes(w_in, 1, 2)
    packed = _transpose_cast(
        w_in_t, [*range(0, o_z, MXU_WIDTH), *range(o_xbc, o_dt, MXU_WIDTH), o_dt], 1, SSD_HEADS, "pack_w_in")
    pair = 2 * MXU_WIDTH
    gates = _transpose_cast(
        w_in_t, [*range(o_z, o_xbc, pair), *range(o_ga, o_gs, pair), *range(o_gs, o_gs + D_MODEL, pair)],
        2, None, "pack_gates")
    return packed, gates


def _cast_kernel(w_ref, o_ref):
    o_ref[...] = w_ref[...].astype(BF16)


def _cast_bf16(w):
    depth, r, c = w.shape
    n_rows = depth * r
    block_rows = PREP_ROWS
    while (n_rows % (2 * block_rows) == 0 and n_rows // (2 * block_rows) >= CAST_MIN_STEPS
           and 2 * block_rows * c * 4 <= CAST_BLOCK_BYTES):
        block_rows *= 2
    assert n_rows % block_rows == 0
    out = pl.pallas_call(
        _cast_kernel,
        grid=(n_rows // block_rows,),
        in_specs=[pl.BlockSpec((block_rows, c), lambda i: (i, 0))],
        out_specs=pl.BlockSpec((block_rows, c), lambda i: (i, 0)),
        out_shape=jax.ShapeDtypeStruct((n_rows, c), BF16),
        compiler_params=pltpu.CompilerParams(dimension_semantics=("arbitrary",),
                                             vmem_limit_bytes=VMEM_LIMIT_BYTES),
        name="cast_bf16",
    )(w.reshape(n_rows, c))
    return out.reshape(depth, r, c)


def _prepare_weights(g_mix, w_in, conv_w, conv_b, dt_bias, a_log, d_skip, g_ssd, sinks, w_att_out, w_ssd_out,
                     w_out, g_ffn, w_gate, w_up, w_down):
    pad_h = ((0, 0), (0, LANES - SSD_HEADS))
    w_packed, w_gates = _pack_w_in(w_in)
    return {
        "sinks": sinks,
        "g_mix": g_mix[:, None, :],
        "w_in": w_packed,
        "w_gates": w_gates,
        "conv_w": conv_w,
        "conv_b": conv_b[:, None, :],
        "dt_bias": jnp.pad(dt_bias, pad_h)[:, None, :],
        "a": jnp.pad(-jnp.exp(a_log), pad_h)[:, None, :],
        "d_skip": jnp.repeat(d_skip, SSD_HEAD_DIM, axis=1)[:, None, :],
        "g_ssd": g_ssd[:, None, :],
        "w_att_out": _cast_bf16(w_att_out),
        "w_ssd_out": _cast_bf16(w_ssd_out),
        "w_out": _cast_bf16(w_out),
        "g_ffn": g_ffn[:, None, :],
        "w_gate": _cast_bf16(w_gate),
        "w_up": _cast_bf16(w_up),
        "w_down": _cast_bf16(w_down),
    }


PREP_ROWS = 256
CAST_BLOCK_BYTES = 8 * 1024 * 1024
CAST_MIN_STEPS = 4
PROMPT_SEQS_PER_TILE = 4
PROMPT_CHUNKS_PER_TILE = 2
SAMPLE_SEQS_PER_TILE = 4
FFN_BLOCK_ROWS = 1024


def kernel(x_prompt, x_sample, cache_k, cache_v, state_conv, state_ssm, rel_table, g_mix, w_in, conv_w, conv_b, dt_bias, a_log, d_skip, g_ssd, sinks, w_att_out, w_ssd_out, w_out, g_ffn, w_gate, w_up, w_down, g_final):
    depth = w_in.shape[0]
    bp, sp, _ = x_prompt.shape
    bs, ts, _ = x_sample.shape
    kv_len = cache_k.shape[2]
    assert kv_len == WINDOW and ts <= CHUNK and ts % 8 == 0 and ts >= CONV_W - 1

    bias = _blocked_bias(rel_table)
    g_fin = g_final[None, :]

    wts = _prepare_weights(g_mix, w_in, conv_w, conv_b, dt_bias, a_log, d_skip, g_ssd, sinks, w_att_out,
                           w_ssd_out, w_out, g_ffn, w_gate, w_up, w_down)
    init = (cache_k.reshape(depth, bs, WINDOW, KV_WIDTH), cache_v.reshape(depth, bs, WINDOW, KV_WIDTH),
            state_conv, state_ssm.reshape(depth, bs, SSD_INNER, SSD_STATE))

    xp = x_prompt
    xs = x_sample.reshape(bs * ts, D_MODEL)
    st_p, st_s = [], []
    for l in range(depth):
        final = l == depth - 1
        hp, *state_p = _mixer(xp, bias, wts, l, None, n_seq=PROMPT_SEQS_PER_TILE,
                              n_chunk=PROMPT_CHUNKS_PER_TILE, nvalid=sp)
        xp = _ffn(hp.reshape(bp * sp, D_MODEL), wts, l, g_fin, block_rows=FFN_BLOCK_ROWS,
                  final=final).reshape(bp, sp, D_MODEL)
        st_p.append(state_p)
        xs_pad = jnp.pad(xs.reshape(bs, ts, D_MODEL), ((0, 0), (0, CHUNK - ts), (0, 0)))
        hs, *state_s = _mixer(xs_pad, bias, wts, l, init, n_seq=SAMPLE_SEQS_PER_TILE, n_chunk=1, nvalid=ts)
        xs = _ffn(hs[:, :ts].reshape(bs * ts, D_MODEL), wts, l, g_fin, block_rows=bs * ts, final=final)
        st_s.append(state_s)

    def states(sts, b):
        k, v, conv, ssm = (jnp.stack(leaves) for leaves in zip(*sts))
        return (k.reshape(depth, b, WINDOW, N_KV_HEADS, HEAD_DIM),
                v.reshape(depth, b, WINDOW, N_KV_HEADS, HEAD_DIM), conv,
                ssm.reshape(depth, b, SSD_HEADS, SSD_HEAD_DIM, SSD_STATE))

    return (xp, xs.reshape(x_sample.shape), *states(st_p, bp), *states(st_s, bs))
```

```python
import functools
import math

import jax
import jax.numpy as jnp
from jax import lax
from jax.experimental import pallas as pl
from jax.experimental.pallas import tpu as pltpu

D_MODEL = 1024
CHUNK = 64
EPS = 1e-6
NEG_INF = -1e30
N_HEADS = 8
N_KV_HEADS = 2
HEAD_DIM = 64
ATT_WIDTH = N_HEADS * HEAD_DIM
KV_WIDTH = N_KV_HEADS * HEAD_DIM
WINDOW = 128
N_KEYS = WINDOW + CHUNK
NUM_BUCKETS = 32
MAX_DISTANCE = 128
SSD_INNER = 1024
SSD_HEADS = 16
SSD_HEAD_DIM = 64
SSD_GROUPS = 2
SSD_STATE = 128
GROUP_WIDTH = SSD_INNER // SSD_GROUPS
CONV_W = 4
CONV_DIM = SSD_INNER + 2 * SSD_GROUPS * SSD_STATE
D_FF = 2816
LANES = 128
SUBLANES = 8
HIST_ROWS = 8
MXU_WIDTH = 256
XBC_BLOCKS = CONV_DIM // MXU_WIDTH
FILL_POINTS = 7
FILL_FIRST_PHASE = 3

OFF_Q = 0
OFF_KV = OFF_Q + ATT_WIDTH
OFF_XBC = OFF_KV + 2 * KV_WIDTH
OFF_DT = OFF_XBC + CONV_DIM
IN_PACKED = OFF_DT + LANES
GATE_Z = 0
GATE_A = GATE_Z + SSD_INNER
GATE_S = GATE_A + D_MODEL
GATE_WIDTH = GATE_S + D_MODEL
GATE_SLABS = 2
PACKED_BLOCKS = -(-IN_PACKED // MXU_WIDTH)
GATE_BLOCKS = GATE_WIDTH // MXU_WIDTH

VMEM_LIMIT_BYTES = 60 * 1024 * 1024

F32 = jnp.float32
BF16 = jnp.bfloat16


def _dot(a, b):
    return jnp.dot(a, b, preferred_element_type=F32)


def _dot_nt(a, b):
    return lax.dot_general(a, b, (((1,), (1,)), ((), ())), preferred_element_type=F32)


def _split3(x):
    hi = x.astype(BF16)
    r1 = x - hi.astype(F32)
    mid = r1.astype(BF16)
    lo = (r1 - mid.astype(F32)).astype(BF16)
    return hi, mid, lo


def _rmsnorm(x, g):
    return x * lax.rsqrt(jnp.mean(x * x, axis=-1, keepdims=True) + EPS) * g


def _sigmoid(x):
    return 1.0 / (1.0 + jnp.exp(-x))


def _silu(x):
    return x * _sigmoid(x)


def _softplus(x):
    return jnp.maximum(x, 0.0) + jnp.log1p(jnp.exp(-jnp.abs(x)))


def _bias_kernel(table_ref, bucket_ref, o_ref):
    bucket = bucket_ref[...]
    rows = lax.broadcasted_iota(jnp.int32, bucket.shape, 0)
    cols = lax.broadcasted_iota(jnp.int32, bucket.shape, 1)
    row_hi = rows >= CHUNK
    col_hi = (cols & HEAD_DIM) != 0
    for g in range(N_KV_HEADS):
        acc = jnp.zeros(bucket.shape, F32)
        for b in range(NUM_BUCKETS):
            t0 = table_ref[b, 4 * g + 0]
            t1 = table_ref[b, 4 * g + 1]
            t2 = table_ref[b, 4 * g + 2]
            t3 = table_ref[b, 4 * g + 3]
            tv = jnp.where(row_hi, jnp.where(col_hi, t3, t2), jnp.where(col_hi, t1, t0))
            acc = jnp.where(bucket == b, tv, acc)
        o_ref[g] = acc


def _t5_bucket(rel):
    nb = NUM_BUCKETS // 2
    max_exact = nb // 2
    ret = jnp.where(rel > 0, nb, 0)
    n = jnp.abs(rel)
    nf = jnp.maximum(n, 1).astype(jnp.float32)
    large = max_exact + (jnp.log(nf / max_exact) / math.log(MAX_DISTANCE / max_exact)
                         * (nb - max_exact)).astype(jnp.int32)
    large = jnp.minimum(large, nb - 1)
    return ret + jnp.where(n < max_exact, n, large)


def _blocked_bias(rel_table):
    qi = jnp.arange(CHUNK, dtype=jnp.int32)
    kj = jnp.arange(N_KEYS, dtype=jnp.int32)
    bucket = _t5_bucket(kj[None, :] - WINDOW - qi[:, None]).astype(jnp.int32)
    bucket = jnp.tile(bucket.reshape(CHUNK, N_KEYS // CHUNK, 1, CHUNK), (2, 1, 2, 1)).reshape(2 * CHUNK, 2 * N_KEYS)
    return pl.pallas_call(
        _bias_kernel,
        out_shape=jax.ShapeDtypeStruct((N_KV_HEADS, 2 * CHUNK, 2 * N_KEYS), F32),
        in_specs=[pl.BlockSpec(memory_space=pltpu.SMEM),
                  pl.BlockSpec(memory_space=pltpu.VMEM)],
        out_specs=pl.BlockSpec(memory_space=pltpu.VMEM),
        name="rel_bias",
    )(rel_table, bucket)


def _blocked_kv(win, g):
    lo_half = lax.broadcasted_iota(jnp.int32, (N_KEYS, LANES), 1) < HEAD_DIM
    if g == 0:
        a0 = jnp.where(lo_half, win, 0.0)
        a1 = pltpu.roll(a0, HEAD_DIM, axis=1)
    else:
        a1 = jnp.where(lo_half, 0.0, win)
        a0 = pltpu.roll(a1, HEAD_DIM, axis=1)
    pieces = []
    for r in range(0, N_KEYS, CHUNK):
        pieces += [a0[r:r + CHUNK], a1[r:r + CHUNK]]
    return jnp.concatenate(pieces, axis=0).astype(BF16)


def _attention_logits(q, kwin, bias_ref, g):
    qs = jnp.concatenate([q[:, 256 * g:256 * g + LANES],
                          q[:, 256 * g + LANES:256 * (g + 1)]], axis=0).astype(BF16)
    return _dot_nt(qs, _blocked_kv(kwin, g)) + bias_ref[g]


def _attention_finish(logits, vwin, sinks_ref, valid, g):
    row_lo = lax.broadcasted_iota(jnp.int32, (2 * CHUNK, 1), 0) < CHUNK
    even = lax.broadcasted_iota(jnp.int32, (2 * CHUNK, LANES), 1) < HEAD_DIM
    logits = jnp.where(valid, logits, NEG_INF)
    tiles = [logits[:, c:c + LANES] for c in range(0, 2 * N_KEYS, LANES)]
    s0 = jnp.where(row_lo, sinks_ref[4 * g + 0], sinks_ref[4 * g + 2])
    s1 = jnp.where(row_lo, sinks_ref[4 * g + 1], sinks_ref[4 * g + 3])
    tmax = functools.reduce(jnp.maximum, tiles)
    m0 = jnp.maximum(jnp.max(jnp.where(even, tmax, -jnp.inf), axis=1, keepdims=True), s0)
    m1 = jnp.maximum(jnp.max(jnp.where(even, -jnp.inf, tmax), axis=1, keepdims=True), s1)
    m = jnp.where(even, m0, m1)
    p = [jnp.exp(tile - m) for tile in tiles]
    psum = functools.reduce(jnp.add, p)
    d0 = jnp.sum(jnp.where(even, psum, 0.0), axis=1, keepdims=True) + jnp.exp(s0 - m0)
    d1 = jnp.sum(jnp.where(even, 0.0, psum), axis=1, keepdims=True) + jnp.exp(s1 - m1)
    o = _dot(jnp.concatenate(p, axis=1).astype(BF16), _blocked_kv(vwin, g)) / jnp.where(even, d0, d1)
    return jnp.concatenate([o[:CHUNK], o[CHUNK:]], axis=1)


def _ssd_constants(n_blocks):
    r = lax.broadcasted_iota(jnp.int32, (n_blocks * CHUNK, n_blocks * CHUNK), 0)
    c = lax.broadcasted_iota(jnp.int32, (n_blocks * CHUNK, n_blocks * CHUNK), 1)
    tri = jnp.where((c <= r) & ((c >> 6) == (r >> 6)), 1.0, 0.0).astype(BF16)
    hrow = lax.broadcasted_iota(jnp.int32, (LANES, SSD_INNER), 0)
    hcol = lax.broadcasted_iota(jnp.int32, (LANES, SSD_INNER), 1)
    expand = jnp.where(hcol >> 6 == hrow, 1.0, 0.0).astype(BF16)
    return tri, expand


def _ssd_prepare(chunks, dt_raws, dtb_ref, a_ref, nvalid, tri, expand):
    n = len(chunks)
    dt = _softplus(jnp.concatenate(dt_raws, axis=0) + dtb_ref[...])
    if nvalid < CHUNK:
        rows = lax.broadcasted_iota(jnp.int32, dt.shape, 0) & (CHUNK - 1)
        dt = jnp.where(rows < nvalid, dt, 0.0)
    acum3 = _dot(tri, jnp.concatenate(_split3(dt * a_ref[...]), axis=1))
    acum = acum3[:, :LANES] + acum3[:, LANES:2 * LANES] + acum3[:, 2 * LANES:]
    a_hi, a_mid, _ = _split3(acum)
    wide = _dot(jnp.concatenate([dt.astype(BF16), a_hi, a_mid], axis=0), expand)
    m = n * CHUNK
    for i, c in enumerate(chunks):
        sl = slice(i * CHUNK, (i + 1) * CHUNK)
        c["dt_exp"] = wide[sl]
        c["a_col"] = wide[m + i * CHUNK:m + (i + 1) * CHUNK] + wide[2 * m + i * CHUNK:2 * m + (i + 1) * CHUNK]


def _ssd_decay(c):
    a_col = c["a_col"]
    a_last = a_col[CHUNK - 1:CHUNK, :]
    c["xs"] = c["xbc_act"][:, :SSD_INNER]
    c["xdt"] = c["xs"] * c.pop("dt_exp")
    c["xw"] = c["xdt"] * jnp.exp(a_last - a_col)
    c["e_col"] = jnp.exp(a_col)
    c["e_last"] = jnp.exp(a_last)


def _ssd_group(c, gq, ht_ref, g_idx):
    l2 = lax.broadcasted_iota(jnp.int32, (CHUNK, LANES), 0)
    j2 = lax.broadcasted_iota(jnp.int32, (CHUNK, LANES), 1)
    s2 = j2 & (CHUNK - 1)
    diag_sel = jnp.where(s2 == l2, 1.0, 0.0)
    causal2 = s2 <= l2
    lane_lo = j2 < SSD_HEAD_DIM

    b0 = SSD_INNER + SSD_STATE * gq
    c0 = SSD_INNER + SSD_GROUPS * SSD_STATE + SSD_STATE * gq
    bg = c["xbc_act"][:, b0:b0 + SSD_STATE]
    cg = c["xbc_act"][:, c0:c0 + SSD_STATE].astype(BF16)
    bg_bf = bg.astype(BF16)
    cb2 = _dot_nt(cg, jnp.concatenate([bg_bf, bg_bf], axis=0))
    gsl = slice(GROUP_WIDTH * gq, GROUP_WIDTH * (gq + 1))
    h_prev = ht_ref[g_idx, :, gsl]
    y_off = _dot(cg, h_prev.astype(BF16)) * c["e_col"][:, gsl]
    yd = []
    for i in range(GROUP_WIDTH // LANES):
        psl = slice(GROUP_WIDTH * gq + LANES * i, GROUP_WIDTH * gq + LANES * (i + 1))
        ac = c["a_col"][:, psl]
        a_row = jnp.sum(ac * diag_sel, axis=0, keepdims=True)
        lmat = jnp.exp(jnp.where(causal2, ac - a_row, -jnp.inf))
        m2 = (cb2 * lmat).astype(BF16)
        xp = c["xdt"][:, psl]
        xblk = jnp.concatenate([jnp.where(lane_lo, xp, 0.0), jnp.where(lane_lo, 0.0, xp)],
                               axis=0).astype(BF16)
        yd.append(_dot(m2, xblk))
    c.setdefault("ys", []).append(jnp.concatenate(yd, axis=1) + y_off)
    st = _dot(bg.T.astype(BF16), c["xw"][:, gsl].astype(BF16))
    ht_ref[g_idx, :, gsl] = h_prev * c["e_last"][:, gsl] + st


class _LayerSinks:
    def __init__(self, ref, layer):
        self.ref, self.layer = ref, layer

    def __getitem__(self, head):
        return self.ref[self.layer, head]


def _mixer_kernel(*refs, layer, n_seq, n_chunk, n_tiles, nvalid, has_init):
    it = iter(refs)
    x_ref = next(it)
    bias_ref = next(it)
    sinks_ref = _LayerSinks(next(it), layer)
    gmix_ref = next(it)
    win_ref = next(it)
    wgate_ref = next(it)
    convw_ref = next(it)
    convb_ref = next(it)
    dtb_ref = next(it)
    a_ref = next(it)
    dskip_ref = next(it)
    gssd_ref = next(it)
    watt_ref = next(it)
    wssd_ref = next(it)
    wout_ref = next(it)
    if has_init:
        ki_ref = next(it)
        vi_ref = next(it)
        convi_ref = next(it)
        ssmi_ref = next(it)
    h_ref = next(it)
    ko_ref = next(it)
    vo_ref = next(it)
    convo_ref = next(it)
    ssmo_ref = next(it)
    hn_s = next(it)
    q_s = next(it)
    kv_s = next(it)
    xbc_s = next(it)
    xact_s = next(it)
    dt_s = next(it)
    oatt_s = next(it)
    y_s = next(it)
    ht_s = next(it)
    gate_s = next(it)

    t = pl.program_id(1)
    seq_rows = n_chunk * CHUNK

    @pl.when(t == 0)
    def _init():
        if has_init:
            kv_s[:, 0:WINDOW, :KV_WIDTH] = ki_ref[...]
            kv_s[:, 0:WINDOW, KV_WIDTH:] = vi_ref[...]
            xbc_s[:, :, 0:HIST_ROWS, :] = jnp.zeros((XBC_BLOCKS, n_seq, HIST_ROWS, MXU_WIDTH), F32)
            for cb in range(XBC_BLOCKS):
                xbc_s[cb, :, HIST_ROWS - (CONV_W - 1):HIST_ROWS, :] = convi_ref[
                    :, :, cb * MXU_WIDTH:(cb + 1) * MXU_WIDTH]
            for g in range(n_seq):
                ht_s[g] = ssmi_ref[g].T
        else:
            kv_s[:, 0:WINDOW, :] = jnp.zeros((n_seq, WINDOW, 2 * KV_WIDTH), F32)
            xbc_s[:, :, 0:HIST_ROWS, :] = jnp.zeros((XBC_BLOCKS, n_seq, HIST_ROWS, MXU_WIDTH), F32)
            ht_s[...] = jnp.zeros(ht_s.shape, F32)

    rows = n_seq * seq_rows
    hn = _rmsnorm(x_ref[...].reshape(rows, D_MODEL), gmix_ref[...]).astype(BF16)
    hn_s[...] = hn

    def project_block(cb):
        c0 = OFF_XBC + cb * MXU_WIDTH
        blk = _dot(hn, win_ref[c0 // MXU_WIDTH])
        for g in range(n_seq):
            xbc_s[cb, g, HIST_ROWS:HIST_ROWS + seq_rows, :] = blk[g * seq_rows:(g + 1) * seq_rows]

    def conv_block(cb):
        csl = slice(cb * MXU_WIDTH, (cb + 1) * MXU_WIDTH)
        for g in range(n_seq):
            conv = convb_ref[:, csl] + convw_ref[CONV_W - 1:CONV_W, csl] * xbc_s[cb, g, HIST_ROWS:HIST_ROWS + seq_rows, :]
            for i in range(1, CONV_W):
                conv = conv + (convw_ref[CONV_W - 1 - i:CONV_W - i, csl]
                               * xbc_s[cb, g, HIST_ROWS - i:HIST_ROWS - i + seq_rows, :])
            xact_s[cb, g * seq_rows:(g + 1) * seq_rows, :] = _silu(conv)

    project_block(0)
    for cb in range(1, XBC_BLOCKS):
        project_block(cb)
        conv_block(cb - 1)
    for c0 in range(OFF_Q, OFF_KV, MXU_WIDTH):
        q_s[:, c0:c0 + MXU_WIDTH] = _dot(hn, win_ref[c0 // MXU_WIDTH]) * (HEAD_DIM ** -0.5)
    conv_block(XBC_BLOCKS - 1)
    kv = _dot(hn, win_ref[OFF_KV // MXU_WIDTH])
    for g in range(n_seq):
        kv_s[g, WINDOW:WINDOW + seq_rows, :] = kv[g * seq_rows:(g + 1) * seq_rows]
    dt_s[...] = _dot(hn, win_ref[OFF_DT // MXU_WIDTH])[:, :LANES]

    cols = lax.broadcasted_iota(jnp.int32, (2 * CHUNK, 2 * N_KEYS), 1)
    colmod = ((cols >> 7) << 6) + (cols & (CHUNK - 1))
    tri, expand = _ssd_constants(n_seq)

    slab = GATE_WIDTH // GATE_SLABS
    slabs_per_iter = GATE_SLABS // n_chunk

    def iter_body(j, carry):
        k0 = pl.multiple_of(j * CHUNK, CHUNK) if n_chunk > 1 else 0
        per_slab = slab // MXU_WIDTH
        n_pieces = slabs_per_iter * per_slab
        n_points = FILL_POINTS * n_seq
        calls = [0]

        first = FILL_FIRST_PHASE * n_seq
        span = n_points - first

        def fill():
            k = calls[0] - first
            calls[0] += 1
            if k < 0:
                return
            for p in range(-(-k * n_pieces // span), -(-(k + 1) * n_pieces // span)):
                s = j * slabs_per_iter + p // per_slab
                c0 = (p % per_slab) * MXU_WIDTH
                gate_s[s, :, c0:c0 + MXU_WIDTH] = _dot(hn_s[...], wgate_ref[s * per_slab + c0 // MXU_WIDTH])

        if has_init:
            valid = colmod < WINDOW + nvalid
        else:
            first_valid = jnp.maximum(0, (2 - (t * n_chunk + j)) * CHUNK)
            valid = colmod >= first_valid
        chunks = [{"g": g, "rows": pl.ds(pl.multiple_of(g * seq_rows + k0, CHUNK), CHUNK)}
                  for g in range(n_seq)]
        for c in chunks:
            kvwin = kv_s[c["g"], pl.ds(k0, N_KEYS), :]
            c["vwin"] = kvwin[:, KV_WIDTH:]
            q = q_s[c["rows"], :]
            c["logits"] = [_attention_logits(q, kvwin[:, :KV_WIDTH], bias_ref, hg) for hg in range(N_KV_HEADS)]
            fill()
        for c in chunks:
            c["xbc_act"] = jnp.concatenate([xact_s[cb, c["rows"], :] for cb in range(XBC_BLOCKS)], axis=1)
            fill()
        for c in chunks:
            outs = [_attention_finish(c["logits"][hg], c["vwin"], sinks_ref, valid, hg)
                    for hg in range(N_KV_HEADS)]
            oatt_s[c["rows"], :] = jnp.concatenate(outs, axis=1)
            del c["logits"], c["vwin"]
            fill()
        _ssd_prepare(chunks, [dt_s[c["rows"], :] for c in chunks], dtb_ref, a_ref, nvalid, tri, expand)
        for _ in chunks:
            fill()
        for c in chunks:
            _ssd_decay(c)
            fill()
        for gq in range(SSD_GROUPS):
            for c in chunks:
                _ssd_group(c, gq, ht_s, c["g"])
                fill()
        for c in chunks:
            y_s[c["rows"], :] = jnp.concatenate(c["ys"], axis=1) + c["xs"] * dskip_ref[...]
        assert calls[0] == n_points
        return carry

    for j in range(n_chunk):
        iter_body(j, 0)

    def gate_cols(lo, hi):
        pieces = []
        while lo < hi:
            s, off = divmod(lo, slab)
            take = min(hi - lo, slab - off)
            pieces.append(gate_s[s, :, off:off + take])
            lo += take
        return jnp.concatenate(pieces, axis=1)

    y_ssd = _rmsnorm(y_s[...] * _silu(gate_cols(GATE_Z, GATE_A)), gssd_ref[...]).astype(BF16)
    merged = (_sigmoid(gate_cols(GATE_A, GATE_S)) * _dot(oatt_s[...].astype(BF16), watt_ref[...])
              + _sigmoid(gate_cols(GATE_S, GATE_WIDTH)) * _dot(y_ssd, wssd_ref[...]))
    h = x_ref[...].reshape(rows, D_MODEL) + _dot(merged.astype(BF16), wout_ref[...])
    h_ref[...] = h.reshape(n_seq, seq_rows, D_MODEL)

    @pl.when(t == n_tiles - 1)
    def _emit_states():
        ko_ref[...] = kv_s[:, nvalid:nvalid + WINDOW, :KV_WIDTH]
        vo_ref[...] = kv_s[:, nvalid:nvalid + WINDOW, KV_WIDTH:]
        last = HIST_ROWS + nvalid
        convo_ref[...] = jnp.concatenate(
            [xbc_s[cb, :, last - (CONV_W - 1):last, :] for cb in range(XBC_BLOCKS)], axis=-1)
        for g in range(n_seq):
            ssmo_ref[g] = ht_s[g].T

    if n_tiles > 1:
        kv_s[:, 0:WINDOW, :] = kv_s[:, seq_rows:seq_rows + WINDOW, :]
        xbc_s[:, :, 0:HIST_ROWS, :] = xbc_s[:, :, seq_rows:seq_rows + HIST_ROWS, :]


def _layer_spec(shape, layer):
    nd = len(shape)
    return pl.BlockSpec((None,) + tuple(shape), lambda *_, _l=layer, _nd=nd: (_l,) + (0,) * _nd,
                        pipeline_mode=pl.Buffered(1))


def _mixer(x3d, bias, wts, layer, init, *, n_seq, n_chunk, nvalid):
    batch, seq_pad, _ = x3d.shape
    seq_rows = n_chunk * CHUNK
    n_tiles = seq_pad // seq_rows
    assert batch % n_seq == 0 and seq_pad % seq_rows == 0 and GATE_SLABS % n_chunk == 0
    rows = n_seq * seq_rows
    has_init = init is not None
    last_valid = nvalid - (n_tiles - 1) * seq_rows

    tile_map = lambda b, t: (b, t, 0)
    seq_map = lambda b, t: (b, 0, 0)
    cache_map = lambda b, t: (layer, b, 0, 0)
    spec = functools.partial(_layer_spec, layer=layer)
    in_specs = [
        pl.BlockSpec((n_seq, seq_rows, D_MODEL), tile_map),
        pl.BlockSpec(bias.shape, lambda b, t: (0, 0, 0), pipeline_mode=pl.Buffered(1)),
        pl.BlockSpec(memory_space=pltpu.SMEM),
        spec((1, D_MODEL)),
        spec((PACKED_BLOCKS, D_MODEL, MXU_WIDTH)),
        spec((GATE_BLOCKS, D_MODEL, MXU_WIDTH)),
        spec((CONV_W, CONV_DIM)),
        spec((1, CONV_DIM)),
        spec((1, LANES)),
        spec((1, LANES)),
        spec((1, SSD_INNER)),
        spec((1, SSD_INNER)),
        spec((ATT_WIDTH, D_MODEL)),
        spec((SSD_INNER, D_MODEL)),
        spec((D_MODEL, D_MODEL)),
    ]
    args = [x3d, bias, wts["sinks"], wts["g_mix"], wts["w_in"], wts["w_gates"], wts["conv_w"], wts["conv_b"],
            wts["dt_bias"], wts["a"], wts["d_skip"], wts["g_ssd"], wts["w_att_out"], wts["w_ssd_out"],
            wts["w_out"]]
    if has_init:
        in_specs += [
            pl.BlockSpec((None, n_seq, WINDOW, KV_WIDTH), cache_map),
            pl.BlockSpec((None, n_seq, WINDOW, KV_WIDTH), cache_map),
            pl.BlockSpec((None, n_seq, CONV_W - 1, CONV_DIM), cache_map),
            pl.BlockSpec((None, n_seq, SSD_INNER, SSD_STATE), cache_map),
        ]
        args += list(init)
    out_shape = [
        jax.ShapeDtypeStruct((batch, seq_pad, D_MODEL), F32),
        jax.ShapeDtypeStruct((batch, WINDOW, KV_WIDTH), F32),
        jax.ShapeDtypeStruct((batch, WINDOW, KV_WIDTH), F32),
        jax.ShapeDtypeStruct((batch, CONV_W - 1, CONV_DIM), F32),
        jax.ShapeDtypeStruct((batch, SSD_INNER, SSD_STATE), F32),
    ]
    out_specs = [
        pl.BlockSpec((n_seq, seq_rows, D_MODEL), tile_map),
        pl.BlockSpec((n_seq, WINDOW, KV_WIDTH), seq_map),
        pl.BlockSpec((n_seq, WINDOW, KV_WIDTH), seq_map),
        pl.BlockSpec((n_seq, CONV_W - 1, CONV_DIM), seq_map),
        pl.BlockSpec((n_seq, SSD_INNER, SSD_STATE), seq_map),
    ]
    scratch = [
        pltpu.VMEM((rows, D_MODEL), BF16),
        pltpu.VMEM((rows, ATT_WIDTH), F32),
        pltpu.VMEM((n_seq, WINDOW + seq_rows, 2 * KV_WIDTH), F32),
        pltpu.VMEM((XBC_BLOCKS, n_seq, HIST_ROWS + seq_rows, MXU_WIDTH), F32),
        pltpu.VMEM((XBC_BLOCKS, rows, MXU_WIDTH), F32),
        pltpu.VMEM((rows, LANES), F32),
        pltpu.VMEM((rows, ATT_WIDTH), F32),
        pltpu.VMEM((rows, SSD_INNER), F32),
        pltpu.VMEM((n_seq, SSD_STATE, SSD_INNER), F32),
        pltpu.VMEM((GATE_SLABS, rows, GATE_WIDTH // GATE_SLABS), F32),
    ]
    kern = functools.partial(_mixer_kernel, layer=layer, n_seq=n_seq, n_chunk=n_chunk, n_tiles=n_tiles,
                             nvalid=last_valid, has_init=has_init)
    return pl.pallas_call(
        kern,
        grid=(batch // n_seq, n_tiles),
        in_specs=in_specs,
        out_specs=out_specs,
        out_shape=out_shape,
        scratch_shapes=scratch,
        compiler_params=pltpu.CompilerParams(
            dimension_semantics=("arbitrary", "arbitrary"), vmem_limit_bytes=VMEM_LIMIT_BYTES),
        name="mixer_init" if has_init else "mixer",
    )(*args)


def _ffn_kernel(h_ref, g_ref, wg_ref, wu_ref, wd_ref, gfin_ref, o_ref, *, final):
    h = h_ref[...]
    hf = _rmsnorm(h, g_ref[...]).astype(BF16)
    act = (_silu(_dot(hf, wg_ref[...])) * _dot(hf, wu_ref[...])).astype(BF16)
    out = h + _dot(act, wd_ref[...])
    if final:
        out = _rmsnorm(out, gfin_ref[...])
    o_ref[...] = out


def _ffn(h2d, wts, layer, g_final, *, block_rows, final):
    n_rows = h2d.shape[0]
    assert n_rows % block_rows == 0
    spec = functools.partial(_layer_spec, layer=layer)
    return pl.pallas_call(
        functools.partial(_ffn_kernel, final=final),
        grid=(n_rows // block_rows,),
        in_specs=[
            pl.BlockSpec((block_rows, D_MODEL), lambda i: (i, 0)),
            spec((1, D_MODEL)),
            spec((D_MODEL, D_FF)),
            spec((D_MODEL, D_FF)),
            spec((D_FF, D_MODEL)),
            pl.BlockSpec((1, D_MODEL), lambda i: (0, 0), pipeline_mode=pl.Buffered(1)),
        ],
        out_specs=pl.BlockSpec((block_rows, D_MODEL), lambda i: (i, 0)),
        out_shape=jax.ShapeDtypeStruct((n_rows, D_MODEL), F32),
        compiler_params=pltpu.CompilerParams(
            dimension_semantics=("arbitrary",), vmem_limit_bytes=VMEM_LIMIT_BYTES),
        name="ffn_final" if final else "ffn",
    )(h2d, wts["g_ffn"], wts["w_gate"], wts["w_up"], wts["w_down"], g_final)


def _transpose_cast_kernel(src_ref, w_ref, o_ref, *, n_sub, keep_last):
    del src_ref
    t = w_ref[0].T
    if keep_last is not None:
        last = pl.program_id(1) == pl.num_programs(1) - 1
        lane = lax.broadcasted_iota(jnp.int32, t.shape, 1)
        t = jnp.where(lane < jnp.where(last, keep_last, t.shape[1]), t, 0.0)
    t = t.astype(BF16)
    for k in range(n_sub):
        o_ref[k] = t[:, k * MXU_WIDTH:(k + 1) * MXU_WIDTH]


def _transpose_cast(w_in_t, sources, n_sub, keep_last, name):
    depth, _, d = w_in_t.shape
    assert all(s % SUBLANES == 0 for s in sources)
    src = jnp.asarray([s // SUBLANES for s in sources], jnp.int32)
    grid_spec = pltpu.PrefetchScalarGridSpec(
        num_scalar_prefetch=1,
        grid=(depth, len(sources)),
        in_specs=[pl.BlockSpec((pl.Element(1), pl.Element(n_sub * MXU_WIDTH), pl.Element(d)),
                               lambda l, i, s: (l, s[i] * SUBLANES, 0))],
        out_specs=pl.BlockSpec((None, n_sub, d, MXU_WIDTH), lambda l, i, s: (l, i, 0, 0)),
    )
    return pl.pallas_call(
        functools.partial(_transpose_cast_kernel, n_sub=n_sub, keep_last=keep_last),
        grid_spec=grid_spec,
        out_shape=jax.ShapeDtypeStruct((depth, len(sources) * n_sub, d, MXU_WIDTH), BF16),
        compiler_params=pltpu.CompilerParams(dimension_semantics=("arbitrary", "arbitrary")),
        name=name,
    )(src, w_in_t)


def _pack_w_in(w_in):
    o_z = ATT_WIDTH + 2 * KV_WIDTH
    o_xbc = o_z + SSD_INNER
    o_dt = o_xbc + CONV_DIM
    o_ga = o_dt + SSD_HEADS
    o_gs = o_ga + D_MODEL
    w_in_t = jnp.swapaxes(w_in, 1, 2)
    packed = _transpose_cast(
        w_in_t, [*range(0, o_z, MXU_WIDTH), *range(o_xbc, o_dt, MXU_WIDTH), o_dt], 1, SSD_HEADS, "pack_w_in")
    gates = _transpose_cast(w_in_t, [o_z, o_ga, o_gs], D_MODEL // MXU_WIDTH, None, "pack_gates")
    return packed, gates


def _cast_kernel(w_ref, o_ref):
    o_ref[...] = w_ref[...].astype(BF16)


def _cast_bf16(w):
    depth, r, c = w.shape
    n_rows = depth * r
    block_rows = max(b for b in range(2 * SUBLANES, n_rows + 1, 2 * SUBLANES)
                     if n_rows % b == 0 and (b == 2 * SUBLANES or (n_rows // b >= CAST_MIN_STEPS
                                                                   and b * c * 4 <= CAST_BLOCK_BYTES)))
    out = pl.pallas_call(
        _cast_kernel,
        grid=(n_rows // block_rows,),
        in_specs=[pl.BlockSpec((block_rows, c), lambda i: (i, 0))],
        out_specs=pl.BlockSpec((block_rows, c), lambda i: (i, 0)),
        out_shape=jax.ShapeDtypeStruct((n_rows, c), BF16),
        compiler_params=pltpu.CompilerParams(dimension_semantics=("arbitrary",),
                                             vmem_limit_bytes=VMEM_LIMIT_BYTES),
        name="cast_bf16",
    )(w.reshape(n_rows, c))
    return out.reshape(depth, r, c)


def _prepare_weights(g_mix, w_in, conv_w, conv_b, dt_bias, a_log, d_skip, g_ssd, sinks, w_att_out, w_ssd_out,
                     w_out, g_ffn, w_gate, w_up, w_down):
    pad_h = ((0, 0), (0, LANES - SSD_HEADS))
    w_packed, w_gates = _pack_w_in(w_in)
    return {
        "sinks": sinks,
        "g_mix": g_mix[:, None, :],
        "w_in": w_packed,
        "w_gates": w_gates,
        "conv_w": conv_w,
        "conv_b": conv_b[:, None, :],
        "dt_bias": jnp.pad(dt_bias, pad_h)[:, None, :],
        "a": jnp.pad(-jnp.exp(a_log), pad_h)[:, None, :],
        "d_skip": jnp.repeat(d_skip, SSD_HEAD_DIM, axis=1)[:, None, :],
        "g_ssd": g_ssd[:, None, :],
        "w_att_out": _cast_bf16(w_att_out),
        "w_ssd_out": _cast_bf16(w_ssd_out),
        "w_out": _cast_bf16(w_out),
        "g_ffn": g_ffn[:, None, :],
        "w_gate": _cast_bf16(w_gate),
        "w_up": _cast_bf16(w_up),
        "w_down": _cast_bf16(w_down),
    }


CAST_BLOCK_BYTES = 8 * 1024 * 1024
CAST_MIN_STEPS = 4
PROMPT_SEQS_PER_TILE = 4
PROMPT_CHUNKS_PER_TILE = 2
SAMPLE_SEQS_PER_TILE = 4
FFN_BLOCK_ROWS = 1024


def kernel(x_prompt, x_sample, cache_k, cache_v, state_conv, state_ssm, rel_table, g_mix, w_in, conv_w, conv_b, dt_bias, a_log, d_skip, g_ssd, sinks, w_att_out, w_ssd_out, w_out, g_ffn, w_gate, w_up, w_down, g_final):
    depth = w_in.shape[0]
    bp, sp, _ = x_prompt.shape
    bs, ts, _ = x_sample.shape
    kv_len = cache_k.shape[2]
    assert kv_len == WINDOW and ts <= CHUNK and ts % 8 == 0 and ts >= CONV_W - 1

    bias = _blocked_bias(rel_table)
    g_fin = g_final[None, :]

    wts = _prepare_weights(g_mix, w_in, conv_w, conv_b, dt_bias, a_log, d_skip, g_ssd, sinks, w_att_out,
                           w_ssd_out, w_out, g_ffn, w_gate, w_up, w_down)
    init = (cache_k.reshape(depth, bs, WINDOW, KV_WIDTH), cache_v.reshape(depth, bs, WINDOW, KV_WIDTH),
            state_conv, state_ssm.reshape(depth, bs, SSD_INNER, SSD_STATE))

    xp = x_prompt
    xs = x_sample.reshape(bs * ts, D_MODEL)
    st_p, st_s = [], []
    for l in range(depth):
        final = l == depth - 1
        hp, *state_p = _mixer(xp, bias, wts, l, None, n_seq=PROMPT_SEQS_PER_TILE,
                              n_chunk=PROMPT_CHUNKS_PER_TILE, nvalid=sp)
        xp = _ffn(hp.reshape(bp * sp, D_MODEL), wts, l, g_fin, block_rows=FFN_BLOCK_ROWS,
                  final=final).reshape(bp, sp, D_MODEL)
        st_p.append(state_p)
        xs_pad = jnp.pad(xs.reshape(bs, ts, D_MODEL), ((0, 0), (0, CHUNK - ts), (0, 0)))
        hs, *state_s = _mixer(xs_pad, bias, wts, l, init, n_seq=SAMPLE_SEQS_PER_TILE, n_chunk=1, nvalid=ts)
        xs = _ffn(hs[:, :ts].reshape(bs * ts, D_MODEL), wts, l, g_fin, block_rows=bs * ts, final=final)
        st_s.append(state_s)

    def states(sts, b):
        k, v, conv, ssm = (jnp.stack(leaves) for leaves in zip(*sts))
        return (k.reshape(depth, b, WINDOW, N_KV_HEADS, HEAD_DIM),
                v.reshape(depth, b, WINDOW, N_KV_HEADS, HEAD_DIM), conv,
                ssm.reshape(depth, b, SSD_HEADS, SSD_HEAD_DIM, SSD_STATE))

    return (xp, xs.reshape(x_sample.shape), *states(st_p, bp), *states(st_s, bs))
```

```python
import functools
import math

import jax
import jax.numpy as jnp
from jax import lax
from jax.experimental import pallas as pl
from jax.experimental.pallas import tpu as pltpu

D_MODEL = 1024
CHUNK = 64
EPS = 1e-6
NEG_INF = -1e30
N_HEADS = 8
N_KV_HEADS = 2
HEAD_DIM = 64
ATT_WIDTH = N_HEADS * HEAD_DIM
KV_WIDTH = N_KV_HEADS * HEAD_DIM
WINDOW = 128
N_KEYS = WINDOW + CHUNK
NUM_BUCKETS = 32
MAX_DISTANCE = 128
SSD_INNER = 1024
SSD_HEADS = 16
SSD_HEAD_DIM = 64
SSD_GROUPS = 2
SSD_STATE = 128
GROUP_WIDTH = SSD_INNER // SSD_GROUPS
CONV_W = 4
CONV_DIM = SSD_INNER + 2 * SSD_GROUPS * SSD_STATE
D_FF = 2816
LANES = 128
SUBLANES = 8
HIST_ROWS = 8
MXU_WIDTH = 256
XBC_BLOCKS = CONV_DIM // MXU_WIDTH
FILL_POINTS = 7
FILL_FIRST_PHASE = 3

OFF_Q = 0
OFF_KV = OFF_Q + ATT_WIDTH
OFF_XBC = OFF_KV + 2 * KV_WIDTH
OFF_DT = OFF_XBC + CONV_DIM
IN_PACKED = OFF_DT + LANES
GATE_Z = 0
GATE_A = GATE_Z + SSD_INNER
GATE_S = GATE_A + D_MODEL
GATE_WIDTH = GATE_S + D_MODEL
GATE_SLABS = 2
PACK_GROUP = 3
PACKED_BLOCKS = PACK_GROUP * -(-IN_PACKED // (PACK_GROUP * MXU_WIDTH))
GATE_BLOCKS = GATE_WIDTH // MXU_WIDTH

VMEM_LIMIT_BYTES = 60 * 1024 * 1024

F32 = jnp.float32
BF16 = jnp.bfloat16


def _dot(a, b):
    return jnp.dot(a, b, preferred_element_type=F32)


def _dot_nt(a, b):
    return lax.dot_general(a, b, (((1,), (1,)), ((), ())), preferred_element_type=F32)


def _split3(x):
    hi = x.astype(BF16)
    r1 = x - hi.astype(F32)
    mid = r1.astype(BF16)
    lo = (r1 - mid.astype(F32)).astype(BF16)
    return hi, mid, lo


def _rmsnorm(x, g):
    return x * lax.rsqrt(jnp.mean(x * x, axis=-1, keepdims=True) + EPS) * g


def _sigmoid(x):
    return 1.0 / (1.0 + jnp.exp(-x))


def _silu(x):
    return x * _sigmoid(x)


def _softplus(x):
    return jnp.maximum(x, 0.0) + jnp.log1p(jnp.exp(-jnp.abs(x)))


def _bias_kernel(table_ref, bucket_ref, o_ref):
    bucket = bucket_ref[...]
    rows = lax.broadcasted_iota(jnp.int32, bucket.shape, 0)
    cols = lax.broadcasted_iota(jnp.int32, bucket.shape, 1)
    row_hi = rows >= CHUNK
    col_hi = (cols & HEAD_DIM) != 0
    for g in range(N_KV_HEADS):
        acc = jnp.zeros(bucket.shape, F32)
        for b in range(NUM_BUCKETS):
            t0 = table_ref[b, 4 * g + 0]
            t1 = table_ref[b, 4 * g + 1]
            t2 = table_ref[b, 4 * g + 2]
            t3 = table_ref[b, 4 * g + 3]
            tv = jnp.where(row_hi, jnp.where(col_hi, t3, t2), jnp.where(col_hi, t1, t0))
            acc = jnp.where(bucket == b, tv, acc)
        o_ref[g] = acc


def _t5_bucket(rel):
    nb = NUM_BUCKETS // 2
    max_exact = nb // 2
    ret = jnp.where(rel > 0, nb, 0)
    n = jnp.abs(rel)
    nf = jnp.maximum(n, 1).astype(jnp.float32)
    large = max_exact + (jnp.log(nf / max_exact) / math.log(MAX_DISTANCE / max_exact)
                         * (nb - max_exact)).astype(jnp.int32)
    large = jnp.minimum(large, nb - 1)
    return ret + jnp.where(n < max_exact, n, large)


def _blocked_bias(rel_table):
    qi = jnp.arange(CHUNK, dtype=jnp.int32)
    kj = jnp.arange(N_KEYS, dtype=jnp.int32)
    bucket = _t5_bucket(kj[None, :] - WINDOW - qi[:, None]).astype(jnp.int32)
    bucket = jnp.tile(bucket.reshape(CHUNK, N_KEYS // CHUNK, 1, CHUNK), (2, 1, 2, 1)).reshape(2 * CHUNK, 2 * N_KEYS)
    return pl.pallas_call(
        _bias_kernel,
        out_shape=jax.ShapeDtypeStruct((N_KV_HEADS, 2 * CHUNK, 2 * N_KEYS), F32),
        in_specs=[pl.BlockSpec(memory_space=pltpu.SMEM),
                  pl.BlockSpec(memory_space=pltpu.VMEM)],
        out_specs=pl.BlockSpec(memory_space=pltpu.VMEM),
        name="rel_bias",
    )(rel_table, bucket)


def _blocked_kv(win, g):
    lo_half = lax.broadcasted_iota(jnp.int32, (N_KEYS, LANES), 1) < HEAD_DIM
    if g == 0:
        a0 = jnp.where(lo_half, win, 0.0)
        a1 = pltpu.roll(a0, HEAD_DIM, axis=1)
    else:
        a1 = jnp.where(lo_half, 0.0, win)
        a0 = pltpu.roll(a1, HEAD_DIM, axis=1)
    pieces = []
    for r in range(0, N_KEYS, CHUNK):
        pieces += [a0[r:r + CHUNK], a1[r:r + CHUNK]]
    return jnp.concatenate(pieces, axis=0).astype(BF16)


def _attention_logits(q, kwin, bias_ref, g):
    qs = jnp.concatenate([q[:, 256 * g:256 * g + LANES],
                          q[:, 256 * g + LANES:256 * (g + 1)]], axis=0).astype(BF16)
    return _dot_nt(qs, _blocked_kv(kwin, g)) + bias_ref[g]


def _attention_finish(logits, vwin, sinks_ref, valid, g):
    row_lo = lax.broadcasted_iota(jnp.int32, (2 * CHUNK, 1), 0) < CHUNK
    even = lax.broadcasted_iota(jnp.int32, (2 * CHUNK, LANES), 1) < HEAD_DIM
    logits = jnp.where(valid, logits, NEG_INF)
    tiles = [logits[:, c:c + LANES] for c in range(0, 2 * N_KEYS, LANES)]
    s0 = jnp.where(row_lo, sinks_ref[4 * g + 0], sinks_ref[4 * g + 2])
    s1 = jnp.where(row_lo, sinks_ref[4 * g + 1], sinks_ref[4 * g + 3])
    tmax = functools.reduce(jnp.maximum, tiles)
    m0 = jnp.maximum(jnp.max(jnp.where(even, tmax, -jnp.inf), axis=1, keepdims=True), s0)
    m1 = jnp.maximum(jnp.max(jnp.where(even, -jnp.inf, tmax), axis=1, keepdims=True), s1)
    m = jnp.where(even, m0, m1)
    p = [jnp.exp(tile - m) for tile in tiles]
    psum = functools.reduce(jnp.add, p)
    d0 = jnp.sum(jnp.where(even, psum, 0.0), axis=1, keepdims=True) + jnp.exp(s0 - m0)
    d1 = jnp.sum(jnp.where(even, 0.0, psum), axis=1, keepdims=True) + jnp.exp(s1 - m1)
    o = _dot(jnp.concatenate(p, axis=1).astype(BF16), _blocked_kv(vwin, g)) / jnp.where(even, d0, d1)
    return jnp.concatenate([o[:CHUNK], o[CHUNK:]], axis=1)


def _ssd_constants(n_blocks):
    r = lax.broadcasted_iota(jnp.int32, (n_blocks * CHUNK, n_blocks * CHUNK), 0)
    c = lax.broadcasted_iota(jnp.int32, (n_blocks * CHUNK, n_blocks * CHUNK), 1)
    tri = jnp.where((c <= r) & ((c >> 6) == (r >> 6)), 1.0, 0.0).astype(BF16)
    hrow = lax.broadcasted_iota(jnp.int32, (LANES, SSD_INNER), 0)
    hcol = lax.broadcasted_iota(jnp.int32, (LANES, SSD_INNER), 1)
    expand = jnp.where(hcol >> 6 == hrow, 1.0, 0.0).astype(BF16)
    return tri, expand


def _ssd_prepare(chunks, dt_raws, dtb_ref, a_ref, nvalid, tri, expand):
    n = len(chunks)
    dt = _softplus(jnp.concatenate(dt_raws, axis=0) + dtb_ref[...])
    if nvalid < CHUNK:
        rows = lax.broadcasted_iota(jnp.int32, dt.shape, 0) & (CHUNK - 1)
        dt = jnp.where(rows < nvalid, dt, 0.0)
    acum3 = _dot(tri, jnp.concatenate(_split3(dt * a_ref[...]), axis=1))
    acum = acum3[:, :LANES] + acum3[:, LANES:2 * LANES] + acum3[:, 2 * LANES:]
    a_hi, a_mid, _ = _split3(acum)
    wide = _dot(jnp.concatenate([dt.astype(BF16), a_hi, a_mid], axis=0), expand)
    m = n * CHUNK
    for i, c in enumerate(chunks):
        sl = slice(i * CHUNK, (i + 1) * CHUNK)
        c["dt_exp"] = wide[sl]
        c["a_col"] = wide[m + i * CHUNK:m + (i + 1) * CHUNK] + wide[2 * m + i * CHUNK:2 * m + (i + 1) * CHUNK]


def _ssd_decay(c):
    a_col = c["a_col"]
    a_last = a_col[CHUNK - 1:CHUNK, :]
    c["xs"] = c["xbc_act"][:, :SSD_INNER]
    c["xdt"] = c["xs"] * c.pop("dt_exp")
    c["xw"] = c["xdt"] * jnp.exp(a_last - a_col)
    c["e_col"] = jnp.exp(a_col)
    c["e_last"] = jnp.exp(a_last)


def _ssd_group(c, gq, ht_ref, g_idx):
    l2 = lax.broadcasted_iota(jnp.int32, (CHUNK, LANES), 0)
    j2 = lax.broadcasted_iota(jnp.int32, (CHUNK, LANES), 1)
    s2 = j2 & (CHUNK - 1)
    diag_sel = jnp.where(s2 == l2, 1.0, 0.0)
    causal2 = s2 <= l2
    lane_lo = j2 < SSD_HEAD_DIM

    b0 = SSD_INNER + SSD_STATE * gq
    c0 = SSD_INNER + SSD_GROUPS * SSD_STATE + SSD_STATE * gq
    bg = c["xbc_act"][:, b0:b0 + SSD_STATE]
    cg = c["xbc_act"][:, c0:c0 + SSD_STATE].astype(BF16)
    bg_bf = bg.astype(BF16)
    cb2 = _dot_nt(cg, jnp.concatenate([bg_bf, bg_bf], axis=0))
    gsl = slice(GROUP_WIDTH * gq, GROUP_WIDTH * (gq + 1))
    h_prev = ht_ref[g_idx, :, gsl]
    y_off = _dot(cg, h_prev.astype(BF16)) * c["e_col"][:, gsl]
    yd = []
    for i in range(GROUP_WIDTH // LANES):
        psl = slice(GROUP_WIDTH * gq + LANES * i, GROUP_WIDTH * gq + LANES * (i + 1))
        ac = c["a_col"][:, psl]
        a_row = jnp.sum(ac * diag_sel, axis=0, keepdims=True)
        lmat = jnp.exp(jnp.where(causal2, ac - a_row, -jnp.inf))
        m2 = (cb2 * lmat).astype(BF16)
        xp = c["xdt"][:, psl]
        xblk = jnp.concatenate([jnp.where(lane_lo, xp, 0.0), jnp.where(lane_lo, 0.0, xp)],
                               axis=0).astype(BF16)
        yd.append(_dot(m2, xblk))
    c.setdefault("ys", []).append(jnp.concatenate(yd, axis=1) + y_off)
    st = _dot(bg.T.astype(BF16), c["xw"][:, gsl].astype(BF16))
    ht_ref[g_idx, :, gsl] = h_prev * c["e_last"][:, gsl] + st


class _LayerSinks:
    def __init__(self, ref, layer):
        self.ref, self.layer = ref, layer

    def __getitem__(self, head):
        return self.ref[self.layer, head]


def _mixer_kernel(*refs, layer, n_seq, n_chunk, n_tiles, nvalid, has_init):
    it = iter(refs)
    x_ref = next(it)
    bias_ref = next(it)
    sinks_ref = _LayerSinks(next(it), layer)
    gmix_ref = next(it)
    win_ref = next(it)
    wgate_ref = next(it)
    convw_ref = next(it)
    convb_ref = next(it)
    dtb_ref = next(it)
    a_ref = next(it)
    dskip_ref = next(it)
    gssd_ref = next(it)
    watt_ref = next(it)
    wssd_ref = next(it)
    wout_ref = next(it)
    if has_init:
        ki_ref = next(it)
        vi_ref = next(it)
        convi_ref = next(it)
        ssmi_ref = next(it)
    h_ref = next(it)
    ko_ref = next(it)
    vo_ref = next(it)
    convo_ref = next(it)
    ssmo_ref = next(it)
    hn_s = next(it)
    q_s = next(it)
    kv_s = next(it)
    xbc_s = next(it)
    xact_s = next(it)
    dt_s = next(it)
    oatt_s = next(it)
    y_s = next(it)
    ht_s = next(it)
    gate_s = next(it)

    t = pl.program_id(1)
    seq_rows = n_chunk * CHUNK

    @pl.when(t == 0)
    def _init():
        if has_init:
            kv_s[:, 0:WINDOW, :KV_WIDTH] = ki_ref[...]
            kv_s[:, 0:WINDOW, KV_WIDTH:] = vi_ref[...]
            xbc_s[:, :, 0:HIST_ROWS, :] = jnp.zeros((XBC_BLOCKS, n_seq, HIST_ROWS, MXU_WIDTH), F32)
            for cb in range(XBC_BLOCKS):
                xbc_s[cb, :, HIST_ROWS - (CONV_W - 1):HIST_ROWS, :] = convi_ref[
                    :, :, cb * MXU_WIDTH:(cb + 1) * MXU_WIDTH]
            for g in range(n_seq):
                ht_s[g] = ssmi_ref[g].T
        else:
            kv_s[:, 0:WINDOW, :] = jnp.zeros((n_seq, WINDOW, 2 * KV_WIDTH), F32)
            xbc_s[:, :, 0:HIST_ROWS, :] = jnp.zeros((XBC_BLOCKS, n_seq, HIST_ROWS, MXU_WIDTH), F32)
            ht_s[...] = jnp.zeros(ht_s.shape, F32)

    rows = n_seq * seq_rows
    hn = _rmsnorm(x_ref[...].reshape(rows, D_MODEL), gmix_ref[...]).astype(BF16)
    hn_s[...] = hn

    def project_block(cb):
        c0 = OFF_XBC + cb * MXU_WIDTH
        blk = _dot(hn, win_ref[c0 // MXU_WIDTH])
        for g in range(n_seq):
            xbc_s[cb, g, HIST_ROWS:HIST_ROWS + seq_rows, :] = blk[g * seq_rows:(g + 1) * seq_rows]

    def conv_block(cb):
        csl = slice(cb * MXU_WIDTH, (cb + 1) * MXU_WIDTH)
        for g in range(n_seq):
            conv = convb_ref[:, csl] + convw_ref[CONV_W - 1:CONV_W, csl] * xbc_s[cb, g, HIST_ROWS:HIST_ROWS + seq_rows, :]
            for i in range(1, CONV_W):
                conv = conv + (convw_ref[CONV_W - 1 - i:CONV_W - i, csl]
                               * xbc_s[cb, g, HIST_ROWS - i:HIST_ROWS - i + seq_rows, :])
            xact_s[cb, g * seq_rows:(g + 1) * seq_rows, :] = _silu(conv)

    project_block(0)
    for cb in range(1, XBC_BLOCKS):
        project_block(cb)
        conv_block(cb - 1)
    for c0 in range(OFF_Q, OFF_KV, MXU_WIDTH):
        q_s[:, c0:c0 + MXU_WIDTH] = _dot(hn, win_ref[c0 // MXU_WIDTH]) * (HEAD_DIM ** -0.5)
    conv_block(XBC_BLOCKS - 1)
    kv = _dot(hn, win_ref[OFF_KV // MXU_WIDTH])
    for g in range(n_seq):
        kv_s[g, WINDOW:WINDOW + seq_rows, :] = kv[g * seq_rows:(g + 1) * seq_rows]
    dt_s[...] = _dot(hn, win_ref[OFF_DT // MXU_WIDTH])[:, :LANES]

    cols = lax.broadcasted_iota(jnp.int32, (2 * CHUNK, 2 * N_KEYS), 1)
    colmod = ((cols >> 7) << 6) + (cols & (CHUNK - 1))
    tri, expand = _ssd_constants(n_seq)

    slab = GATE_WIDTH // GATE_SLABS
    slabs_per_iter = GATE_SLABS // n_chunk

    def iter_body(j, carry):
        k0 = pl.multiple_of(j * CHUNK, CHUNK) if n_chunk > 1 else 0
        per_slab = slab // MXU_WIDTH
        n_pieces = slabs_per_iter * per_slab
        n_points = FILL_POINTS * n_seq
        calls = [0]

        first = FILL_FIRST_PHASE * n_seq
        span = n_points - first

        def fill():
            k = calls[0] - first
            calls[0] += 1
            if k < 0:
                return
            for p in range(-(-k * n_pieces // span), -(-(k + 1) * n_pieces // span)):
                s = j * slabs_per_iter + p // per_slab
                c0 = (p % per_slab) * MXU_WIDTH
                gate_s[s, :, c0:c0 + MXU_WIDTH] = _dot(hn_s[...], wgate_ref[s * per_slab + c0 // MXU_WIDTH])

        if has_init:
            valid = colmod < WINDOW + nvalid
        else:
            first_valid = jnp.maximum(0, (2 - (t * n_chunk + j)) * CHUNK)
            valid = colmod >= first_valid
        chunks = [{"g": g, "rows": pl.ds(pl.multiple_of(g * seq_rows + k0, CHUNK), CHUNK)}
                  for g in range(n_seq)]
        for c in chunks:
            kvwin = kv_s[c["g"], pl.ds(k0, N_KEYS), :]
            c["vwin"] = kvwin[:, KV_WIDTH:]
            q = q_s[c["rows"], :]
            c["logits"] = [_attention_logits(q, kvwin[:, :KV_WIDTH], bias_ref, hg) for hg in range(N_KV_HEADS)]
            fill()
        for c in chunks:
            c["xbc_act"] = jnp.concatenate([xact_s[cb, c["rows"], :] for cb in range(XBC_BLOCKS)], axis=1)
            fill()
        for c in chunks:
            outs = [_attention_finish(c["logits"][hg], c["vwin"], sinks_ref, valid, hg)
                    for hg in range(N_KV_HEADS)]
            oatt_s[c["rows"], :] = jnp.concatenate(outs, axis=1)
            del c["logits"], c["vwin"]
            fill()
        _ssd_prepare(chunks, [dt_s[c["rows"], :] for c in chunks], dtb_ref, a_ref, nvalid, tri, expand)
        for _ in chunks:
            fill()
        for c in chunks:
            _ssd_decay(c)
            fill()
        for gq in range(SSD_GROUPS):
            for c in chunks:
                _ssd_group(c, gq, ht_s, c["g"])
                fill()
        for c in chunks:
            y_s[c["rows"], :] = jnp.concatenate(c["ys"], axis=1) + c["xs"] * dskip_ref[...]
        assert calls[0] == n_points
        return carry

    for j in range(n_chunk):
        iter_body(j, 0)

    def gate_cols(lo, hi):
        pieces = []
        while lo < hi:
            s, off = divmod(lo, slab)
            take = min(hi - lo, slab - off)
            pieces.append(gate_s[s, :, off:off + take])
            lo += take
        return jnp.concatenate(pieces, axis=1)

    y_ssd = _rmsnorm(y_s[...] * _silu(gate_cols(GATE_Z, GATE_A)), gssd_ref[...]).astype(BF16)
    merged = (_sigmoid(gate_cols(GATE_A, GATE_S)) * _dot(oatt_s[...].astype(BF16), watt_ref[...])
              + _sigmoid(gate_cols(GATE_S, GATE_WIDTH)) * _dot(y_ssd, wssd_ref[...]))
    h = x_ref[...].reshape(rows, D_MODEL) + _dot(merged.astype(BF16), wout_ref[...])
    h_ref[...] = h.reshape(n_seq, seq_rows, D_MODEL)

    @pl.when(t == n_tiles - 1)
    def _emit_states():
        ko_ref[...] = kv_s[:, nvalid:nvalid + WINDOW, :KV_WIDTH]
        vo_ref[...] = kv_s[:, nvalid:nvalid + WINDOW, KV_WIDTH:]
        last = HIST_ROWS + nvalid
        convo_ref[...] = jnp.concatenate(
            [xbc_s[cb, :, last - (CONV_W - 1):last, :] for cb in range(XBC_BLOCKS)], axis=-1)
        for g in range(n_seq):
            ssmo_ref[g] = ht_s[g].T

    if n_tiles > 1:
        kv_s[:, 0:WINDOW, :] = kv_s[:, seq_rows:seq_rows + WINDOW, :]
        xbc_s[:, :, 0:HIST_ROWS, :] = xbc_s[:, :, seq_rows:seq_rows + HIST_ROWS, :]


def _layer_spec(shape, layer):
    nd = len(shape)
    return pl.BlockSpec((None,) + tuple(shape), lambda *_, _l=layer, _nd=nd: (_l,) + (0,) * _nd,
                        pipeline_mode=pl.Buffered(1))


def _mixer(x3d, bias, wts, layer, init, *, n_seq, n_chunk, nvalid):
    batch, seq_pad, _ = x3d.shape
    seq_rows = n_chunk * CHUNK
    n_tiles = seq_pad // seq_rows
    assert batch % n_seq == 0 and seq_pad % seq_rows == 0 and GATE_SLABS % n_chunk == 0
    rows = n_seq * seq_rows
    has_init = init is not None
    last_valid = nvalid - (n_tiles - 1) * seq_rows

    tile_map = lambda b, t: (b, t, 0)
    seq_map = lambda b, t: (b, 0, 0)
    cache_map = lambda b, t: (layer, b, 0, 0)
    spec = functools.partial(_layer_spec, layer=layer)
    in_specs = [
        pl.BlockSpec((n_seq, seq_rows, D_MODEL), tile_map),
        pl.BlockSpec(bias.shape, lambda b, t: (0, 0, 0), pipeline_mode=pl.Buffered(1)),
        pl.BlockSpec(memory_space=pltpu.SMEM),
        spec((1, D_MODEL)),
        spec((PACKED_BLOCKS, D_MODEL, MXU_WIDTH)),
        spec((GATE_BLOCKS, D_MODEL, MXU_WIDTH)),
        spec((CONV_W, CONV_DIM)),
        spec((1, CONV_DIM)),
        spec((1, LANES)),
        spec((1, LANES)),
        spec((1, SSD_INNER)),
        spec((1, SSD_INNER)),
        spec((ATT_WIDTH, D_MODEL)),
        spec((SSD_INNER, D_MODEL)),
        spec((D_MODEL, D_MODEL)),
    ]
    args = [x3d, bias, wts["sinks"], wts["g_mix"], wts["w_in"], wts["w_gates"], wts["conv_w"], wts["conv_b"],
            wts["dt_bias"], wts["a"], wts["d_skip"], wts["g_ssd"], wts["w_att_out"], wts["w_ssd_out"],
            wts["w_out"]]
    if has_init:
        in_specs += [
            pl.BlockSpec((None, n_seq, WINDOW, KV_WIDTH), cache_map),
            pl.BlockSpec((None, n_seq, WINDOW, KV_WIDTH), cache_map),
            pl.BlockSpec((None, n_seq, CONV_W - 1, CONV_DIM), cache_map),
            pl.BlockSpec((None, n_seq, SSD_INNER, SSD_STATE), cache_map),
        ]
        args += list(init)
    out_shape = [
        jax.ShapeDtypeStruct((batch, seq_pad, D_MODEL), F32),
        jax.ShapeDtypeStruct((batch, WINDOW, KV_WIDTH), F32),
        jax.ShapeDtypeStruct((batch, WINDOW, KV_WIDTH), F32),
        jax.ShapeDtypeStruct((batch, CONV_W - 1, CONV_DIM), F32),
        jax.ShapeDtypeStruct((batch, SSD_INNER, SSD_STATE), F32),
    ]
    out_specs = [
        pl.BlockSpec((n_seq, seq_rows, D_MODEL), tile_map),
        pl.BlockSpec((n_seq, WINDOW, KV_WIDTH), seq_map),
        pl.BlockSpec((n_seq, WINDOW, KV_WIDTH), seq_map),
        pl.BlockSpec((n_seq, CONV_W - 1, CONV_DIM), seq_map),
        pl.BlockSpec((n_seq, SSD_INNER, SSD_STATE), seq_map),
    ]
    scratch = [
        pltpu.VMEM((rows, D_MODEL), BF16),
        pltpu.VMEM((rows, ATT_WIDTH), F32),
        pltpu.VMEM((n_seq, WINDOW + seq_rows, 2 * KV_WIDTH), F32),
        pltpu.VMEM((XBC_BLOCKS, n_seq, HIST_ROWS + seq_rows, MXU_WIDTH), F32),
        pltpu.VMEM((XBC_BLOCKS, rows, MXU_WIDTH), F32),
        pltpu.VMEM((rows, LANES), F32),
        pltpu.VMEM((rows, ATT_WIDTH), F32),
        pltpu.VMEM((rows, SSD_INNER), F32),
        pltpu.VMEM((n_seq, SSD_STATE, SSD_INNER), F32),
        pltpu.VMEM((GATE_SLABS, rows, GATE_WIDTH // GATE_SLABS), F32),
    ]
    kern = functools.partial(_mixer_kernel, layer=layer, n_seq=n_seq, n_chunk=n_chunk, n_tiles=n_tiles,
                             nvalid=last_valid, has_init=has_init)
    return pl.pallas_call(
        kern,
        grid=(batch // n_seq, n_tiles),
        in_specs=in_specs,
        out_specs=out_specs,
        out_shape=out_shape,
        scratch_shapes=scratch,
        compiler_params=pltpu.CompilerParams(
            dimension_semantics=("arbitrary", "arbitrary"), vmem_limit_bytes=VMEM_LIMIT_BYTES),
        name="mixer_init" if has_init else "mixer",
    )(*args)


def _ffn_kernel(h_ref, g_ref, wg_ref, wu_ref, wd_ref, gfin_ref, o_ref, *, final):
    h = h_ref[...]
    hf = _rmsnorm(h, g_ref[...]).astype(BF16)
    act = (_silu(_dot(hf, wg_ref[...])) * _dot(hf, wu_ref[...])).astype(BF16)
    out = h + _dot(act, wd_ref[...])
    if final:
        out = _rmsnorm(out, gfin_ref[...])
    o_ref[...] = out


def _ffn(h2d, wts, layer, g_final, *, block_rows, final):
    n_rows = h2d.shape[0]
    assert n_rows % block_rows == 0
    spec = functools.partial(_layer_spec, layer=layer)
    return pl.pallas_call(
        functools.partial(_ffn_kernel, final=final),
        grid=(n_rows // block_rows,),
        in_specs=[
            pl.BlockSpec((block_rows, D_MODEL), lambda i: (i, 0)),
            spec((1, D_MODEL)),
            spec((D_MODEL, D_FF)),
            spec((D_MODEL, D_FF)),
            spec((D_FF, D_MODEL)),
            pl.BlockSpec((1, D_MODEL), lambda i: (0, 0), pipeline_mode=pl.Buffered(1)),
        ],
        out_specs=pl.BlockSpec((block_rows, D_MODEL), lambda i: (i, 0)),
        out_shape=jax.ShapeDtypeStruct((n_rows, D_MODEL), F32),
        compiler_params=pltpu.CompilerParams(
            dimension_semantics=("arbitrary",), vmem_limit_bytes=VMEM_LIMIT_BYTES),
        name="ffn_final" if final else "ffn",
    )(h2d, wts["g_ffn"], wts["w_gate"], wts["w_up"], wts["w_down"], g_final)


def _transpose_cast_kernel(src_ref, w_ref, o_ref, *, n_sub, keep_last):
    del src_ref
    t = w_ref[0].T
    if keep_last is not None:
        last = pl.program_id(1) == pl.num_programs(1) - 1
        lane = lax.broadcasted_iota(jnp.int32, t.shape, 1)
        t = jnp.where(lane < jnp.where(last, keep_last, t.shape[1]), t, 0.0)
    t = t.astype(BF16)
    for k in range(n_sub):
        o_ref[k] = t[:, k * MXU_WIDTH:(k + 1) * MXU_WIDTH]


def _transpose_cast(w_in_t, sources, n_sub, keep_last, name):
    depth, _, d = w_in_t.shape
    assert all(s % SUBLANES == 0 for s in sources)
    src = jnp.asarray([s // SUBLANES for s in sources], jnp.int32)
    grid_spec = pltpu.PrefetchScalarGridSpec(
        num_scalar_prefetch=1,
        grid=(depth, len(sources)),
        in_specs=[pl.BlockSpec((pl.Element(1), pl.Element(n_sub * MXU_WIDTH), pl.Element(d)),
                               lambda l, i, s: (l, s[i] * SUBLANES, 0))],
        out_specs=pl.BlockSpec((None, n_sub, d, MXU_WIDTH), lambda l, i, s: (l, i, 0, 0)),
    )
    return pl.pallas_call(
        functools.partial(_transpose_cast_kernel, n_sub=n_sub, keep_last=keep_last),
        grid_spec=grid_spec,
        out_shape=jax.ShapeDtypeStruct((depth, len(sources) * n_sub, d, MXU_WIDTH), BF16),
        compiler_params=pltpu.CompilerParams(dimension_semantics=("arbitrary", "arbitrary")),
        name=name,
    )(src, w_in_t)


def _pack_w_in(w_in):
    o_z = ATT_WIDTH + 2 * KV_WIDTH
    o_xbc = o_z + SSD_INNER
    o_dt = o_xbc + CONV_DIM
    o_ga = o_dt + SSD_HEADS
    o_gs = o_ga + D_MODEL
    w_in_t = jnp.swapaxes(w_in, 1, 2)
    group = PACK_GROUP * MXU_WIDTH
    assert o_z == group and CONV_DIM % group == 0 and PACKED_BLOCKS == (o_z + CONV_DIM + group) // MXU_WIDTH
    packed = _transpose_cast(w_in_t, [0, *range(o_xbc, o_dt, group), o_dt], PACK_GROUP, SSD_HEADS, "pack_w_in")
    gates = _transpose_cast(w_in_t, [o_z, o_ga, o_gs], D_MODEL // MXU_WIDTH, None, "pack_gates")
    return packed, gates


def _cast_kernel(w_ref, o_ref):
    o_ref[...] = w_ref[...].astype(BF16)


def _cast_bf16(w):
    depth, r, c = w.shape
    n_rows = depth * r
    block_rows = max(b for b in range(2 * SUBLANES, n_rows + 1, 2 * SUBLANES)
                     if n_rows % b == 0 and (b == 2 * SUBLANES or (n_rows // b >= CAST_MIN_STEPS
                                                                   and b * c * 4 <= CAST_BLOCK_BYTES)))
    out = pl.pallas_call(
        _cast_kernel,
        grid=(n_rows // block_rows,),
        in_specs=[pl.BlockSpec((block_rows, c), lambda i: (i, 0))],
        out_specs=pl.BlockSpec((block_rows, c), lambda i: (i, 0)),
        out_shape=jax.ShapeDtypeStruct((n_rows, c), BF16),
        compiler_params=pltpu.CompilerParams(dimension_semantics=("arbitrary",),
                                             vmem_limit_bytes=VMEM_LIMIT_BYTES),
        name="cast_bf16",
    )(w.reshape(n_rows, c))
    return out.reshape(depth, r, c)


def _prepare_weights(g_mix, w_in, conv_w, conv_b, dt_bias, a_log, d_skip, g_ssd, sinks, w_att_out, w_ssd_out,
                     w_out, g_ffn, w_gate, w_up, w_down):
    pad_h = ((0, 0), (0, LANES - SSD_HEADS))
    w_packed, w_gates = _pack_w_in(w_in)
    return {
        "sinks": sinks,
        "g_mix": g_mix[:, None, :],
        "w_in": w_packed,
        "w_gates": w_gates,
        "conv_w": conv_w,
        "conv_b": conv_b[:, None, :],
        "dt_bias": jnp.pad(dt_bias, pad_h)[:, None, :],
        "a": jnp.pad(-jnp.exp(a_log), pad_h)[:, None, :],
        "d_skip": jnp.repeat(d_skip, SSD_HEAD_DIM, axis=1)[:, None, :],
        "g_ssd": g_ssd[:, None, :],
        "w_att_out": _cast_bf16(w_att_out),
        "w_ssd_out": _cast_bf16(w_ssd_out),
        "w_out": _cast_bf16(w_out),
        "g_ffn": g_ffn[:, None, :],
        "w_gate": _cast_bf16(w_gate),
        "w_up": _cast_bf16(w_up),
        "w_down": _cast_bf16(w_down),
    }


CAST_BLOCK_BYTES = 8 * 1024 * 1024
CAST_MIN_STEPS = 4
PROMPT_SEQS_PER_TILE = 4
PROMPT_CHUNKS_PER_TILE = 2
SAMPLE_SEQS_PER_TILE = 4
FFN_BLOCK_ROWS = 1024


def kernel(x_prompt, x_sample, cache_k, cache_v, state_conv, state_ssm, rel_table, g_mix, w_in, conv_w, conv_b, dt_bias, a_log, d_skip, g_ssd, sinks, w_att_out, w_ssd_out, w_out, g_ffn, w_gate, w_up, w_down, g_final):
    depth = w_in.shape[0]
    bp, sp, _ = x_prompt.shape
    bs, ts, _ = x_sample.shape
    kv_len = cache_k.shape[2]
    assert kv_len == WINDOW and ts <= CHUNK and ts % 8 == 0 and ts >= CONV_W - 1

    bias = _blocked_bias(rel_table)
    g_fin = g_final[None, :]

    wts = _prepare_weights(g_mix, w_in, conv_w, conv_b, dt_bias, a_log, d_skip, g_ssd, sinks, w_att_out,
                           w_ssd_out, w_out, g_ffn, w_gate, w_up, w_down)
    init = (cache_k.reshape(depth, bs, WINDOW, KV_WIDTH), cache_v.reshape(depth, bs, WINDOW, KV_WIDTH),
            state_conv, state_ssm.reshape(depth, bs, SSD_INNER, SSD_STATE))

    xp = x_prompt
    xs = x_sample.reshape(bs * ts, D_MODEL)
    st_p, st_s = [], []
    for l in range(depth):
        final = l == depth - 1
        hp, *state_p = _mixer(xp, bias, wts, l, None, n_seq=PROMPT_SEQS_PER_TILE,
                              n_chunk=PROMPT_CHUNKS_PER_TILE, nvalid=sp)
        xp = _ffn(hp.reshape(bp * sp, D_MODEL), wts, l, g_fin, block_rows=FFN_BLOCK_ROWS,
                  final=final).reshape(bp, sp, D_MODEL)
        st_p.append(state_p)
        xs_pad = jnp.pad(xs.reshape(bs, ts, D_MODEL), ((0, 0), (0, CHUNK - ts), (0, 0)))
        hs, *state_s = _mixer(xs_pad, bias, wts, l, init, n_seq=SAMPLE_SEQS_PER_TILE, n_chunk=1, nvalid=ts)
        xs = _ffn(hs[:, :ts].reshape(bs * ts, D_MODEL), wts, l, g_fin, block_rows=bs * ts, final=final)
        st_s.append(state_s)

    def states(sts, b):
        k, v, conv, ssm = (jnp.stack(leaves) for leaves in zip(*sts))
        return (k.reshape(depth, b, WINDOW, N_KV_HEADS, HEAD_DIM),
                v.reshape(depth, b, WINDOW, N_KV_HEADS, HEAD_DIM), conv,
                ssm.reshape(depth, b, SSD_HEADS, SSD_HEAD_DIM, SSD_STATE))

    return (xp, xs.reshape(x_sample.shape), *states(st_p, bp), *states(st_s, bs))
```

```python
import functools
import math

import jax
import jax.numpy as jnp
from jax import lax
from jax.experimental import pallas as pl
from jax.experimental.pallas import tpu as pltpu

D_MODEL = 1024
CHUNK = 64
EPS = 1e-6
NEG_INF = -1e30
N_HEADS = 8
N_KV_HEADS = 2
HEAD_DIM = 64
ATT_WIDTH = N_HEADS * HEAD_DIM
KV_WIDTH = N_KV_HEADS * HEAD_DIM
WINDOW = 128
N_KEYS = WINDOW + CHUNK
NUM_BUCKETS = 32
MAX_DISTANCE = 128
SSD_INNER = 1024
SSD_HEADS = 16
SSD_HEAD_DIM = 64
SSD_GROUPS = 2
SSD_STATE = 128
GROUP_WIDTH = SSD_INNER // SSD_GROUPS
CONV_W = 4
CONV_DIM = SSD_INNER + 2 * SSD_GROUPS * SSD_STATE
D_FF = 2816
LANES = 128
SUBLANES = 8
HIST_ROWS = 8
MXU_WIDTH = 256
XBC_BLOCKS = CONV_DIM // MXU_WIDTH
FILL_POINTS = 7
FILL_FIRST_PHASE = 3

OFF_Q = 0
OFF_KV = OFF_Q + ATT_WIDTH
OFF_XBC = OFF_KV + 2 * KV_WIDTH
OFF_DT = OFF_XBC + CONV_DIM
IN_PACKED = OFF_DT + LANES
GATE_Z = 0
GATE_A = GATE_Z + SSD_INNER
GATE_S = GATE_A + D_MODEL
GATE_WIDTH = GATE_S + D_MODEL
GATE_SLABS = 2
PACK_GROUP = 3
PACKED_BLOCKS = PACK_GROUP * -(-IN_PACKED // (PACK_GROUP * MXU_WIDTH))
GATE_BLOCKS = GATE_WIDTH // MXU_WIDTH

VMEM_LIMIT_BYTES = 60 * 1024 * 1024

F32 = jnp.float32
BF16 = jnp.bfloat16


def _dot(a, b):
    return jnp.dot(a, b, preferred_element_type=F32)


def _dot_nt(a, b):
    return lax.dot_general(a, b, (((1,), (1,)), ((), ())), preferred_element_type=F32)


def _split3(x):
    hi = x.astype(BF16)
    r1 = x - hi.astype(F32)
    mid = r1.astype(BF16)
    lo = (r1 - mid.astype(F32)).astype(BF16)
    return hi, mid, lo


def _rmsnorm(x, g):
    return x * lax.rsqrt(jnp.mean(x * x, axis=-1, keepdims=True) + EPS) * g


def _sigmoid(x):
    return 1.0 / (1.0 + jnp.exp(-x))


def _silu(x):
    return x * _sigmoid(x)


def _softplus(x):
    return jnp.maximum(x, 0.0) + jnp.log1p(jnp.exp(-jnp.abs(x)))


def _bias_kernel(table_ref, bucket_ref, o_ref):
    bucket = bucket_ref[...]
    rows = lax.broadcasted_iota(jnp.int32, bucket.shape, 0)
    cols = lax.broadcasted_iota(jnp.int32, bucket.shape, 1)
    row_hi = rows >= CHUNK
    col_hi = (cols & HEAD_DIM) != 0
    for g in range(N_KV_HEADS):
        acc = jnp.zeros(bucket.shape, F32)
        for b in range(NUM_BUCKETS):
            t0 = table_ref[b, 4 * g + 0]
            t1 = table_ref[b, 4 * g + 1]
            t2 = table_ref[b, 4 * g + 2]
            t3 = table_ref[b, 4 * g + 3]
            tv = jnp.where(row_hi, jnp.where(col_hi, t3, t2), jnp.where(col_hi, t1, t0))
            acc = jnp.where(bucket == b, tv, acc)
        o_ref[g] = acc


def _t5_bucket(rel):
    nb = NUM_BUCKETS // 2
    max_exact = nb // 2
    ret = jnp.where(rel > 0, nb, 0)
    n = jnp.abs(rel)
    nf = jnp.maximum(n, 1).astype(jnp.float32)
    large = max_exact + (jnp.log(nf / max_exact) / math.log(MAX_DISTANCE / max_exact)
                         * (nb - max_exact)).astype(jnp.int32)
    large = jnp.minimum(large, nb - 1)
    return ret + jnp.where(n < max_exact, n, large)


def _blocked_bias(rel_table):
    qi = jnp.arange(CHUNK, dtype=jnp.int32)
    kj = jnp.arange(N_KEYS, dtype=jnp.int32)
    bucket = _t5_bucket(kj[None, :] - WINDOW - qi[:, None]).astype(jnp.int32)
    bucket = jnp.tile(bucket.reshape(CHUNK, N_KEYS // CHUNK, 1, CHUNK), (2, 1, 2, 1)).reshape(2 * CHUNK, 2 * N_KEYS)
    return pl.pallas_call(
        _bias_kernel,
        out_shape=jax.ShapeDtypeStruct((N_KV_HEADS, 2 * CHUNK, 2 * N_KEYS), F32),
        in_specs=[pl.BlockSpec(memory_space=pltpu.SMEM),
                  pl.BlockSpec(memory_space=pltpu.VMEM)],
        out_specs=pl.BlockSpec(memory_space=pltpu.VMEM),
        name="rel_bias",
    )(rel_table, bucket)


def _blocked_kv(win, g):
    lo_half = lax.broadcasted_iota(jnp.int32, (N_KEYS, LANES), 1) < HEAD_DIM
    if g == 0:
        a0 = jnp.where(lo_half, win, 0.0)
        a1 = pltpu.roll(a0, HEAD_DIM, axis=1)
    else:
        a1 = jnp.where(lo_half, 0.0, win)
        a0 = pltpu.roll(a1, HEAD_DIM, axis=1)
    pieces = []
    for r in range(0, N_KEYS, CHUNK):
        pieces += [a0[r:r + CHUNK], a1[r:r + CHUNK]]
    return jnp.concatenate(pieces, axis=0).astype(BF16)


def _attention_logits(q, kwin, bias_ref, g):
    qs = jnp.concatenate([q[:, 256 * g:256 * g + LANES],
                          q[:, 256 * g + LANES:256 * (g + 1)]], axis=0).astype(BF16)
    return _dot_nt(qs, _blocked_kv(kwin, g)) + bias_ref[g]


def _attention_finish(logits, vwin, sinks_ref, valid, g):
    row_lo = lax.broadcasted_iota(jnp.int32, (2 * CHUNK, 1), 0) < CHUNK
    even = lax.broadcasted_iota(jnp.int32, (2 * CHUNK, LANES), 1) < HEAD_DIM
    logits = jnp.where(valid, logits, NEG_INF)
    tiles = [logits[:, c:c + LANES] for c in range(0, 2 * N_KEYS, LANES)]
    s0 = jnp.where(row_lo, sinks_ref[4 * g + 0], sinks_ref[4 * g + 2])
    s1 = jnp.where(row_lo, sinks_ref[4 * g + 1], sinks_ref[4 * g + 3])
    tmax = functools.reduce(jnp.maximum, tiles)
    m0 = jnp.maximum(jnp.max(jnp.where(even, tmax, -jnp.inf), axis=1, keepdims=True), s0)
    m1 = jnp.maximum(jnp.max(jnp.where(even, -jnp.inf, tmax), axis=1, keepdims=True), s1)
    m = jnp.where(even, m0, m1)
    p = [jnp.exp(tile - m) for tile in tiles]
    psum = functools.reduce(jnp.add, p)
    d0 = jnp.sum(jnp.where(even, psum, 0.0), axis=1, keepdims=True) + jnp.exp(s0 - m0)
    d1 = jnp.sum(jnp.where(even, 0.0, psum), axis=1, keepdims=True) + jnp.exp(s1 - m1)
    o = _dot(jnp.concatenate(p, axis=1).astype(BF16), _blocked_kv(vwin, g)) / jnp.where(even, d0, d1)
    return jnp.concatenate([o[:CHUNK], o[CHUNK:]], axis=1)


def _ssd_constants(n_blocks):
    r = lax.broadcasted_iota(jnp.int32, (n_blocks * CHUNK, n_blocks * CHUNK), 0)
    c = lax.broadcasted_iota(jnp.int32, (n_blocks * CHUNK, n_blocks * CHUNK), 1)
    tri = jnp.where((c <= r) & ((c >> 6) == (r >> 6)), 1.0, 0.0).astype(BF16)
    hrow = lax.broadcasted_iota(jnp.int32, (LANES, SSD_INNER), 0)
    hcol = lax.broadcasted_iota(jnp.int32, (LANES, SSD_INNER), 1)
    expand = jnp.where(hcol >> 6 == hrow, 1.0, 0.0).astype(BF16)
    return tri, expand


def _ssd_prepare(chunks, dt_raws, dtb_ref, a_ref, nvalid, tri, expand):
    n = len(chunks)
    dt = _softplus(jnp.concatenate(dt_raws, axis=0) + dtb_ref[...])
    if nvalid < CHUNK:
        rows = lax.broadcasted_iota(jnp.int32, dt.shape, 0) & (CHUNK - 1)
        dt = jnp.where(rows < nvalid, dt, 0.0)
    acum3 = _dot(tri, jnp.concatenate(_split3(dt * a_ref[...]), axis=1))
    acum = acum3[:, :LANES] + acum3[:, LANES:2 * LANES] + acum3[:, 2 * LANES:]
    a_hi, a_mid, _ = _split3(acum)
    wide = _dot(jnp.concatenate([dt.astype(BF16), a_hi, a_mid], axis=0), expand)
    m = n * CHUNK
    for i, c in enumerate(chunks):
        sl = slice(i * CHUNK, (i + 1) * CHUNK)
        c["dt_exp"] = wide[sl]
        c["a_col"] = wide[m + i * CHUNK:m + (i + 1) * CHUNK] + wide[2 * m + i * CHUNK:2 * m + (i + 1) * CHUNK]


def _ssd_decay(c):
    a_col = c["a_col"]
    a_last = a_col[CHUNK - 1:CHUNK, :]
    c["xs"] = c["xbc_act"][:, :SSD_INNER]
    c["xdt"] = c["xs"] * c.pop("dt_exp")
    c["xw"] = c["xdt"] * jnp.exp(a_last - a_col)
    c["e_col"] = jnp.exp(a_col)
    c["e_last"] = jnp.exp(a_last)


def _ssd_group(c, gq, ht_ref, g_idx):
    l2 = lax.broadcasted_iota(jnp.int32, (CHUNK, LANES), 0)
    j2 = lax.broadcasted_iota(jnp.int32, (CHUNK, LANES), 1)
    s2 = j2 & (CHUNK - 1)
    diag_sel = jnp.where(s2 == l2, 1.0, 0.0)
    causal2 = s2 <= l2
    lane_lo = j2 < SSD_HEAD_DIM

    b0 = SSD_INNER + SSD_STATE * gq
    c0 = SSD_INNER + SSD_GROUPS * SSD_STATE + SSD_STATE * gq
    bg = c["xbc_act"][:, b0:b0 + SSD_STATE]
    cg = c["xbc_act"][:, c0:c0 + SSD_STATE].astype(BF16)
    bg_bf = bg.astype(BF16)
    cb2 = _dot_nt(cg, jnp.concatenate([bg_bf, bg_bf], axis=0))
    gsl = slice(GROUP_WIDTH * gq, GROUP_WIDTH * (gq + 1))
    h_prev = ht_ref[g_idx, :, gsl]
    y_off = _dot(cg, h_prev.astype(BF16)) * c["e_col"][:, gsl]
    yd = []
    for i in range(GROUP_WIDTH // LANES):
        psl = slice(GROUP_WIDTH * gq + LANES * i, GROUP_WIDTH * gq + LANES * (i + 1))
        ac = c["a_col"][:, psl]
        a_row = jnp.sum(ac * diag_sel, axis=0, keepdims=True)
        lmat = jnp.exp(jnp.where(causal2, ac - a_row, -jnp.inf))
        m2 = (cb2 * lmat).astype(BF16)
        xp = c["xdt"][:, psl]
        xblk = jnp.concatenate([jnp.where(lane_lo, xp, 0.0), jnp.where(lane_lo, 0.0, xp)],
                               axis=0).astype(BF16)
        yd.append(_dot(m2, xblk))
    c.setdefault("ys", []).append(jnp.concatenate(yd, axis=1) + y_off)
    st = _dot(bg.T.astype(BF16), c["xw"][:, gsl].astype(BF16))
    ht_ref[g_idx, :, gsl] = h_prev * c["e_last"][:, gsl] + st


class _LayerSinks:
    def __init__(self, ref, layer):
        self.ref, self.layer = ref, layer

    def __getitem__(self, head):
        return self.ref[self.layer, head]


def _mixer_kernel(*refs, layer, n_seq, n_chunk, n_tiles, nvalid, has_init):
    it = iter(refs)
    x_ref = next(it)
    bias_ref = next(it)
    sinks_ref = _LayerSinks(next(it), layer)
    gmix_ref = next(it)
    win_ref = next(it)
    wgate_ref = next(it)
    convw_ref = next(it)
    convb_ref = next(it)
    dtb_ref = next(it)
    a_ref = next(it)
    dskip_ref = next(it)
    gssd_ref = next(it)
    watt_ref = next(it)
    wssd_ref = next(it)
    wout_ref = next(it)
    if has_init:
        ki_ref = next(it)
        vi_ref = next(it)
        convi_ref = next(it)
        ssmi_ref = next(it)
    h_ref = next(it)
    ko_ref = next(it)
    vo_ref = next(it)
    convo_ref = next(it)
    ssmo_ref = next(it)
    hn_s = next(it)
    q_s = next(it)
    kv_s = next(it)
    xbc_s = next(it)
    xact_s = next(it)
    dt_s = next(it)
    oatt_s = next(it)
    y_s = next(it)
    ht_s = next(it)
    gate_s = next(it)

    t = pl.program_id(1)
    seq_rows = n_chunk * CHUNK

    @pl.when(t == 0)
    def _init():
        if has_init:
            kv_s[:, 0:WINDOW, :KV_WIDTH] = ki_ref[...]
            kv_s[:, 0:WINDOW, KV_WIDTH:] = vi_ref[...]
            xbc_s[:, :, 0:HIST_ROWS, :] = jnp.zeros((XBC_BLOCKS, n_seq, HIST_ROWS, MXU_WIDTH), F32)
            for cb in range(XBC_BLOCKS):
                xbc_s[cb, :, HIST_ROWS - (CONV_W - 1):HIST_ROWS, :] = convi_ref[
                    :, :, cb * MXU_WIDTH:(cb + 1) * MXU_WIDTH]
            for g in range(n_seq):
                ht_s[g] = ssmi_ref[g].T
        else:
            kv_s[:, 0:WINDOW, :] = jnp.zeros((n_seq, WINDOW, 2 * KV_WIDTH), F32)
            xbc_s[:, :, 0:HIST_ROWS, :] = jnp.zeros((XBC_BLOCKS, n_seq, HIST_ROWS, MXU_WIDTH), F32)
            ht_s[...] = jnp.zeros(ht_s.shape, F32)

    rows = n_seq * seq_rows
    hn = _rmsnorm(x_ref[...].reshape(rows, D_MODEL), gmix_ref[...]).astype(BF16)
    hn_s[...] = hn

    def project_block(cb):
        c0 = OFF_XBC + cb * MXU_WIDTH
        blk = _dot(hn, win_ref[c0 // MXU_WIDTH])
        for g in range(n_seq):
            xbc_s[cb, g, HIST_ROWS:HIST_ROWS + seq_rows, :] = blk[g * seq_rows:(g + 1) * seq_rows]

    def conv_block(cb):
        csl = slice(cb * MXU_WIDTH, (cb + 1) * MXU_WIDTH)
        for g in range(n_seq):
            conv = convb_ref[:, csl] + convw_ref[CONV_W - 1:CONV_W, csl] * xbc_s[cb, g, HIST_ROWS:HIST_ROWS + seq_rows, :]
            for i in range(1, CONV_W):
                conv = conv + (convw_ref[CONV_W - 1 - i:CONV_W - i, csl]
                               * xbc_s[cb, g, HIST_ROWS - i:HIST_ROWS - i + seq_rows, :])
            xact_s[cb, g * seq_rows:(g + 1) * seq_rows, :] = _silu(conv)

    project_block(0)
    for cb in range(1, XBC_BLOCKS):
        project_block(cb)
        conv_block(cb - 1)
    for c0 in range(OFF_Q, OFF_KV, MXU_WIDTH):
        q_s[:, c0:c0 + MXU_WIDTH] = _dot(hn, win_ref[c0 // MXU_WIDTH]) * (HEAD_DIM ** -0.5)
    conv_block(XBC_BLOCKS - 1)
    kv = _dot(hn, win_ref[OFF_KV // MXU_WIDTH])
    for g in range(n_seq):
        kv_s[g, WINDOW:WINDOW + seq_rows, :] = kv[g * seq_rows:(g + 1) * seq_rows]
    dt_s[...] = _dot(hn, win_ref[OFF_DT // MXU_WIDTH])[:, :LANES]

    cols = lax.broadcasted_iota(jnp.int32, (2 * CHUNK, 2 * N_KEYS), 1)
    colmod = ((cols >> 7) << 6) + (cols & (CHUNK - 1))
    tri, expand = _ssd_constants(n_seq)

    slab = GATE_WIDTH // GATE_SLABS
    slabs_per_iter = GATE_SLABS // n_chunk

    def iter_body(j, carry):
        k0 = pl.multiple_of(j * CHUNK, CHUNK) if n_chunk > 1 else 0
        per_slab = slab // MXU_WIDTH
        n_pieces = slabs_per_iter * per_slab
        n_points = FILL_POINTS * n_seq
        calls = [0]

        first = FILL_FIRST_PHASE * n_seq
        span = n_points - first

        def fill():
            k = calls[0] - first
            calls[0] += 1
            if k < 0:
                return
            for p in range(-(-k * n_pieces // span), -(-(k + 1) * n_pieces // span)):
                s = j * slabs_per_iter + p // per_slab
                c0 = (p % per_slab) * MXU_WIDTH
                gate_s[s, :, c0:c0 + MXU_WIDTH] = _dot(hn_s[...], wgate_ref[s * per_slab + c0 // MXU_WIDTH])

        if has_init:
            valid = colmod < WINDOW + nvalid
        else:
            first_valid = jnp.maximum(0, (2 - (t * n_chunk + j)) * CHUNK)
            valid = colmod >= first_valid
        chunks = [{"g": g, "rows": pl.ds(pl.multiple_of(g * seq_rows + k0, CHUNK), CHUNK)}
                  for g in range(n_seq)]
        for c in chunks:
            kvwin = kv_s[c["g"], pl.ds(k0, N_KEYS), :]
            c["vwin"] = kvwin[:, KV_WIDTH:]
            q = q_s[c["rows"], :]
            c["logits"] = [_attention_logits(q, kvwin[:, :KV_WIDTH], bias_ref, hg) for hg in range(N_KV_HEADS)]
            fill()
        for c in chunks:
            c["xbc_act"] = jnp.concatenate([xact_s[cb, c["rows"], :] for cb in range(XBC_BLOCKS)], axis=1)
            fill()
        for c in chunks:
            outs = [_attention_finish(c["logits"][hg], c["vwin"], sinks_ref, valid, hg)
                    for hg in range(N_KV_HEADS)]
            oatt_s[c["rows"], :] = jnp.concatenate(outs, axis=1)
            del c["logits"], c["vwin"]
            fill()
        _ssd_prepare(chunks, [dt_s[c["rows"], :] for c in chunks], dtb_ref, a_ref, nvalid, tri, expand)
        for _ in chunks:
            fill()
        for c in chunks:
            _ssd_decay(c)
            fill()
        for gq in range(SSD_GROUPS):
            for c in chunks:
                _ssd_group(c, gq, ht_s, c["g"])
                fill()
        for c in chunks:
            y_s[c["rows"], :] = jnp.concatenate(c["ys"], axis=1) + c["xs"] * dskip_ref[...]
        assert calls[0] == n_points
        return carry

    for j in range(n_chunk):
        iter_body(j, 0)

    def gate_cols(lo, hi):
        pieces = []
        while lo < hi:
            s, off = divmod(lo, slab)
            take = min(hi - lo, slab - off)
            pieces.append(gate_s[s, :, off:off + take])
            lo += take
        return jnp.concatenate(pieces, axis=1)

    y_ssd = _rmsnorm(y_s[...] * _silu(gate_cols(GATE_Z, GATE_A)), gssd_ref[...]).astype(BF16)
    merged = (_sigmoid(gate_cols(GATE_A, GATE_S)) * _dot(oatt_s[...].astype(BF16), watt_ref[...])
              + _sigmoid(gate_cols(GATE_S, GATE_WIDTH)) * _dot(y_ssd, wssd_ref[...]))
    h = x_ref[...].reshape(rows, D_MODEL) + _dot(merged.astype(BF16), wout_ref[...])
    h_ref[...] = h.reshape(n_seq, seq_rows, D_MODEL)

    @pl.when(t == n_tiles - 1)
    def _emit_states():
        ko_ref[...] = kv_s[:, nvalid:nvalid + WINDOW, :KV_WIDTH]
        vo_ref[...] = kv_s[:, nvalid:nvalid + WINDOW, KV_WIDTH:]
        last = HIST_ROWS + nvalid
        convo_ref[...] = jnp.concatenate(
            [xbc_s[cb, :, last - (CONV_W - 1):last, :] for cb in range(XBC_BLOCKS)], axis=-1)
        for g in range(n_seq):
            ssmo_ref[g] = ht_s[g].T

    if n_tiles > 1:
        kv_s[:, 0:WINDOW, :] = kv_s[:, seq_rows:seq_rows + WINDOW, :]
        xbc_s[:, :, 0:HIST_ROWS, :] = xbc_s[:, :, seq_rows:seq_rows + HIST_ROWS, :]


def _layer_spec(shape, layer):
    nd = len(shape)
    return pl.BlockSpec((None,) + tuple(shape), lambda *_, _l=layer, _nd=nd: (_l,) + (0,) * _nd,
                        pipeline_mode=pl.Buffered(1))


def _mixer(x3d, bias, wts, layer, init, *, n_seq, n_chunk, nvalid):
    batch, seq_pad, _ = x3d.shape
    seq_rows = n_chunk * CHUNK
    n_tiles = seq_pad // seq_rows
    assert batch % n_seq == 0 and seq_pad % seq_rows == 0 and GATE_SLABS % n_chunk == 0
    rows = n_seq * seq_rows
    has_init = init is not None
    last_valid = nvalid - (n_tiles - 1) * seq_rows

    tile_map = lambda b, t: (b, t, 0)
    seq_map = lambda b, t: (b, 0, 0)
    cache_map = lambda b, t: (layer, b, 0, 0)
    spec = functools.partial(_layer_spec, layer=layer)
    in_specs = [
        pl.BlockSpec((n_seq, seq_rows, D_MODEL), tile_map),
        pl.BlockSpec(bias.shape, lambda b, t: (0, 0, 0), pipeline_mode=pl.Buffered(1)),
        pl.BlockSpec(memory_space=pltpu.SMEM),
        spec((1, D_MODEL)),
        spec((PACKED_BLOCKS, D_MODEL, MXU_WIDTH)),
        spec((GATE_BLOCKS, D_MODEL, MXU_WIDTH)),
        spec((CONV_W, CONV_DIM)),
        spec((1, CONV_DIM)),
        spec((1, LANES)),
        spec((1, LANES)),
        spec((1, SSD_INNER)),
        spec((1, SSD_INNER)),
        spec((ATT_WIDTH, D_MODEL)),
        spec((SSD_INNER, D_MODEL)),
        spec((D_MODEL, D_MODEL)),
    ]
    args = [x3d, bias, wts["sinks"], wts["g_mix"], wts["w_in"], wts["w_gates"], wts["conv_w"], wts["conv_b"],
            wts["dt_bias"], wts["a"], wts["d_skip"], wts["g_ssd"], wts["w_att_out"], wts["w_ssd_out"],
            wts["w_out"]]
    if has_init:
        in_specs += [
            pl.BlockSpec((None, n_seq, WINDOW, KV_WIDTH), cache_map),
            pl.BlockSpec((None, n_seq, WINDOW, KV_WIDTH), cache_map),
            pl.BlockSpec((None, n_seq, CONV_W - 1, CONV_DIM), cache_map),
            pl.BlockSpec((None, n_seq, SSD_INNER, SSD_STATE), cache_map),
        ]
        args += list(init)
    out_shape = [
        jax.ShapeDtypeStruct((batch, seq_pad, D_MODEL), F32),
        jax.ShapeDtypeStruct((batch, WINDOW, KV_WIDTH), F32),
        jax.ShapeDtypeStruct((batch, WINDOW, KV_WIDTH), F32),
        jax.ShapeDtypeStruct((batch, CONV_W - 1, CONV_DIM), F32),
        jax.ShapeDtypeStruct((batch, SSD_INNER, SSD_STATE), F32),
    ]
    out_specs = [
        pl.BlockSpec((n_seq, seq_rows, D_MODEL), tile_map),
        pl.BlockSpec((n_seq, WINDOW, KV_WIDTH), seq_map),
        pl.BlockSpec((n_seq, WINDOW, KV_WIDTH), seq_map),
        pl.BlockSpec((n_seq, CONV_W - 1, CONV_DIM), seq_map),
        pl.BlockSpec((n_seq, SSD_INNER, SSD_STATE), seq_map),
    ]
    scratch = [
        pltpu.VMEM((rows, D_MODEL), BF16),
        pltpu.VMEM((rows, ATT_WIDTH), F32),
        pltpu.VMEM((n_seq, WINDOW + seq_rows, 2 * KV_WIDTH), F32),
        pltpu.VMEM((XBC_BLOCKS, n_seq, HIST_ROWS + seq_rows, MXU_WIDTH), F32),
        pltpu.VMEM((XBC_BLOCKS, rows, MXU_WIDTH), F32),
        pltpu.VMEM((rows, LANES), F32),
        pltpu.VMEM((rows, ATT_WIDTH), F32),
        pltpu.VMEM((rows, SSD_INNER), F32),
        pltpu.VMEM((n_seq, SSD_STATE, SSD_INNER), F32),
        pltpu.VMEM((GATE_SLABS, rows, GATE_WIDTH // GATE_SLABS), F32),
    ]
    kern = functools.partial(_mixer_kernel, layer=layer, n_seq=n_seq, n_chunk=n_chunk, n_tiles=n_tiles,
                             nvalid=last_valid, has_init=has_init)
    return pl.pallas_call(
        kern,
        grid=(batch // n_seq, n_tiles),
        in_specs=in_specs,
        out_specs=out_specs,
        out_shape=out_shape,
        scratch_shapes=scratch,
        compiler_params=pltpu.CompilerParams(
            dimension_semantics=("arbitrary", "arbitrary"), vmem_limit_bytes=VMEM_LIMIT_BYTES),
        name="mixer_init" if has_init else "mixer",
    )(*args)


def _ffn_kernel(h_ref, g_ref, wg_ref, wu_ref, wd_ref, gfin_ref, o_ref, *, final):
    h = h_ref[...]
    hf = _rmsnorm(h, g_ref[...]).astype(BF16)
    act = (_silu(_dot(hf, wg_ref[...])) * _dot(hf, wu_ref[...])).astype(BF16)
    out = h + _dot(act, wd_ref[...])
    if final:
        out = _rmsnorm(out, gfin_ref[...])
    o_ref[...] = out


def _ffn(h2d, wts, layer, g_final, *, block_rows, final):
    n_rows = h2d.shape[0]
    assert n_rows % block_rows == 0
    spec = functools.partial(_layer_spec, layer=layer)
    return pl.pallas_call(
        functools.partial(_ffn_kernel, final=final),
        grid=(n_rows // block_rows,),
        in_specs=[
            pl.BlockSpec((block_rows, D_MODEL), lambda i: (i, 0)),
            spec((1, D_MODEL)),
            spec((D_MODEL, D_FF)),
            spec((D_MODEL, D_FF)),
            spec((D_FF, D_MODEL)),
            pl.BlockSpec((1, D_MODEL), lambda i: (0, 0), pipeline_mode=pl.Buffered(1)),
        ],
        out_specs=pl.BlockSpec((block_rows, D_MODEL), lambda i: (i, 0)),
        out_shape=jax.ShapeDtypeStruct((n_rows, D_MODEL), F32),
        compiler_params=pltpu.CompilerParams(
            dimension_semantics=("arbitrary",), vmem_limit_bytes=VMEM_LIMIT_BYTES),
        name="ffn_final" if final else "ffn",
    )(h2d, wts["g_ffn"], wts["w_gate"], wts["w_up"], wts["w_down"], g_final)


def _transpose_cast_kernel(src_ref, w_ref, o_ref, *, n_sub, keep_last):
    del src_ref
    t = w_ref[0].T
    if keep_last is not None:
        last = pl.program_id(1) == pl.num_programs(1) - 1
        lane = lax.broadcasted_iota(jnp.int32, t.shape, 1)
        t = jnp.where(lane < jnp.where(last, keep_last, t.shape[1]), t, 0.0)
    t = t.astype(BF16)
    for k in range(n_sub):
        o_ref[k] = t[:, k * MXU_WIDTH:(k + 1) * MXU_WIDTH]


def _transpose_cast(w_in_t, sources, n_sub, keep_last, name):
    depth, _, d = w_in_t.shape
    assert all(s % SUBLANES == 0 for s in sources)
    src = jnp.asarray([s // SUBLANES for s in sources], jnp.int32)
    grid_spec = pltpu.PrefetchScalarGridSpec(
        num_scalar_prefetch=1,
        grid=(depth, len(sources)),
        in_specs=[pl.BlockSpec((pl.Element(1), pl.Element(n_sub * MXU_WIDTH), pl.Element(d)),
                               lambda l, i, s: (l, s[i] * SUBLANES, 0))],
        out_specs=pl.BlockSpec((None, n_sub, d, MXU_WIDTH), lambda l, i, s: (l, i, 0, 0)),
    )
    return pl.pallas_call(
        functools.partial(_transpose_cast_kernel, n_sub=n_sub, keep_last=keep_last),
        grid_spec=grid_spec,
        out_shape=jax.ShapeDtypeStruct((depth, len(sources) * n_sub, d, MXU_WIDTH), BF16),
        compiler_params=pltpu.CompilerParams(dimension_semantics=("arbitrary", "arbitrary")),
        name=name,
    )(src, w_in_t)


def _pack_w_in(w_in):
    o_z = ATT_WIDTH + 2 * KV_WIDTH
    o_xbc = o_z + SSD_INNER
    o_dt = o_xbc + CONV_DIM
    o_ga = o_dt + SSD_HEADS
    o_gs = o_ga + D_MODEL
    w_in_t = jnp.swapaxes(w_in, 1, 2)
    group = PACK_GROUP * MXU_WIDTH
    assert o_z == group and CONV_DIM % group == 0 and PACKED_BLOCKS == (o_z + CONV_DIM + group) // MXU_WIDTH
    packed = _transpose_cast(w_in_t, [0, *range(o_xbc, o_dt, group), o_dt], PACK_GROUP, SSD_HEADS, "pack_w_in")
    gates = _transpose_cast(w_in_t, [o_z, o_ga, o_gs], D_MODEL // MXU_WIDTH, None, "pack_gates")
    return packed, gates


def _cast_kernel(w_ref, o_ref):
    o_ref[...] = w_ref[...].astype(BF16)


def _cast_bf16(w):
    depth, r, c = w.shape
    n_rows = depth * r
    block_rows = max(b for b in range(2 * SUBLANES, n_rows + 1, 2 * SUBLANES)
                     if n_rows % b == 0 and (b == 2 * SUBLANES or (n_rows // b >= CAST_MIN_STEPS
                                                                   and b * c * 4 <= CAST_BLOCK_BYTES)))
    out = pl.pallas_call(
        _cast_kernel,
        grid=(n_rows // block_rows,),
        in_specs=[pl.BlockSpec((block_rows, c), lambda i: (i, 0))],
        out_specs=pl.BlockSpec((block_rows, c), lambda i: (i, 0)),
        out_shape=jax.ShapeDtypeStruct((n_rows, c), BF16),
        compiler_params=pltpu.CompilerParams(dimension_semantics=("arbitrary",),
                                             vmem_limit_bytes=VMEM_LIMIT_BYTES),
        name="cast_bf16",
    )(w.reshape(n_rows, c))
    return out.reshape(depth, r, c)


def _prepare_weights(g_mix, w_in, conv_w, conv_b, dt_bias, a_log, d_skip, g_ssd, sinks, w_att_out, w_ssd_out,
                     w_out, g_ffn, w_gate, w_up, w_down):
    pad_h = ((0, 0), (0, LANES - SSD_HEADS))
    w_packed, w_gates = _pack_w_in(w_in)
    return {
        "sinks": sinks,
        "g_mix": g_mix[:, None, :],
        "w_in": w_packed,
        "w_gates": w_gates,
        "conv_w": conv_w,
        "conv_b": conv_b[:, None, :],
        "dt_bias": jnp.pad(dt_bias, pad_h)[:, None, :],
        "a": jnp.pad(-jnp.exp(a_log), pad_h)[:, None, :],
        "d_skip": jnp.repeat(d_skip, SSD_HEAD_DIM, axis=1)[:, None, :],
        "g_ssd": g_ssd[:, None, :],
        "w_att_out": _cast_bf16(w_att_out),
        "w_ssd_out": _cast_bf16(w_ssd_out),
        "w_out": _cast_bf16(w_out),
        "g_ffn": g_ffn[:, None, :],
        "w_gate": _cast_bf16(w_gate),
        "w_up": _cast_bf16(w_up),
        "w_down": _cast_bf16(w_down),
    }


CAST_BLOCK_BYTES = 8 * 1024 * 1024
CAST_MIN_STEPS = 2
PROMPT_SEQS_PER_TILE = 4
PROMPT_CHUNKS_PER_TILE = 2
SAMPLE_SEQS_PER_TILE = 4
FFN_BLOCK_ROWS = 1024


def kernel(x_prompt, x_sample, cache_k, cache_v, state_conv, state_ssm, rel_table, g_mix, w_in, conv_w, conv_b, dt_bias, a_log, d_skip, g_ssd, sinks, w_att_out, w_ssd_out, w_out, g_ffn, w_gate, w_up, w_down, g_final):
    depth = w_in.shape[0]
    bp, sp, _ = x_prompt.shape
    bs, ts, _ = x_sample.shape
    kv_len = cache_k.shape[2]
    assert kv_len == WINDOW and ts <= CHUNK and ts % 8 == 0 and ts >= CONV_W - 1

    bias = _blocked_bias(rel_table)
    g_fin = g_final[None, :]

    wts = _prepare_weights(g_mix, w_in, conv_w, conv_b, dt_bias, a_log, d_skip, g_ssd, sinks, w_att_out,
                           w_ssd_out, w_out, g_ffn, w_gate, w_up, w_down)
    init = (cache_k.reshape(depth, bs, WINDOW, KV_WIDTH), cache_v.reshape(depth, bs, WINDOW, KV_WIDTH),
            state_conv, state_ssm.reshape(depth, bs, SSD_INNER, SSD_STATE))

    xp = x_prompt
    xs = x_sample.reshape(bs * ts, D_MODEL)
    st_p, st_s = [], []
    for l in range(depth):
        final = l == depth - 1
        hp, *state_p = _mixer(xp, bias, wts, l, None, n_seq=PROMPT_SEQS_PER_TILE,
                              n_chunk=PROMPT_CHUNKS_PER_TILE, nvalid=sp)
        xp = _ffn(hp.reshape(bp * sp, D_MODEL), wts, l, g_fin, block_rows=FFN_BLOCK_ROWS,
                  final=final).reshape(bp, sp, D_MODEL)
        st_p.append(state_p)
        xs_pad = jnp.pad(xs.reshape(bs, ts, D_MODEL), ((0, 0), (0, CHUNK - ts), (0, 0)))
        hs, *state_s = _mixer(xs_pad, bias, wts, l, init, n_seq=SAMPLE_SEQS_PER_TILE, n_chunk=1, nvalid=ts)
        xs = _ffn(hs[:, :ts].reshape(bs * ts, D_MODEL), wts, l, g_fin, block_rows=bs * ts, final=final)
        st_s.append(state_s)

    def states(sts, b):
        k, v, conv, ssm = (jnp.stack(leaves) for leaves in zip(*sts))
        return (k.reshape(depth, b, WINDOW, N_KV_HEADS, HEAD_DIM),
                v.reshape(depth, b, WINDOW, N_KV_HEADS, HEAD_DIM), conv,
                ssm.reshape(depth, b, SSD_HEADS, SSD_HEAD_DIM, SSD_STATE))

    return (xp, xs.reshape(x_sample.shape), *states(st_p, bp), *states(st_s, bs))
```
